```python
import jax, jax.numpy as jnp
from jax import lax
import numpy as np

D_MODEL = 1024
BATCH = 8
SEQ = 2048
DEPTH = 2
DEC_BATCH = 128
DEC_SEQ = 8
PAST_LEN = 16384
PAGE_SIZE = 128

N_META = 16
EPS = 1e-6
D_FF = 2816
N_BRANCH = 4
BRANCH_WIDTH = 256

DN_HEADS = 4
DN_DK = 64
DN_DV = 64
DN_QK = DN_HEADS * DN_DK
DN_VW = DN_HEADS * DN_DV
DN_CONV = 4
DN_CHUNK = 64

S5_WIDTH = 256
S5_GROUP = 16
S5_GROUPS = S5_WIDTH // S5_GROUP
S5_STATE = 64

LRU_WIDTH = 256
LRU_BLOCKS = 4
LRU_BLOCK = LRU_WIDTH // LRU_BLOCKS
LRU_CONV = 4
LRU_C = 8.0

CV_WIDTH = 256
CV_KERNEL = 31

IN_SIZES = (DN_QK, DN_QK, DN_VW, DN_VW, DN_HEADS, DN_HEADS, S5_WIDTH, LRU_WIDTH, LRU_WIDTH, CV_WIDTH, CV_WIDTH, N_BRANCH * D_MODEL)
N_IN = 2 * DN_QK + 2 * DN_VW + 2 * DN_HEADS + S5_WIDTH + 2 * LRU_WIDTH + 2 * CV_WIDTH + N_BRANCH * D_MODEL

kernel_name = 'hybrid_gated_parallel_decoder_step'


def rmsnorm(x, g):
    xf = x.astype(jnp.float32)
    y = xf * lax.rsqrt(jnp.mean(xf * xf, axis=-1, keepdims=True) + EPS)
    return (y * g.astype(jnp.float32)).astype(x.dtype)


def layernorm(x, g, b):
    xf = x.astype(jnp.float32)
    xc = xf - jnp.mean(xf, axis=-1, keepdims=True)
    y = xc * lax.rsqrt(jnp.mean(xc * xc, axis=-1, keepdims=True) + EPS)
    return (y * g.astype(jnp.float32) + b.astype(jnp.float32)).astype(x.dtype)


def l2norm(x):
    return x * lax.rsqrt(jnp.sum(x * x, axis=-1, keepdims=True) + EPS)


def swiglu(x, w_gu, w_down):
    gate, up = jnp.split(x @ w_gu, 2, axis=-1)
    return (jax.nn.silu(gate) * up) @ w_down


def causal_dwconv(buf, x, w):
    k_width, ch = w.shape
    xf = jnp.concatenate([buf.astype(x.dtype), x], axis=1)
    y = lax.conv_general_dilated(xf, w[:, None, :].astype(x.dtype), window_strides=(1,), padding='VALID',
                                 dimension_numbers=('NWC', 'WIO', 'NWC'), feature_group_count=ch)
    return y, xf[:, xf.shape[1] - (k_width - 1):]


def linear_scan(a, b):
    def combine(e1, e2):
        a1, b1 = e1
        a2, b2 = e2
        return a1 * a2, a2 * b1 + b2
    return lax.associative_scan(combine, (a, b), axis=1)[1]


def to_chunks(a, n, c):
    bsz, _, h = a.shape[:3]
    a = a.reshape(bsz, n, c, h, *a.shape[3:])
    return jnp.moveaxis(jnp.moveaxis(a, 1, 0), 3, 2)


def gated_delta_rule(q, k, v, g, beta, s0):
    bsz, t, h, dk = q.shape
    dv = v.shape[-1]
    c = min(DN_CHUNK, t)
    pad = (-t) % c
    n = (t + pad) // c

    def lpad(a):
        return jnp.pad(a, [(0, 0), (pad, 0)] + [(0, 0)] * (a.ndim - 2))

    qc = to_chunks(lpad(q * dk ** -0.5), n, c)
    kc = to_chunks(lpad(k), n, c)
    vc = to_chunks(lpad(v), n, c)
    bc = to_chunks(lpad(beta), n, c)
    gc = jnp.cumsum(to_chunks(lpad(g), n, c), axis=-1)
    causal = jnp.tril(jnp.ones((c, c), bool))
    strict = jnp.tril(jnp.ones((c, c), bool), -1)
    diff = gc[..., :, None] - gc[..., None, :]
    decay = jnp.where(causal, jnp.exp(jnp.where(causal, diff, 0.0)), 0.0)
    kb = kc * bc[..., None]
    lower = jnp.where(strict, jnp.einsum('nbhid,nbhjd->nbhij', kb, kc) * decay, 0.0)
    rhs = jnp.concatenate([vc * bc[..., None], kb * jnp.exp(gc)[..., None]], axis=-1)
    sol = lax.linalg.triangular_solve(lower + jnp.eye(c, dtype=q.dtype), rhs, left_side=True, lower=True)
    u_c, w_c = sol[..., :dv], sol[..., dv:]
    intra = jnp.where(causal, jnp.einsum('nbhid,nbhjd->nbhij', qc, kc) * decay, 0.0)

    def step(s, inp):
        qi, ki, ui, wi, ai, gi = inp
        v_new = ui - jnp.einsum('bhck,bhkv->bhcv', wi, s)
        o = (jnp.einsum('bhck,bhkv->bhcv', qi * jnp.exp(gi)[..., None], s)
             + jnp.einsum('bhij,bhjv->bhiv', ai, v_new))
        g_last = gi[..., -1]
        k_dec = ki * jnp.exp(g_last[..., None] - gi)[..., None]
        s = s * jnp.exp(g_last)[..., None, None] + jnp.einsum('bhck,bhcv->bhkv', k_dec, v_new)
        return s, o

    s_fin, o = lax.scan(step, s0, (qc, kc, u_c, w_c, intra, gc))
    o = jnp.moveaxis(jnp.moveaxis(o, 2, 3), 0, 1).reshape(bsz, n * c, h, dv)[:, pad:]
    return o, s_fin


def delta_branch(q_in, k_in, v_in, z, b_in, a_in, conv_buf, s0, conv_w, a_log, dt_bias, norm_g):
    bsz, t, _ = q_in.shape
    f32 = jnp.float32
    qkv, new_buf = causal_dwconv(conv_buf, jnp.concatenate([q_in, k_in, v_in], axis=-1), conv_w)
    qkv = jax.nn.silu(qkv.astype(f32))
    q, k, v = jnp.split(qkv, [DN_QK, 2 * DN_QK], axis=-1)
    q = l2norm(q.reshape(bsz, t, DN_HEADS, DN_DK))
    k = l2norm(k.reshape(bsz, t, DN_HEADS, DN_DK))
    v = v.reshape(bsz, t, DN_HEADS, DN_DV)
    beta = jax.nn.sigmoid(b_in.astype(f32))
    g = -jnp.exp(a_log.astype(f32)) * jax.nn.softplus(a_in.astype(f32) + dt_bias.astype(f32))
    o, s_new = gated_delta_rule(q, k, v, g, beta, s0.astype(f32))
    o = o * lax.rsqrt(jnp.mean(o * o, axis=-1, keepdims=True) + EPS) * norm_g.astype(f32)
    o = o * jax.nn.silu(z.astype(f32).reshape(bsz, t, DN_HEADS, DN_DV))
    return o.reshape(bsz, t, DN_VW).astype(q_in.dtype), new_buf, s_new.astype(s0.dtype)


def s5_branch(u, h0_re, h0_im, lam_re, lam_im, log_step, b_re, b_im, c_re, c_im, d_skip, w_glu, b_glu):
    bsz, t, _ = u.shape
    f32 = jnp.float32
    uf = u.astype(f32).reshape(bsz, t, S5_GROUPS, S5_GROUP)
    lam_re = lam_re.astype(f32)
    lam_im = lam_im.astype(f32)
    dt = jnp.exp(log_step.astype(f32))[:, None]
    mag = jnp.exp(lam_re * dt)
    lb_re = mag * jnp.cos(lam_im * dt)
    lb_im = mag * jnp.sin(lam_im * dt)
    den = lam_re * lam_re + lam_im * lam_im
    cf_re = ((lb_re - 1.0) * lam_re + lb_im * lam_im) / den
    cf_im = (lb_im * lam_re - (lb_re - 1.0) * lam_im) / den
    bu_re = jnp.einsum('btgc,gpc->btgp', uf, b_re.astype(f32))
    bu_im = jnp.einsum('btgc,gpc->btgp', uf, b_im.astype(f32))
    x_re = cf_re * bu_re - cf_im * bu_im
    x_im = cf_re * bu_im + cf_im * bu_re
    h0r = h0_re.astype(f32)
    h0i = h0_im.astype(f32)
    x_re = x_re.at[:, 0].add(lb_re * h0r - lb_im * h0i)
    x_im = x_im.at[:, 0].add(lb_re * h0i + lb_im * h0r)
    a_re = jnp.broadcast_to(lb_re, x_re.shape)
    a_im = jnp.broadcast_to(lb_im, x_im.shape)

    def combine(e1, e2):
        a1r, a1i, b1r, b1i = e1
        a2r, a2i, b2r, b2i = e2
        return (a1r * a2r - a1i * a2i, a1r * a2i + a1i * a2r,
                a2r * b1r - a2i * b1i + b2r, a2r * b1i + a2i * b1r + b2i)

    _, _, h_re, h_im = lax.associative_scan(combine, (a_re, a_im, x_re, x_im), axis=1)
    y = (jnp.einsum('btgp,gcp->btgc', h_re, c_re.astype(f32)) - jnp.einsum('btgp,gcp->btgc', h_im, c_im.astype(f32))
         + d_skip.astype(f32).reshape(S5_GROUPS, S5_GROUP) * uf)
    y = jax.nn.gelu(y.reshape(bsz, t, S5_WIDTH))
    ga, gb = jnp.split(y @ w_glu.astype(f32) + b_glu.astype(f32), 2, axis=-1)
    out = ga * jax.nn.sigmoid(gb)
    return out.astype(u.dtype), h_re[:, -1].astype(h0_re.dtype), h_im[:, -1].astype(h0_im.dtype)


def lru_branch(x_in, gate_in, conv_buf, h0, conv_w, conv_b, w_a, b_a, w_x, b_x, lam):
    bsz, t, _ = x_in.shape
    f32 = jnp.float32
    xc, new_buf = causal_dwconv(conv_buf, x_in, conv_w)
    xf = xc.astype(f32) + conv_b.astype(f32)
    xb = xf.reshape(bsz, t, LRU_BLOCKS, LRU_BLOCK)
    r = jax.nn.sigmoid(jnp.einsum('btnc,ncd->btnd', xb, w_a.astype(f32)).reshape(bsz, t, LRU_WIDTH) + b_a.astype(f32))
    i = jax.nn.sigmoid(jnp.einsum('btnc,ncd->btnd', xb, w_x.astype(f32)).reshape(bsz, t, LRU_WIDTH) + b_x.astype(f32))
    log_a = -LRU_C * r * jax.nn.softplus(-lam.astype(f32))
    a = jnp.exp(log_a)
    b = jnp.sqrt(-jnp.expm1(2.0 * log_a)) * (i * xf)
    b = b.at[:, 0].add(a[:, 0] * h0.astype(f32))
    h = linear_scan(a, b)
    out = h * jax.nn.gelu(gate_in.astype(f32))
    return out.astype(x_in.dtype), new_buf, h[:, -1].astype(h0.dtype)


def conv_branch(val, gate, conv_buf, conv_w, conv_b, ln_g, ln_b):
    glu = val * jax.nn.sigmoid(gate)
    y, new_buf = causal_dwconv(conv_buf, glu, conv_w)
    y = layernorm(y + conv_b.astype(y.dtype), ln_g, ln_b)
    return jax.nn.silu(y), new_buf


def decoder_layer(x, st, p):
    s_dn, s_dnc, s_re, s_im, s_lru, s_lruc, s_cv = st
    x = x + 0.5 * swiglu(rmsnorm(x, p['ffn1_norm']), p['ffn1_w_gu'], p['ffn1_w_down'])
    u = rmsnorm(x, p['mix_norm'])
    z = u @ p['w_in']
    dq, dk, dv, dz, db, da, su, lx, lg, cval, cgate, zg = jnp.split(z, np.cumsum(IN_SIZES)[:-1].tolist(), axis=-1)
    oa, n_dnc, n_dn = delta_branch(dq, dk, dv, dz, db, da, s_dnc, s_dn, p['dn_conv_w'], p['dn_a_log'],
                                   p['dn_dt_bias'], p['dn_norm'])
    ob, n_re, n_im = s5_branch(su, s_re, s_im, p['s5_lam_re'], p['s5_lam_im'], p['s5_log_step'], p['s5_b_re'],
                               p['s5_b_im'], p['s5_c_re'], p['s5_c_im'], p['s5_d'], p['s5_w_glu'], p['s5_b_glu'])
    oc, n_lruc, n_lru = lru_branch(lx, lg, s_lruc, s_lru, p['lru_conv_w'], p['lru_conv_b'], p['lru_w_a'],
                                   p['lru_b_a'], p['lru_w_x'], p['lru_b_x'], p['lru_lam'])
    od, n_cv = conv_branch(cval, cgate, s_cv, p['cv_conv_w'], p['cv_conv_b'], p['cv_ln_g'], p['cv_ln_b'])
    branches = jnp.stack([oa, ob, oc, od], axis=-2)
    proj = jnp.einsum('btnc,ncd->btnd', branches, p['w_branch'])
    gates = jax.nn.sigmoid(zg.reshape(*zg.shape[:-1], N_BRANCH, D_MODEL))
    x = x + jnp.sum(gates * proj, axis=-2) @ p['w_out']
    x = x + 0.5 * swiglu(rmsnorm(x, p['ffn2_norm']), p['ffn2_w_gu'], p['ffn2_w_down'])
    return x, (n_dn, n_dnc, n_re, n_im, n_lru, n_lruc, n_cv)


def empty_state(batch, dtype):
    return (jnp.zeros((batch, DN_HEADS, DN_DK, DN_DV), dtype),
            jnp.zeros((batch, DN_CONV - 1, 2 * DN_QK + DN_VW), dtype),
            jnp.zeros((batch, S5_GROUPS, S5_STATE), dtype),
            jnp.zeros((batch, S5_GROUPS, S5_STATE), dtype),
            jnp.zeros((batch, LRU_WIDTH), dtype),
            jnp.zeros((batch, LRU_CONV - 1, LRU_WIDTH), dtype),
            jnp.zeros((batch, CV_KERNEL - 1, CV_WIDTH), dtype))


def setup_inputs(seed: int = 0) -> dict:
    key = jax.random.key(seed)
    ks = iter(jax.random.split(key, 64))

    def nrm(shape, scale):
        return scale * jax.random.normal(next(ks), shape, jnp.float32)

    def unif(shape, lo, hi):
        return jax.random.uniform(next(ks), shape, jnp.float32, lo, hi)

    L = DEPTH
    cw = 2 * DN_QK + DN_VW
    dt = jnp.exp(unif((L, DN_HEADS), float(np.log(1e-3)), float(np.log(1e-1))))
    a0 = unif((L, LRU_WIDTH), 0.9, 0.999) ** (1.0 / LRU_C)
    lam_im = jnp.broadcast_to(jnp.pi * jnp.arange(S5_STATE, dtype=jnp.float32), (L, S5_GROUPS, S5_STATE))
    return {
        'x_prompt': nrm((BATCH, SEQ, D_MODEL), 1.0),
        'x_sample': nrm((DEC_BATCH, DEC_SEQ, D_MODEL), 1.0),
        'state_delta': nrm((L, DEC_BATCH, DN_HEADS, DN_DK, DN_DV), 0.1),
        'state_delta_conv': nrm((L, DEC_BATCH, DN_CONV - 1, cw), 1.0),
        'state_s5_re': nrm((L, DEC_BATCH, S5_GROUPS, S5_STATE), 0.1),
        'state_s5_im': nrm((L, DEC_BATCH, S5_GROUPS, S5_STATE), 0.1),
        'state_lru': nrm((L, DEC_BATCH, LRU_WIDTH), 0.5),
        'state_lru_conv': nrm((L, DEC_BATCH, LRU_CONV - 1, LRU_WIDTH), 1.0),
        'state_conv': nrm((L, DEC_BATCH, CV_KERNEL - 1, CV_WIDTH), 0.5),
        'meta_tokens': nrm((N_META, D_MODEL), 1.0),
        'ffn1_norm': 1.0 + nrm((L, D_MODEL), 0.01),
        'ffn1_w_gu': nrm((L, D_MODEL, 2 * D_FF), D_MODEL ** -0.5),
        'ffn1_w_down': nrm((L, D_FF, D_MODEL), D_FF ** -0.5),
        'mix_norm': 1.0 + nrm((L, D_MODEL), 0.01),
        'w_in': nrm((L, D_MODEL, N_IN), D_MODEL ** -0.5),
        'dn_conv_w': nrm((L, DN_CONV, cw), 0.5),
        'dn_a_log': jnp.log(unif((L, DN_HEADS), 1.0, 16.0)),
        'dn_dt_bias': dt + jnp.log(-jnp.expm1(-dt)),
        'dn_norm': 1.0 + nrm((L, DN_DV), 0.01),
        's5_lam_re': -0.5 + nrm((L, S5_GROUPS, S5_STATE), 0.01),
        's5_lam_im': lam_im + nrm((L, S5_GROUPS, S5_STATE), 0.01),
        's5_log_step': unif((L, S5_GROUPS), float(np.log(1e-3)), float(np.log(1e-1))),
        's5_b_re': nrm((L, S5_GROUPS, S5_STATE, S5_GROUP), (2.0 * S5_GROUP) ** -0.5),
        's5_b_im': nrm((L, S5_GROUPS, S5_STATE, S5_GROUP), (2.0 * S5_GROUP) ** -0.5),
        's5_c_re': nrm((L, S5_GROUPS, S5_GROUP, S5_STATE), 0.5),
        's5_c_im': nrm((L, S5_GROUPS, S5_GROUP, S5_STATE), 0.5),
        's5_d': nrm((L, S5_WIDTH), 1.0),
        's5_w_glu': nrm((L, S5_WIDTH, 2 * S5_WIDTH), S5_WIDTH ** -0.5),
        's5_b_glu': nrm((L, 2 * S5_WIDTH), 0.01),
        'lru_conv_w': nrm((L, LRU_CONV, LRU_WIDTH), 0.5),
        'lru_conv_b': nrm((L, LRU_WIDTH), 0.01),
        'lru_w_a': nrm((L, LRU_BLOCKS, LRU_BLOCK, LRU_BLOCK), LRU_BLOCK ** -0.5),
        'lru_b_a': nrm((L, LRU_WIDTH), 0.01),
        'lru_w_x': nrm((L, LRU_BLOCKS, LRU_BLOCK, LRU_BLOCK), LRU_BLOCK ** -0.5),
        'lru_b_x': nrm((L, LRU_WIDTH), 0.01),
        'lru_lam': jnp.log(a0) - jnp.log1p(-a0),
        'cv_conv_w': nrm((L, CV_KERNEL, CV_WIDTH), CV_KERNEL ** -0.5),
        'cv_conv_b': nrm((L, CV_WIDTH), 0.01),
        'cv_ln_g': 1.0 + nrm((L, CV_WIDTH), 0.01),
        'cv_ln_b': nrm((L, CV_WIDTH), 0.01),
        'w_branch': nrm((L, N_BRANCH, BRANCH_WIDTH, D_MODEL), BRANCH_WIDTH ** -0.5),
        'w_out': nrm((L, D_MODEL, D_MODEL), D_MODEL ** -0.5),
        'ffn2_norm': 1.0 + nrm((L, D_MODEL), 0.01),
        'ffn2_w_gu': nrm((L, D_MODEL, 2 * D_FF), D_MODEL ** -0.5),
        'ffn2_w_down': nrm((L, D_FF, D_MODEL), D_FF ** -0.5),
        'final_norm': 1.0 + nrm((D_MODEL,), 0.01),
    }


def stack_layers(states, i):
    return jnp.stack([st[i] for st in states], axis=0)


def reference(x_prompt, x_sample, state_delta, state_delta_conv, state_s5_re, state_s5_im, state_lru,
              state_lru_conv, state_conv, meta_tokens, ffn1_norm, ffn1_w_gu, ffn1_w_down, mix_norm, w_in,
              dn_conv_w, dn_a_log, dn_dt_bias, dn_norm, s5_lam_re, s5_lam_im, s5_log_step, s5_b_re, s5_b_im,
              s5_c_re, s5_c_im, s5_d, s5_w_glu, s5_b_glu, lru_conv_w, lru_conv_b, lru_w_a, lru_b_a, lru_w_x,
              lru_b_x, lru_lam, cv_conv_w, cv_conv_b, cv_ln_g, cv_ln_b, w_branch, w_out, ffn2_norm, ffn2_w_gu,
              ffn2_w_down, final_norm):
    bp = x_prompt.shape[0]
    meta = jnp.broadcast_to(meta_tokens.astype(x_prompt.dtype)[None], (bp, N_META, D_MODEL))
    xp = jnp.concatenate([meta, x_prompt], axis=1)
    xs = x_sample
    prompt_new = []
    sample_new = []
    for l in range(DEPTH):
        p = {'ffn1_norm': ffn1_norm[l], 'ffn1_w_gu': ffn1_w_gu[l], 'ffn1_w_down': ffn1_w_down[l],
             'mix_norm': mix_norm[l], 'w_in': w_in[l], 'dn_conv_w': dn_conv_w[l], 'dn_a_log': dn_a_log[l],
             'dn_dt_bias': dn_dt_bias[l], 'dn_norm': dn_norm[l], 's5_lam_re': s5_lam_re[l],
             's5_lam_im': s5_lam_im[l], 's5_log_step': s5_log_step[l], 's5_b_re': s5_b_re[l],
             's5_b_im': s5_b_im[l], 's5_c_re': s5_c_re[l], 's5_c_im': s5_c_im[l], 's5_d': s5_d[l],
             's5_w_glu': s5_w_glu[l], 's5_b_glu': s5_b_glu[l], 'lru_conv_w': lru_conv_w[l],
             'lru_conv_b': lru_conv_b[l], 'lru_w_a': lru_w_a[l], 'lru_b_a': lru_b_a[l], 'lru_w_x': lru_w_x[l],
             'lru_b_x': lru_b_x[l], 'lru_lam': lru_lam[l], 'cv_conv_w': cv_conv_w[l], 'cv_conv_b': cv_conv_b[l],
             'cv_ln_g': cv_ln_g[l], 'cv_ln_b': cv_ln_b[l], 'w_branch': w_branch[l], 'w_out': w_out[l],
             'ffn2_norm': ffn2_norm[l], 'ffn2_w_gu': ffn2_w_gu[l], 'ffn2_w_down': ffn2_w_down[l]}
        xp, st_p = decoder_layer(xp, empty_state(bp, xp.dtype), p)
        xs, st_s = decoder_layer(xs, (state_delta[l], state_delta_conv[l], state_s5_re[l], state_s5_im[l],
                                      state_lru[l], state_lru_conv[l], state_conv[l]), p)
        prompt_new.append(st_p)
        sample_new.append(st_s)
    y_prompt = rmsnorm(xp[:, N_META:], final_norm)
    y_sample = rmsnorm(xs, final_norm)
    p_delta = stack_layers(prompt_new, 0)
    p_delta_conv = stack_layers(prompt_new, 1)
    p_s5_re = stack_layers(prompt_new, 2)
    p_s5_im = stack_layers(prompt_new, 3)
    p_lru = stack_layers(prompt_new, 4)
    p_lru_conv = stack_layers(prompt_new, 5)
    p_conv = stack_layers(prompt_new, 6)
    s_delta = stack_layers(sample_new, 0)
    s_delta_conv = stack_layers(sample_new, 1)
    s_s5_re = stack_layers(sample_new, 2)
    s_s5_im = stack_layers(sample_new, 3)
    s_lru = stack_layers(sample_new, 4)
    s_lru_conv = stack_layers(sample_new, 5)
    s_conv = stack_layers(sample_new, 6)
    return (y_prompt, y_sample, p_delta, p_delta_conv, p_s5_re, p_s5_im, p_lru, p_lru_conv, p_conv,
            s_delta, s_delta_conv, s_s5_re, s_s5_im, s_lru, s_lru_conv, s_conv)
```

```python
import functools

import jax
import jax.numpy as jnp
from jax import lax
from jax.experimental import pallas as pl
from jax.experimental.pallas import tpu as pltpu
from jax.scipy.linalg import block_diag

F32 = jnp.float32
BF16 = jnp.bfloat16

D_MODEL = 1024
D_FF = 2816
N_META = 16
EPS = 1e-6
HEADS = 4
DK = 64
QKV_W = 768
CHUNK = 64
S5_STATES = 1024
LRU_C = 8.0
CV_K = 31
LRU_K = 4
DN_K = 4

FF_CHUNK = 256
FF_NCHUNK = D_FF // FF_CHUNK
MIX_COLS = 2816
COL_DZ_BLK = 3
COL_LRU_BLK = 2
COL_CV_BLK = 3
COL_S5_BLK = 8
COL_BETA_BLK = 9
COL_DECAY_BLK = 10

VMEM_LIMIT = 56 * 1024 * 1024


def _const_spec(shape):
    nd = len(shape)
    return pl.BlockSpec(shape, lambda *_: (0,) * nd, pipeline_mode=pl.Buffered(1))


def _bdot(a, b):
    return jnp.dot(a.astype(BF16), b.astype(BF16), preferred_element_type=F32)


def _hdot(a, b):
    return jnp.dot(a, b, precision=lax.Precision.HIGHEST, preferred_element_type=F32)


def _rms(x, g):
    return x * lax.rsqrt(jnp.mean(x * x, axis=-1, keepdims=True) + EPS) * g


def _silu(x):
    return x * jax.nn.sigmoid(x)


def _softplus(x):
    return jnp.maximum(x, 0.0) + jnp.log1p(jnp.exp(-jnp.abs(x)))


def _swiglu_residual(x, g_ref, wg_ref, wu_ref, wd_ref):
    xn = _rms(x, g_ref[...]).astype(BF16)
    acc = jnp.zeros_like(x)
    for c in range(FF_NCHUNK):
        gate = jnp.dot(xn, wg_ref[c], preferred_element_type=F32)
        up = jnp.dot(xn, wu_ref[c], preferred_element_type=F32)
        h = (_silu(gate) * up).astype(BF16)
        acc = acc + jnp.dot(h, wd_ref[c], preferred_element_type=F32)
    return x + 0.5 * acc


def _pre_kernel(x_ref, g1_ref, wg_ref, wu_ref, wd_ref, gm_ref, wmix_ref, x1_ref, z_ref):
    x1 = _swiglu_residual(x_ref[...], g1_ref, wg_ref, wu_ref, wd_ref)
    x1_ref[...] = x1
    u = _rms(x1, gm_ref[...]).astype(BF16)
    for c in range(MIX_COLS // 256):
        sl = slice(c * 256, (c + 1) * 256)
        z_ref[:, sl] = jnp.dot(u, wmix_ref[:, sl], preferred_element_type=F32)


def _pre_call(x, lw, tm):
    n = x.shape[0]
    row = lambda w: pl.BlockSpec((tm, w), lambda i: (i, 0))
    return pl.pallas_call(
        _pre_kernel,
        grid=(n // tm,),
        in_specs=[row(D_MODEL), _const_spec((1, D_MODEL)),
                  _const_spec((FF_NCHUNK, D_MODEL, FF_CHUNK)), _const_spec((FF_NCHUNK, D_MODEL, FF_CHUNK)),
                  _const_spec((FF_NCHUNK, FF_CHUNK, D_MODEL)), _const_spec((1, D_MODEL)),
                  _const_spec((D_MODEL, MIX_COLS))],
        out_specs=[row(D_MODEL), row(MIX_COLS)],
        out_shape=[jax.ShapeDtypeStruct((n, D_MODEL), F32), jax.ShapeDtypeStruct((n, MIX_COLS), F32)],
        compiler_params=pltpu.CompilerParams(dimension_semantics=("arbitrary",), vmem_limit_bytes=VMEM_LIMIT),
        name="pre",
    )(x, lw["ffn1_norm"], lw["ffn1_wg"], lw["ffn1_wu"], lw["ffn1_wd"], lw["mix_norm"], lw["w_mix"])


def _post_kernel(final, x1_ref, oa_ref, obcd_ref, gm_ref, wgate_ref, wbr_ref, wout_ref,
                 g2_ref, wg_ref, wu_ref, wd_ref, gf_ref, out_ref):
    x1 = x1_ref[...]
    u = _rms(x1, gm_ref[...]).astype(BF16)
    m = jnp.zeros_like(x1)
    for i in range(4):
        gates = jax.nn.sigmoid(jnp.dot(u, wgate_ref[:, i * D_MODEL:(i + 1) * D_MODEL],
                                       preferred_element_type=F32))
        br = oa_ref[...] if i == 0 else obcd_ref[:, (i - 1) * 256:i * 256]
        m = m + gates * _bdot(br, wbr_ref[i])
    x2 = x1 + _bdot(m, wout_ref[...])
    x3 = _swiglu_residual(x2, g2_ref, wg_ref, wu_ref, wd_ref)
    if final:
        x3 = _rms(x3, gf_ref[...])
    out_ref[...] = x3


def _post_call(x1, oa, obcd, lw, final_norm, final, tm):
    n = x1.shape[0]
    row = lambda w: pl.BlockSpec((tm, w), lambda i: (i, 0))
    return pl.pallas_call(
        functools.partial(_post_kernel, final),
        grid=(n // tm,),
        in_specs=[row(D_MODEL), row(256), row(768), _const_spec((1, D_MODEL)),
                  _const_spec((D_MODEL, 4 * D_MODEL)), _const_spec((4, 256, D_MODEL)),
                  _const_spec((D_MODEL, D_MODEL)), _const_spec((1, D_MODEL)),
                  _const_spec((FF_NCHUNK, D_MODEL, FF_CHUNK)), _const_spec((FF_NCHUNK, D_MODEL, FF_CHUNK)),
                  _const_spec((FF_NCHUNK, FF_CHUNK, D_MODEL)), _const_spec((1, D_MODEL))],
        out_specs=row(D_MODEL),
        out_shape=jax.ShapeDtypeStruct((n, D_MODEL), F32),
        compiler_params=pltpu.CompilerParams(dimension_semantics=("arbitrary",), vmem_limit_bytes=VMEM_LIMIT),
        name="post",
    )(x1, oa, obcd, lw["mix_norm"], lw["w_gate"], lw["w_branch"], lw["w_out"],
      lw["ffn2_norm"], lw["ffn2_wg"], lw["ffn2_wu"], lw["ffn2_wd"], final_norm)


def _solve_factors(low, eye_f, blk16):
    nd = jnp.where(blk16, low, 0.0)
    off = low - nd
    p = eye_f - nd
    npow = nd
    for _ in range(3):
        npow = _hdot(npow, npow)
        p = _hdot(p, eye_f + npow)
    m = _hdot(p, off)
    m2 = _hdot(m, m)
    return p, m, m2


def _solve_apply(p, m, m2, rhs):
    y = _hdot(p, rhs)
    y2 = y + _hdot(m2, y)
    return y2 - _hdot(m, y2)


def _delta_chunk(qkv, beta, gl, dz, s_ref, ng, consts):
    tri, eye, eye_f, causal, strict, blk16 = consts
    outs = []
    for h in range(HEADS):
        sl = slice(h * DK, (h + 1) * DK)
        q = qkv[:, h * DK:(h + 1) * DK]
        k = qkv[:, 256 + h * DK:256 + (h + 1) * DK]
        v = qkv[:, 512 + h * DK:512 + (h + 1) * DK]
        q = q * lax.rsqrt(jnp.sum(q * q, axis=-1, keepdims=True) + EPS) * (DK ** -0.5)
        k = k * lax.rsqrt(jnp.sum(k * k, axis=-1, keepdims=True) + EPS)
        b = beta[:, sl]
        gc = _hdot(tri, gl[:, sl])
        gct = jnp.sum(jnp.where(eye, gc, 0.0), axis=0, keepdims=True)
        dec = jnp.where(causal, jnp.exp(jnp.where(causal, gc - gct, 0.0)), 0.0)
        kb = k * b
        k16 = k.astype(BF16)
        nt = (((1,), (1,)), ((), ()))
        kk = lax.dot_general(kb.astype(BF16), k16, nt, preferred_element_type=F32)
        qk = lax.dot_general(q.astype(BF16), k16, nt, preferred_element_type=F32)
        low = jnp.where(strict, kk * dec, 0.0)
        intra = qk * dec
        eg = jnp.exp(gc)
        p, m, m2 = _solve_factors(low, eye_f, blk16)
        u = _solve_apply(p, m, m2, v * b)
        w = _solve_apply(p, m, m2, kb * eg)
        s = s_ref[h]
        s16 = s.astype(BF16)
        vnew = u - jnp.dot(w.astype(BF16), s16, preferred_element_type=F32)
        vn16 = vnew.astype(BF16)
        o = (jnp.dot((q * eg).astype(BF16), s16, preferred_element_type=F32)
             + jnp.dot(intra.astype(BF16), vn16, preferred_element_type=F32))
        glast = gc[CHUNK - 1:CHUNK, :]
        kdec = k * jnp.exp(glast - gc)
        tn = (((0,), (0,)), ((), ()))
        s_ref[h] = s * jnp.exp(glast) + lax.dot_general(kdec.astype(BF16), vn16, tn,
                                                       preferred_element_type=F32)
        o = o * lax.rsqrt(jnp.mean(o * o, axis=-1, keepdims=True) + EPS) * ng
        outs.append(o * _silu(dz[:, sl]))
    return outs


def _delta_kernel(t_len, q_ref, k_ref, v_ref, dz_ref, be_ref, ae_ref, buf_ref, s0_ref, cw_ref, alog_ref,
                  dtb_ref, ng_ref, o_ref, nbuf_ref, sfin_ref, xs_ref, s_ref):
    t0 = t_len % CHUNK
    nfull = t_len // CHUNK
    xs_ref[0:8, :] = jnp.zeros((8, QKV_W), F32)
    xs_ref[5:8, :] = buf_ref[...]
    xs_ref[8:8 + t_len, 0:256] = q_ref[...]
    xs_ref[8:8 + t_len, 256:512] = k_ref[...]
    xs_ref[8:8 + t_len, 512:768] = v_ref[...]
    nbuf_ref[...] = xs_ref[t_len + 5:t_len + 8, :]
    s_ref[...] = s0_ref[...]

    ri = lax.broadcasted_iota(jnp.int32, (CHUNK, CHUNK), 0)
    ci = lax.broadcasted_iota(jnp.int32, (CHUNK, CHUNK), 1)
    causal = ri >= ci
    strict = ri > ci
    eye = ri == ci
    consts = (causal.astype(F32), eye, eye.astype(F32), causal, strict, (ri // 16) == (ci // 16))
    cw = cw_ref[...]
    neg_a = -jnp.exp(alog_ref[...])
    dtb = dtb_ref[...]
    ng = ng_ref[...]

    def prep(r0, n):
        win = xs_ref[pl.ds(r0, n + 8), :]
        y = (cw[3:4] * win[8:8 + n] + cw[2:3] * win[7:7 + n]
             + cw[1:2] * win[6:6 + n] + cw[0:1] * win[5:5 + n])
        qkv = _silu(y)
        beta = jax.nn.sigmoid(be_ref[pl.ds(r0, n), :])
        gl = neg_a * _softplus(ae_ref[pl.ds(r0, n), :] + dtb)
        return qkv, beta, gl, dz_ref[pl.ds(r0, n), :]

    if t0:
        pad = CHUNK - t0
        parts = prep(0, t0)
        qkv, beta, gl, dz = [jnp.concatenate([jnp.zeros((pad, a.shape[1]), F32), a], axis=0) for a in parts]
        outs = _delta_chunk(qkv, beta, gl, dz, s_ref, ng, consts)
        for h in range(HEADS):
            o_ref[0:t0, h * DK:(h + 1) * DK] = outs[h][pad:, :]

    if nfull:
        def body(c, carry):
            r0 = pl.multiple_of(t0 + c * CHUNK, 8)
            qkv, beta, gl, dz = prep(r0, CHUNK)
            outs = _delta_chunk(qkv, beta, gl, dz, s_ref, ng, consts)
            for h in range(HEADS):
                o_ref[pl.ds(r0, CHUNK), h * DK:(h + 1) * DK] = outs[h]
            return carry
        lax.fori_loop(0, nfull, body, 0)

    sfin_ref[...] = s_ref[...]


def _delta_call(z2, bsz, buf, s0, lw):
    t_len = z2.shape[0]
    nblk = MIX_COLS // 256
    sq = pl.Squeezed()
    zspec = lambda blk: pl.BlockSpec((t_len, 256), lambda b: (0, b * nblk + blk))
    return pl.pallas_call(
        functools.partial(_delta_kernel, t_len),
        grid=(bsz,),
        in_specs=[zspec(0), zspec(1), zspec(2), zspec(COL_DZ_BLK), zspec(COL_BETA_BLK), zspec(COL_DECAY_BLK),
                  pl.BlockSpec((sq, DN_K - 1, QKV_W), lambda b: (b, 0, 0)),
                  pl.BlockSpec((sq, HEADS, DK, DK), lambda b: (b, 0, 0, 0)),
                  _const_spec((DN_K, QKV_W)), _const_spec((1, 256)), _const_spec((1, 256)),
                  _const_spec((1, DK))],
        out_specs=[pl.BlockSpec((t_len, 256), lambda b: (0, b)),
                   pl.BlockSpec((sq, DN_K - 1, QKV_W), lambda b: (b, 0, 0)),
                   pl.BlockSpec((sq, HEADS, DK, DK), lambda b: (b, 0, 0, 0))],
        out_shape=[jax.ShapeDtypeStruct((t_len, bsz * 256), F32),
                   jax.ShapeDtypeStruct((bsz, DN_K - 1, QKV_W), F32),
                   jax.ShapeDtypeStruct((bsz, HEADS, DK, DK), F32)],
        scratch_shapes=[pltpu.VMEM((t_len + 8, QKV_W), F32), pltpu.VMEM((HEADS, DK, DK), F32)],
        compiler_params=pltpu.CompilerParams(dimension_semantics=("arbitrary",), vmem_limit_bytes=VMEM_LIMIT),
        name="delta",
    )(z2, z2, z2, z2, z2, z2, buf, s0, lw["dn_conv_w"], lw["dn_a_log"], lw["dn_dt_bias"], lw["dn_norm"])


def _scan_time_major(t_len, bsz, state_refs, step):
    def run_group(goff):
        hs = tuple(r[pl.ds(goff, 8), :] for r in state_refs)
        if t_len <= 8:
            for t in range(t_len):
                hs = step(hs, t * bsz + goff)
        else:
            def body(t, hs):
                return step(hs, pl.multiple_of(t * bsz + goff, 8))
            hs = lax.fori_loop(0, t_len, body, hs, unroll=3)
        for r, h in zip(state_refs, hs):
            r[pl.ds(goff, 8), :] = h

    if bsz == 8:
        run_group(0)
    else:
        def gbody(g, carry):
            run_group(pl.multiple_of(g * 8, 8))
            return carry
        lax.fori_loop(0, bsz // 8, gbody, 0)


def _bcd_kernel(t_len, bsz, nsteps,
                lru_ref, cv_ref, s5_ref, s5re0, s5im0, lru0, lbuf0, cbuf0,
                lam_re_ref, lam_im_ref, lstep_ref, wb_ref, wcre_ref, wcim_ref, dskip_ref, wglu_ref, bglu_ref,
                lcw_ref, lcb_ref, wa_ref, ba_ref, wx_ref, bx_ref, llam_ref,
                ccw_ref, ccb_ref, lng_ref, lnb_ref,
                o_ref, s5re_o, s5im_o, lru_o, lbuf_o, cbuf_o,
                xr, xi, lxs, cxs, a_s, b_s):
    rows = t_len * bsz
    lb = (LRU_K - 1) * bsz
    cb = (CV_K - 1) * bsz

    @pl.when(pl.program_id(0) == 0)
    def _():
        s5re_o[...] = s5re0[...]
        s5im_o[...] = s5im0[...]
        lru_o[...] = lru0[...]
        lxs[0:lb, :] = lbuf0[...]
        cxs[0:cb, :] = cbuf0[...]

    lxs[lb:lb + rows, :] = lru_ref[:, 0:256]
    xf = lcb_ref[...] + lcw_ref[0:1, :] * lxs[0:rows, :]
    for k in range(1, LRU_K):
        xf = xf + lcw_ref[k:k + 1, :] * lxs[k * bsz:k * bsz + rows, :]
    r = jax.nn.sigmoid(_bdot(xf, wa_ref[...]) + ba_ref[...])
    i = jax.nn.sigmoid(_bdot(xf, wx_ref[...]) + bx_ref[...])
    log_a = (-LRU_C) * r * _softplus(-llam_ref[...])
    a_s[...] = jnp.exp(log_a)
    b_s[...] = jnp.sqrt(1.0 - jnp.exp(2.0 * log_a)) * (i * xf)

    def lru_step(hs, row):
        h = a_s[pl.ds(row, 8), :] * hs[0] + b_s[pl.ds(row, 8), :]
        b_s[pl.ds(row, 8), :] = h
        return (h,)
    _scan_time_major(t_len, bsz, (lru_o,), lru_step)
    o_ref[:, 256:512] = b_s[...] * jax.nn.gelu(lru_ref[:, 256:512])
    lbuf_o[...] = lxs[rows:rows + lb, :]
    if nsteps > 1:
        lxs[0:lb, :] = lxs[rows:rows + lb, :]

    dt = jnp.exp(lstep_ref[...])
    lam_re = lam_re_ref[...]
    lam_im = lam_im_ref[...]
    mag = jnp.exp(lam_re * dt)
    lb_re = mag * jnp.cos(lam_im * dt)
    lb_im = mag * jnp.sin(lam_im * dt)
    den = lam_re * lam_re + lam_im * lam_im
    cf_re = ((lb_re - 1.0) * lam_re + lb_im * lam_im) / den
    cf_im = (lb_im * lam_re - (lb_re - 1.0) * lam_im) / den
    u16 = s5_ref[...].astype(BF16)
    bu_re = jnp.dot(u16, wb_ref[:, 0:S5_STATES], preferred_element_type=F32)
    bu_im = jnp.dot(u16, wb_ref[:, S5_STATES:2 * S5_STATES], preferred_element_type=F32)
    xr[...] = cf_re * bu_re - cf_im * bu_im
    xi[...] = cf_re * bu_im + cf_im * bu_re
    lbr = jnp.broadcast_to(lb_re, (8, S5_STATES))
    lbi = jnp.broadcast_to(lb_im, (8, S5_STATES))

    def s5_step(hs, row):
        hr, hi = hs
        nr = lbr * hr - lbi * hi + xr[pl.ds(row, 8), :]
        ni = lbr * hi + lbi * hr + xi[pl.ds(row, 8), :]
        xr[pl.ds(row, 8), :] = nr
        xi[pl.ds(row, 8), :] = ni
        return (nr, ni)
    _scan_time_major(t_len, bsz, (s5re_o, s5im_o), s5_step)
    y = (_bdot(xr[...], wcre_ref[...]) - _bdot(xi[...], wcim_ref[...])
         + dskip_ref[...] * s5_ref[...])
    y = jax.nn.gelu(y)
    glu = _bdot(y, wglu_ref[...]) + bglu_ref[...]
    o_ref[:, 0:256] = glu[:, 0:256] * jax.nn.sigmoid(glu[:, 256:512])

    cxs[cb:cb + rows, :] = cv_ref[:, 0:256] * jax.nn.sigmoid(cv_ref[:, 256:512])
    yc = ccb_ref[...] + ccw_ref[0:1, :] * cxs[0:rows, :]
    for k in range(1, CV_K):
        yc = yc + ccw_ref[k:k + 1, :] * cxs[k * bsz:k * bsz + rows, :]
    mu = jnp.mean(yc, axis=-1, keepdims=True)
    ycc = yc - mu
    yn = ycc * lax.rsqrt(jnp.mean(ycc * ycc, axis=-1, keepdims=True) + EPS) * lng_ref[...] + lnb_ref[...]
    o_ref[:, 512:768] = _silu(yn)
    cbuf_o[...] = cxs[rows:rows + cb, :]
    if nsteps > 1:
        cxs[0:cb, :] = cxs[rows:rows + cb, :]


def _bcd_call(zmix, states, lw, t_len, bsz, tb):
    n = zmix.shape[0]
    nsteps = t_len // tb
    rows = tb * bsz
    lb = (LRU_K - 1) * bsz
    cb = (CV_K - 1) * bsz
    assert nsteps == 1 or rows >= cb
    zspec = lambda w, blk: pl.BlockSpec((rows, w), lambda i: (i, blk))
    state_shapes = [(bsz, S5_STATES), (bsz, S5_STATES), (bsz, 256), (lb, 256), (cb, 256)]
    params = [lw["s5_lam_re"], lw["s5_lam_im"], lw["s5_log_step"], lw["s5_wb"], lw["s5_wcre"], lw["s5_wcim"],
              lw["s5_d"], lw["s5_w_glu"], lw["s5_b_glu"],
              lw["lru_conv_w"], lw["lru_conv_b"], lw["lru_wa"], lw["lru_b_a"], lw["lru_wx"], lw["lru_b_x"],
              lw["lru_lam"], lw["cv_conv_w"], lw["cv_conv_b"], lw["cv_ln_g"], lw["cv_ln_b"]]
    return pl.pallas_call(
        functools.partial(_bcd_kernel, tb, bsz, nsteps),
        grid=(nsteps,),
        in_specs=([zspec(512, COL_LRU_BLK), zspec(512, COL_CV_BLK), zspec(256, COL_S5_BLK)]
                  + [_const_spec(s) for s in state_shapes]
                  + [_const_spec(p.shape) for p in params]),
        out_specs=[pl.BlockSpec((rows, 768), lambda i: (i, 0))]
                  + [pl.BlockSpec(s, lambda i: (0, 0)) for s in state_shapes],
        out_shape=[jax.ShapeDtypeStruct((n, 768), F32)]
                  + [jax.ShapeDtypeStruct(s, F32) for s in state_shapes],
        scratch_shapes=[pltpu.VMEM((rows, S5_STATES), F32), pltpu.VMEM((rows, S5_STATES), F32),
                        pltpu.VMEM((lb + rows, 256), F32), pltpu.VMEM((cb + rows, 256), F32),
                        pltpu.VMEM((rows, 256), F32), pltpu.VMEM((rows, 256), F32)],
        compiler_params=pltpu.CompilerParams(dimension_semantics=("arbitrary",), vmem_limit_bytes=VMEM_LIMIT),
        name="bcd",
    )(zmix, zmix, zmix, *states, *params)


def _layer_weights(l, p):
    def ffn(w_gu, w_down):
        split = lambda w: w.reshape(D_MODEL, FF_NCHUNK, FF_CHUNK).transpose(1, 0, 2).astype(BF16)
        return split(w_gu[:, :D_FF]), split(w_gu[:, D_FF:]), w_down.reshape(FF_NCHUNK, FF_CHUNK, D_MODEL).astype(BF16)

    row = lambda v: v.reshape(1, -1).astype(F32)
    w_in = p["w_in"][l]
    w_mix = jnp.concatenate(
        [w_in[:, 0:1024], w_in[:, 1288:1800], w_in[:, 1800:2312], w_in[:, 1032:1288],
         jnp.repeat(w_in[:, 1024:1028], DK, axis=1), jnp.repeat(w_in[:, 1028:1032], DK, axis=1)], axis=1)
    lw = {}
    lw["ffn1_wg"], lw["ffn1_wu"], lw["ffn1_wd"] = ffn(p["ffn1_w_gu"][l], p["ffn1_w_down"][l])
    lw["ffn2_wg"], lw["ffn2_wu"], lw["ffn2_wd"] = ffn(p["ffn2_w_gu"][l], p["ffn2_w_down"][l])
    lw["ffn1_norm"] = row(p["ffn1_norm"][l])
    lw["ffn2_norm"] = row(p["ffn2_norm"][l])
    lw["mix_norm"] = row(p["mix_norm"][l])
    lw["w_mix"] = w_mix.astype(BF16)
    lw["w_gate"] = w_in[:, 2312:].astype(BF16)
    lw["w_branch"] = p["w_branch"][l].astype(BF16)
    lw["w_out"] = p["w_out"][l].astype(BF16)
    lw["dn_conv_w"] = p["dn_conv_w"][l]
    lw["dn_a_log"] = row(jnp.repeat(p["dn_a_log"][l], DK))
    lw["dn_dt_bias"] = row(jnp.repeat(p["dn_dt_bias"][l], DK))
    lw["dn_norm"] = row(p["dn_norm"][l])
    lw["s5_lam_re"] = row(p["s5_lam_re"][l])
    lw["s5_lam_im"] = row(p["s5_lam_im"][l])
    lw["s5_log_step"] = row(jnp.repeat(p["s5_log_step"][l], 64))
    bre = block_diag(*[p["s5_b_re"][l][g].T for g in range(16)])
    bim = block_diag(*[p["s5_b_im"][l][g].T for g in range(16)])
    lw["s5_wb"] = jnp.concatenate([bre, bim], axis=1).astype(BF16)
    lw["s5_wcre"] = block_diag(*[p["s5_c_re"][l][g].T for g in range(16)]).astype(BF16)
    lw["s5_wcim"] = block_diag(*[p["s5_c_im"][l][g].T for g in range(16)]).astype(BF16)
    lw["s5_d"] = row(p["s5_d"][l])
    lw["s5_w_glu"] = p["s5_w_glu"][l].astype(BF16)
    lw["s5_b_glu"] = row(p["s5_b_glu"][l])
    lw["lru_conv_w"] = p["lru_conv_w"][l]
    lw["lru_conv_b"] = row(p["lru_conv_b"][l])
    lw["lru_wa"] = block_diag(*[p["lru_w_a"][l][n] for n in range(4)]).astype(BF16)
    lw["lru_wx"] = block_diag(*[p["lru_w_x"][l][n] for n in range(4)]).astype(BF16)
    lw["lru_b_a"] = row(p["lru_b_a"][l])
    lw["lru_b_x"] = row(p["lru_b_x"][l])
    lw["lru_lam"] = row(p["lru_lam"][l])
    lw["cv_conv_w"] = p["cv_conv_w"][l]
    lw["cv_conv_b"] = row(p["cv_conv_b"][l])
    lw["cv_ln_g"] = row(p["cv_ln_g"][l])
    lw["cv_ln_b"] = row(p["cv_ln_b"][l])
    return lw


def _to_time_major(a):
    bsz, k, c = a.shape
    return jnp.transpose(a, (1, 0, 2)).reshape(k * bsz, c)


def _from_time_major(a, bsz):
    k = a.shape[0] // bsz
    return jnp.transpose(a.reshape(k, bsz, a.shape[1]), (1, 0, 2))


def _layer(x, st, lw, final_norm, final, t_len, bsz, tm, tb):
    s_dn, s_dnc, s_re, s_im, s_lru, s_lruc, s_cv = st
    x1, zmix = _pre_call(x, lw, tm)
    oa2, n_dnc, n_dn = _delta_call(zmix.reshape(t_len, bsz * MIX_COLS), bsz, s_dnc, s_dn, lw)
    tm_states = (s_re.reshape(bsz, S5_STATES), s_im.reshape(bsz, S5_STATES), s_lru,
                 _to_time_major(s_lruc), _to_time_major(s_cv))
    obcd, n_re, n_im, n_lru, n_lruc, n_cv = _bcd_call(zmix, tm_states, lw, t_len, bsz, tb)
    x3 = _post_call(x1, oa2.reshape(t_len * bsz, 256), obcd, lw, final_norm, final, tm)
    new = (n_dn, n_dnc, n_re.reshape(bsz, 16, 64), n_im.reshape(bsz, 16, 64), n_lru,
           _from_time_major(n_lruc, bsz), _from_time_major(n_cv, bsz))
    return x3, new


def _zero_state(bsz):
    return (jnp.zeros((bsz, HEADS, DK, DK), F32), jnp.zeros((bsz, DN_K - 1, QKV_W), F32),
            jnp.zeros((bsz, 16, 64), F32), jnp.zeros((bsz, 16, 64), F32), jnp.zeros((bsz, 256), F32),
            jnp.zeros((bsz, LRU_K - 1, 256), F32), jnp.zeros((bsz, CV_K - 1, 256), F32))


def kernel(x_prompt, x_sample, state_delta, state_delta_conv, state_s5_re, state_s5_im, state_lru, state_lru_conv, state_conv, meta_tokens, ffn1_norm, ffn1_w_gu, ffn1_w_down, mix_norm, w_in, dn_conv_w, dn_a_log, dn_dt_bias, dn_norm, s5_lam_re, s5_lam_im, s5_log_step, s5_b_re, s5_b_im, s5_c_re, s5_c_im, s5_d, s5_w_glu, s5_b_glu, lru_conv_w, lru_conv_b, lru_w_a, lru_b_a, lru_w_x, lru_b_x, lru_lam, cv_conv_w, cv_conv_b, cv_ln_g, cv_ln_b, w_branch, w_out, ffn2_norm, ffn2_w_gu, ffn2_w_down, final_norm):
    p = dict(ffn1_norm=ffn1_norm, ffn1_w_gu=ffn1_w_gu, ffn1_w_down=ffn1_w_down, mix_norm=mix_norm, w_in=w_in,
             dn_conv_w=dn_conv_w, dn_a_log=dn_a_log, dn_dt_bias=dn_dt_bias, dn_norm=dn_norm,
             s5_lam_re=s5_lam_re, s5_lam_im=s5_lam_im, s5_log_step=s5_log_step, s5_b_re=s5_b_re,
             s5_b_im=s5_b_im, s5_c_re=s5_c_re, s5_c_im=s5_c_im, s5_d=s5_d, s5_w_glu=s5_w_glu,
             s5_b_glu=s5_b_glu, lru_conv_w=lru_conv_w, lru_conv_b=lru_conv_b, lru_w_a=lru_w_a,
             lru_b_a=lru_b_a, lru_w_x=lru_w_x, lru_b_x=lru_b_x, lru_lam=lru_lam, cv_conv_w=cv_conv_w,
             cv_conv_b=cv_conv_b, cv_ln_g=cv_ln_g, cv_ln_b=cv_ln_b, w_branch=w_branch, w_out=w_out,
             ffn2_norm=ffn2_norm, ffn2_w_gu=ffn2_w_gu, ffn2_w_down=ffn2_w_down)
    depth = w_in.shape[0]
    bp, seq, _ = x_prompt.shape
    bs, dseq, _ = x_sample.shape
    tp = seq + N_META
    fnorm = final_norm.reshape(1, D_MODEL)

    meta = jnp.broadcast_to(meta_tokens[:, None, :], (N_META, bp, D_MODEL))
    xp = jnp.concatenate([meta, jnp.transpose(x_prompt, (1, 0, 2))], axis=0).reshape(tp * bp, D_MODEL)
    xs = jnp.transpose(x_sample, (1, 0, 2)).reshape(dseq * bs, D_MODEL)

    p_new, s_new = [], []
    for l in range(depth):
        lw = _layer_weights(l, p)
        final = l == depth - 1
        xp, st_p = _layer(xp, _zero_state(bp), lw, fnorm, final, tp, bp, tm=344, tb=129)
        st_s = (state_delta[l], state_delta_conv[l], state_s5_re[l], state_s5_im[l], state_lru[l],
                state_lru_conv[l], state_conv[l])
        xs, st_s = _layer(xs, st_s, lw, fnorm, final, dseq, bs, tm=512, tb=dseq)
        p_new.append(st_p)
        s_new.append(st_s)

    y_prompt = jnp.transpose(xp.reshape(tp, bp, D_MODEL)[N_META:], (1, 0, 2))
    y_sample = jnp.transpose(xs.reshape(dseq, bs, D_MODEL), (1, 0, 2))
    stack = lambda new, i: jnp.stack([st[i] for st in new], axis=0)
    return (y_prompt, y_sample, *[stack(p_new, i) for i in range(7)], *[stack(s_new, i) for i in range(7)])
```

```python
import functools

import jax
import jax.numpy as jnp
from jax import lax
from jax.experimental import pallas as pl
from jax.experimental.pallas import tpu as pltpu
from jax.scipy.linalg import block_diag

F32 = jnp.float32
BF16 = jnp.bfloat16

D_MODEL = 1024
D_FF = 2816
N_META = 16
EPS = 1e-6
HEADS = 4
DK = 64
QKV_W = 768
CHUNK = 64
DELTA_BLOCK = 8
PREP_GROUP = 4
SOLVE_BLOCK = 16
S5_STATES = 1024
LRU_C = 8.0
CV_K = 31
LRU_K = 4
DN_K = 4

FF_CHUNK = 256
FF_NCHUNK = D_FF // FF_CHUNK
MIX_COLS = 2816
COL_DZ_BLK = 3
COL_LRU_BLK = 2
COL_CV_BLK = 3
COL_S5_BLK = 8
COL_BETA_BLK = 9
COL_DECAY_BLK = 10

VMEM_LIMIT = 56 * 1024 * 1024


def _const_spec(shape):
    nd = len(shape)
    return pl.BlockSpec(shape, lambda *_: (0,) * nd, pipeline_mode=pl.Buffered(1))


def _bdot(a, b):
    return jnp.dot(a.astype(BF16), b.astype(BF16), preferred_element_type=F32)


def _hdot(a, b):
    return jnp.dot(a, b, precision=lax.Precision.HIGHEST, preferred_element_type=F32)


def _rms(x, g):
    return x * lax.rsqrt(jnp.mean(x * x, axis=-1, keepdims=True) + EPS) * g


def _silu(x):
    return x * jax.nn.sigmoid(x)


def _softplus(x):
    return jnp.maximum(x, 0.0) + jnp.log1p(jnp.exp(-jnp.abs(x)))


def _swiglu_residual(x, g_ref, wg_ref, wu_ref, wd_ref):
    xn = _rms(x, g_ref[...]).astype(BF16)
    acc = jnp.zeros_like(x)
    for c in range(FF_NCHUNK):
        gate = jnp.dot(xn, wg_ref[c], preferred_element_type=F32)
        up = jnp.dot(xn, wu_ref[c], preferred_element_type=F32)
        h = (_silu(gate) * up).astype(BF16)
        acc = acc + jnp.dot(h, wd_ref[c], preferred_element_type=F32)
    return x + 0.5 * acc


def _pre_kernel(x_ref, g1_ref, wg_ref, wu_ref, wd_ref, gm_ref, wmix_ref, x1_ref, z_ref):
    x1 = _swiglu_residual(x_ref[...], g1_ref, wg_ref, wu_ref, wd_ref)
    x1_ref[...] = x1
    u = _rms(x1, gm_ref[...]).astype(BF16)
    for c in range(MIX_COLS // 256):
        sl = slice(c * 256, (c + 1) * 256)
        z_ref[:, sl] = jnp.dot(u, wmix_ref[:, sl], preferred_element_type=F32)


def _pre_call(x, lw, tm):
    n = x.shape[0]
    row = lambda w: pl.BlockSpec((tm, w), lambda i: (i, 0))
    return pl.pallas_call(
        _pre_kernel,
        grid=(n // tm,),
        in_specs=[row(D_MODEL), _const_spec((1, D_MODEL)),
                  _const_spec((FF_NCHUNK, D_MODEL, FF_CHUNK)), _const_spec((FF_NCHUNK, D_MODEL, FF_CHUNK)),
                  _const_spec((FF_NCHUNK, FF_CHUNK, D_MODEL)), _const_spec((1, D_MODEL)),
                  _const_spec((D_MODEL, MIX_COLS))],
        out_specs=[row(D_MODEL), row(MIX_COLS)],
        out_shape=[jax.ShapeDtypeStruct((n, D_MODEL), F32), jax.ShapeDtypeStruct((n, MIX_COLS), F32)],
        compiler_params=pltpu.CompilerParams(dimension_semantics=("arbitrary",), vmem_limit_bytes=VMEM_LIMIT),
        name="pre",
    )(x, lw["ffn1_norm"], lw["ffn1_wg"], lw["ffn1_wu"], lw["ffn1_wd"], lw["mix_norm"], lw["w_mix"])


def _post_kernel(final, x1_ref, oa_ref, obcd_ref, gm_ref, wgate_ref, wbr_ref, wout_ref,
                 g2_ref, wg_ref, wu_ref, wd_ref, gf_ref, out_ref):
    x1 = x1_ref[...]
    u = _rms(x1, gm_ref[...]).astype(BF16)
    m = jnp.zeros_like(x1)
    for i in range(4):
        gates = jax.nn.sigmoid(jnp.dot(u, wgate_ref[:, i * D_MODEL:(i + 1) * D_MODEL],
                                       preferred_element_type=F32))
        br = oa_ref[...] if i == 0 else obcd_ref[:, (i - 1) * 256:i * 256]
        m = m + gates * _bdot(br, wbr_ref[i])
    x2 = x1 + _bdot(m, wout_ref[...])
    x3 = _swiglu_residual(x2, g2_ref, wg_ref, wu_ref, wd_ref)
    if final:
        x3 = _rms(x3, gf_ref[...])
    out_ref[...] = x3


def _post_call(x1, oa, obcd, lw, final_norm, final, tm):
    n = x1.shape[0]
    row = lambda w: pl.BlockSpec((tm, w), lambda i: (i, 0))
    return pl.pallas_call(
        functools.partial(_post_kernel, final),
        grid=(n // tm,),
        in_specs=[row(D_MODEL), row(256), row(768), _const_spec((1, D_MODEL)),
                  _const_spec((D_MODEL, 4 * D_MODEL)), _const_spec((4, 256, D_MODEL)),
                  _const_spec((D_MODEL, D_MODEL)), _const_spec((1, D_MODEL)),
                  _const_spec((FF_NCHUNK, D_MODEL, FF_CHUNK)), _const_spec((FF_NCHUNK, D_MODEL, FF_CHUNK)),
                  _const_spec((FF_NCHUNK, FF_CHUNK, D_MODEL)), _const_spec((1, D_MODEL))],
        out_specs=row(D_MODEL),
        out_shape=jax.ShapeDtypeStruct((n, D_MODEL), F32),
        compiler_params=pltpu.CompilerParams(dimension_semantics=("arbitrary",), vmem_limit_bytes=VMEM_LIMIT),
        name="post",
    )(x1, oa, obcd, lw["mix_norm"], lw["w_gate"], lw["w_branch"], lw["w_out"],
      lw["ffn2_norm"], lw["ffn2_wg"], lw["ffn2_wu"], lw["ffn2_wd"], final_norm)


def _split16(x):
    hi = x.astype(BF16)
    return hi, (x - hi.astype(F32)).astype(BF16)


def _bmm(a, b, contract=(2, 1)):
    dims = (((contract[0],), (contract[1],)), ((0,), (0,)))
    return lax.dot_general(a, b, dims, preferred_element_type=F32)


def _dot3(a, b):
    return _bmm(a[0], b[0]) + _bmm(a[0], b[1]) + _bmm(a[1], b[0])


def _chunk_masks(seq_len):
    ri = lax.broadcasted_iota(jnp.int32, (CHUNK, CHUNK), 0)
    ci = lax.broadcasted_iota(jnp.int32, (CHUNK, CHUNK), 1)
    causal = ri >= ci
    strict = ri > ci
    if seq_len < CHUNK:
        same = (ri // seq_len) == (ci // seq_len)
        causal = causal & same
        strict = strict & same
    eye = ri == ci
    return dict(causal=causal, strict=strict, eye=eye, eye_f=eye.astype(F32),
                blk=(ri // SOLVE_BLOCK) == (ci // SOLVE_BLOCK), rowseq=ri // seq_len,
                rowpos=lax.broadcasted_iota(jnp.int32, (CHUNK, 256), 0) % seq_len)


def _wy_solve(low, rhs, mk, seq_len):
    eye_f = mk["eye_f"]
    if seq_len > SOLVE_BLOCK:
        nd = jnp.where(mk["blk"], low, 0.0)
        off = low - nd
        blk = SOLVE_BLOCK
    else:
        nd, off, blk = low, None, seq_len
    p = eye_f - nd
    npow = nd
    for _ in range(blk.bit_length() - 2):
        ns = _split16(npow)
        npow = _dot3(ns, ns)
        p = _dot3(_split16(p), _split16(eye_f + npow))
    ps = _split16(p)
    y = _dot3(ps, _split16(rhs))
    if off is None:
        return y
    assert seq_len // SOLVE_BLOCK == 4
    ms = _split16(_dot3(ps, _split16(off)))
    y2 = y + _dot3(_split16(_dot3(ms, ms)), _split16(y))
    return y2 - _dot3(ms, _split16(y2))


def _delta_prepare(chunks, mk, seq_len, scr, slots):
    u_s, w_s, qe_s, kd_s, a_s, egl_s = scr
    qs, ks, vs, bs, gcs = [], [], [], [], []
    for qkv, beta, gl in chunks:
        gc = gl
        shift = 1
        while shift < seq_len:
            gc = gc + jnp.where(mk["rowpos"] >= shift, pltpu.roll(gc, shift, axis=0), 0.0)
            shift *= 2
        for h in range(HEADS):
            qs.append(qkv[:, h * DK:(h + 1) * DK])
            ks.append(qkv[:, 256 + h * DK:256 + (h + 1) * DK])
            vs.append(qkv[:, 512 + h * DK:512 + (h + 1) * DK])
            bs.append(beta[:, h * DK:(h + 1) * DK])
            gcs.append(gc[:, h * DK:(h + 1) * DK])
    q, k, v, b, gc = [jnp.stack(a, axis=0) for a in (qs, ks, vs, bs, gcs)]
    q = q * lax.rsqrt(jnp.sum(q * q, axis=-1, keepdims=True) + EPS) * (DK ** -0.5)
    k = k * lax.rsqrt(jnp.sum(k * k, axis=-1, keepdims=True) + EPS)
    gct = jnp.sum(jnp.where(mk["eye"], gc, 0.0), axis=1, keepdims=True)
    dec = jnp.where(mk["causal"], jnp.exp(jnp.where(mk["causal"], gc - gct, 0.0)), 0.0)
    eg = jnp.exp(gc)
    kb = k * b
    k16 = k.astype(BF16)
    kk = _bmm(kb.astype(BF16), k16, (2, 2))
    qk = _bmm(q.astype(BF16), k16, (2, 2))
    low = jnp.where(mk["strict"], kk * dec, 0.0)
    x = _wy_solve(low, jnp.concatenate([v * b, kb * eg], axis=2), mk, seq_len)
    if seq_len == CHUNK:
        glast = gc[:, CHUNK - 1:CHUNK, :]
    else:
        glast = jnp.concatenate(
            [jnp.broadcast_to(gc[:, (j + 1) * seq_len - 1:(j + 1) * seq_len, :], (gc.shape[0], seq_len, DK))
             for j in range(CHUNK // seq_len)], axis=1)
    w16 = x[:, :, DK:2 * DK].astype(BF16)
    qe16 = (q * eg).astype(BF16)
    kd16 = (k * jnp.exp(glast - gc)).astype(BF16)
    a16 = (qk * dec).astype(BF16)
    egl = jnp.exp(glast)
    for c, slot in enumerate(slots):
        rows = pl.ds(slot * CHUNK, CHUNK)
        ps = slice(c * HEADS, (c + 1) * HEADS)
        u_s[:, rows, :] = x[ps, :, 0:DK]
        w_s[:, rows, :] = w16[ps]
        qe_s[:, rows, :] = qe16[ps]
        kd_s[:, rows, :] = kd16[ps]
        a_s[:, rows, :] = a16[ps]
        egl_s[:, rows, :] = jnp.broadcast_to(egl[ps], (HEADS, CHUNK, DK))


def _gated_norm(o, dz, ng):
    return o * lax.rsqrt(jnp.mean(o * o, axis=-1, keepdims=True) + EPS) * ng * _silu(dz)


def _delta_scratch(nslots):
    n = nslots * CHUNK
    return ([pltpu.VMEM((HEADS, n, DK), F32)] + [pltpu.VMEM((HEADS, n, DK), BF16)] * 4
            + [pltpu.VMEM((HEADS, n, DK), F32)])


TN_DIMS = (((0,), (0,)), ((), ()))


def _delta_kernel(t_len, q_ref, k_ref, v_ref, dz_ref, be_ref, ae_ref, buf_ref, s0_ref, cw_ref, alog_ref,
                  dtb_ref, ng_ref, o_ref, nbuf_ref, s_ref, xs_ref, *scr):
    t0 = t_len % CHUNK
    nblocks = (t_len // CHUNK) // DELTA_BLOCK
    assert nblocks * DELTA_BLOCK * CHUNK + t0 == t_len
    xs_ref[0:8, :] = jnp.zeros((8, QKV_W), F32)
    xs_ref[5:8, :] = buf_ref[...]
    xs_ref[8:8 + t_len, 0:256] = q_ref[...]
    xs_ref[8:8 + t_len, 256:512] = k_ref[...]
    xs_ref[8:8 + t_len, 512:768] = v_ref[...]
    nbuf_ref[...] = xs_ref[t_len + 5:t_len + 8, :]
    s_ref[...] = s0_ref[...]
    u_s, w_s, qe_s, kd_s, a_s, egl_s = scr

    mk = _chunk_masks(CHUNK)
    cw = cw_ref[...]
    neg_a = -jnp.exp(alog_ref[...])
    dtb = dtb_ref[...]
    ng = ng_ref[...]

    def prep(r0, n):
        win = xs_ref[pl.ds(r0, n + 8), :]
        y = (cw[3:4] * win[8:8 + n] + cw[2:3] * win[7:7 + n]
             + cw[1:2] * win[6:6 + n] + cw[0:1] * win[5:5 + n])
        beta = jax.nn.sigmoid(be_ref[pl.ds(r0, n), :])
        gl = neg_a * _softplus(ae_ref[pl.ds(r0, n), :] + dtb)
        return _silu(y), beta, gl

    def state_step(slot, r0, n):
        rows = pl.ds(slot * CHUNK, CHUNK)
        s = s_ref[...]
        s16 = s.astype(BF16)
        vnew = u_s[:, rows, :] - _bmm(w_s[:, rows, :], s16)
        vn16 = vnew.astype(BF16)
        o = _bmm(qe_s[:, rows, :], s16) + _bmm(a_s[:, rows, :], vn16)
        s_ref[...] = (s * egl_s[:, pl.ds(slot * CHUNK, 8), :][:, 0:1, :]
                      + _bmm(kd_s[:, rows, :], vn16, (1, 1)))
        for h in range(HEADS):
            sl = slice(h * DK, (h + 1) * DK)
            o_ref[pl.ds(r0, n), sl] = _gated_norm(o[h, CHUNK - n:, :], dz_ref[pl.ds(r0, n), sl], ng)

    if t0:
        pad = CHUNK - t0
        parts = [jnp.concatenate([jnp.zeros((pad, a.shape[1]), F32), a], axis=0) for a in prep(0, t0)]
        _delta_prepare([parts], mk, CHUNK, scr, [0])
        state_step(0, 0, t0)

    def block(blk, carry):
        base = t0 + blk * (DELTA_BLOCK * CHUNK)

        def prepare(g, carry):
            slots = [g * PREP_GROUP + i for i in range(PREP_GROUP)]
            chunks = [prep(pl.multiple_of(base + c * CHUNK, 8), CHUNK) for c in slots]
            _delta_prepare(chunks, mk, CHUNK, scr, slots)
            return carry
        lax.fori_loop(0, DELTA_BLOCK // PREP_GROUP, prepare, 0)

        def step(c, carry):
            state_step(c, pl.multiple_of(base + c * CHUNK, 8), CHUNK)
            return carry
        lax.fori_loop(0, DELTA_BLOCK, step, 0)
        return carry
    if nblocks:
        lax.fori_loop(0, nblocks, block, 0)


def _delta_call(z2, bsz, buf, s0, lw):
    t_len = z2.shape[0]
    nblk = MIX_COLS // 256
    sq = pl.Squeezed()
    zspec = lambda blk: pl.BlockSpec((t_len, 256), lambda b: (0, b * nblk + blk))
    return pl.pallas_call(
        functools.partial(_delta_kernel, t_len),
        grid=(bsz,),
        in_specs=[zspec(0), zspec(1), zspec(2), zspec(COL_DZ_BLK), zspec(COL_BETA_BLK), zspec(COL_DECAY_BLK),
                  pl.BlockSpec((sq, DN_K - 1, QKV_W), lambda b: (b, 0, 0)),
                  pl.BlockSpec((sq, HEADS, DK, DK), lambda b: (b, 0, 0, 0)),
                  _const_spec((DN_K, QKV_W)), _const_spec((1, 256)), _const_spec((1, 256)),
                  _const_spec((1, DK))],
        out_specs=[pl.BlockSpec((t_len, 256), lambda b: (0, b)),
                   pl.BlockSpec((sq, DN_K - 1, QKV_W), lambda b: (b, 0, 0)),
                   pl.BlockSpec((sq, HEADS, DK, DK), lambda b: (b, 0, 0, 0))],
        out_shape=[jax.ShapeDtypeStruct((t_len, bsz * 256), F32),
                   jax.ShapeDtypeStruct((bsz, DN_K - 1, QKV_W), F32),
                   jax.ShapeDtypeStruct((bsz, HEADS, DK, DK), F32)],
        scratch_shapes=[pltpu.VMEM((t_len + 8, QKV_W), F32)] + _delta_scratch(DELTA_BLOCK),
        compiler_params=pltpu.CompilerParams(dimension_semantics=("arbitrary",), vmem_limit_bytes=VMEM_LIMIT),
        name="delta",
    )(z2, z2, z2, z2, z2, z2, buf, s0, lw["dn_conv_w"], lw["dn_a_log"], lw["dn_dt_bias"], lw["dn_norm"])


def _delta_short_kernel(seq_len, q_ref, k_ref, v_ref, dz_ref, be_ref, ae_ref, buf_ref, s0_ref, cw_ref,
                        alog_ref, dtb_ref, ng_ref, o_ref, nbuf_ref, sfin_ref, xs_ref, *scr):
    nseq = CHUNK // seq_len
    span = seq_len + 8
    mk = _chunk_masks(seq_len)
    cw = cw_ref[...]
    xs_ref[...] = jnp.zeros(xs_ref.shape, F32)
    for j in range(nseq):
        xs_ref[j * span + 5:j * span + 8, :] = buf_ref[j]
        rows = slice(j * seq_len, (j + 1) * seq_len)
        xs_ref[j * span + 8:(j + 1) * span, 0:256] = q_ref[rows, :]
        xs_ref[j * span + 8:(j + 1) * span, 256:512] = k_ref[rows, :]
        xs_ref[j * span + 8:(j + 1) * span, 512:768] = v_ref[rows, :]
    ys = []
    for j in range(nseq):
        win = xs_ref[j * span:(j + 1) * span, :]
        ys.append(cw[3:4] * win[8:span] + cw[2:3] * win[7:span - 1]
                  + cw[1:2] * win[6:span - 2] + cw[0:1] * win[5:span - 3])
        nbuf_ref[j] = win[span - 3:span, :]
    qkv = _silu(jnp.concatenate(ys, axis=0))
    beta = jax.nn.sigmoid(be_ref[...])
    gl = -jnp.exp(alog_ref[...]) * _softplus(ae_ref[...] + dtb_ref[...])
    _delta_prepare([(qkv, beta, gl)], mk, seq_len, scr, [0])

    u_s, w_s, qe_s, kd_s, a_s, egl_s = scr
    w = w_s[...]
    qe = qe_s[...]
    kd = kd_s[...].astype(F32)
    u = u_s[...]
    ws, qs = [], []
    for j in range(nseq):
        s16 = s0_ref[j].astype(BF16)
        ws.append(_bmm(w, s16))
        qs.append(_bmm(qe, s16))
    vnew, o = u, jnp.zeros_like(u)
    for j in range(nseq):
        mine = mk["rowseq"] == j
        vnew = jnp.where(mine, u - ws[j], vnew)
        o = jnp.where(mine, qs[j], o)
    vn16 = vnew.astype(BF16)
    o = o + _bmm(a_s[...], vn16)
    for j in range(nseq):
        kdj = jnp.where(mk["rowseq"] == j, kd, 0.0).astype(BF16)
        sfin_ref[j] = (s0_ref[j] * egl_s[:, j * seq_len:(j + 1) * seq_len, :][:, 0:1, :]
                       + _bmm(kdj, vn16, (1, 1)))
    ng = ng_ref[...]
    for h in range(HEADS):
        sl = slice(h * DK, (h + 1) * DK)
        o_ref[:, sl] = _gated_norm(o[h], dz_ref[:, sl], ng)


def _delta_short_call(z, seq_len, buf, s0, lw):
    n = z.shape[0]
    nseq = CHUNK // seq_len
    zspec = lambda blk: pl.BlockSpec((CHUNK, 256), lambda g: (g, blk))
    bspec = pl.BlockSpec((nseq, DN_K - 1, QKV_W), lambda g: (g, 0, 0))
    sspec = pl.BlockSpec((nseq, HEADS, DK, DK), lambda g: (g, 0, 0, 0))
    return pl.pallas_call(
        functools.partial(_delta_short_kernel, seq_len),
        grid=(n // CHUNK,),
        in_specs=[zspec(0), zspec(1), zspec(2), zspec(COL_DZ_BLK), zspec(COL_BETA_BLK), zspec(COL_DECAY_BLK),
                  bspec, sspec, _const_spec((DN_K, QKV_W)), _const_spec((1, 256)), _const_spec((1, 256)),
                  _const_spec((1, DK))],
        out_specs=[pl.BlockSpec((CHUNK, 256), lambda g: (g, 0)), bspec, sspec],
        out_shape=[jax.ShapeDtypeStruct((n, 256), F32), jax.ShapeDtypeStruct(buf.shape, F32),
                   jax.ShapeDtypeStruct(s0.shape, F32)],
        scratch_shapes=[pltpu.VMEM((nseq * (seq_len + 8), QKV_W), F32)] + _delta_scratch(1),
        compiler_params=pltpu.CompilerParams(dimension_semantics=("arbitrary",), vmem_limit_bytes=VMEM_LIMIT),
        name="delta_short",
    )(z, z, z, z, z, z, buf, s0, lw["dn_conv_w"], lw["dn_a_log"], lw["dn_dt_bias"], lw["dn_norm"])


def _scan_time_major(t_len, bsz, state_refs, step):
    def run_group(goff):
        hs = tuple(r[pl.ds(goff, 8), :] for r in state_refs)
        if t_len <= 8:
            for t in range(t_len):
                hs = step(hs, t * bsz + goff)
        else:
            def body(t, hs):
                return step(hs, pl.multiple_of(t * bsz + goff, 8))
            hs = lax.fori_loop(0, t_len, body, hs, unroll=3)
        for r, h in zip(state_refs, hs):
            r[pl.ds(goff, 8), :] = h

    if bsz == 8:
        run_group(0)
    else:
        def gbody(g, carry):
            run_group(pl.multiple_of(g * 8, 8))
            return carry
        lax.fori_loop(0, bsz // 8, gbody, 0)


def _bcd_kernel(t_len, bsz, nsteps,
                lru_ref, cv_ref, s5_ref, s5re0, s5im0, lru0, lbuf0, cbuf0,
                lam_re_ref, lam_im_ref, lstep_ref, wb_ref, wcre_ref, wcim_ref, dskip_ref, wglu_ref, bglu_ref,
                lcw_ref, lcb_ref, wa_ref, ba_ref, wx_ref, bx_ref, llam_ref,
                ccw_ref, ccb_ref, lng_ref, lnb_ref,
                o_ref, s5re_o, s5im_o, lru_o, lbuf_o, cbuf_o,
                xr, xi, lxs, cxs, a_s, b_s):
    rows = t_len * bsz
    lb = (LRU_K - 1) * bsz
    cb = (CV_K - 1) * bsz

    @pl.when(pl.program_id(0) == 0)
    def _():
        s5re_o[...] = s5re0[...]
        s5im_o[...] = s5im0[...]
        lru_o[...] = lru0[...]
        lxs[0:lb, :] = lbuf0[...]
        cxs[0:cb, :] = cbuf0[...]

    lxs[lb:lb + rows, :] = lru_ref[:, 0:256]
    xf = lcb_ref[...] + lcw_ref[0:1, :] * lxs[0:rows, :]
    for k in range(1, LRU_K):
        xf = xf + lcw_ref[k:k + 1, :] * lxs[k * bsz:k * bsz + rows, :]
    r = jax.nn.sigmoid(_bdot(xf, wa_ref[...]) + ba_ref[...])
    i = jax.nn.sigmoid(_bdot(xf, wx_ref[...]) + bx_ref[...])
    log_a = (-LRU_C) * r * _softplus(-llam_ref[...])
    a_s[...] = jnp.exp(log_a)
    b_s[...] = jnp.sqrt(1.0 - jnp.exp(2.0 * log_a)) * (i * xf)

    def lru_step(hs, row):
        h = a_s[pl.ds(row, 8), :] * hs[0] + b_s[pl.ds(row, 8), :]
        b_s[pl.ds(row, 8), :] = h
        return (h,)
    _scan_time_major(t_len, bsz, (lru_o,), lru_step)
    o_ref[:, 256:512] = b_s[...] * jax.nn.gelu(lru_ref[:, 256:512])
    lbuf_o[...] = lxs[rows:rows + lb, :]
    if nsteps > 1:
        lxs[0:lb, :] = lxs[rows:rows + lb, :]

    dt = jnp.exp(lstep_ref[...])
    lam_re = lam_re_ref[...]
    lam_im = lam_im_ref[...]
    mag = jnp.exp(lam_re * dt)
    lb_re = mag * jnp.cos(lam_im * dt)
    lb_im = mag * jnp.sin(lam_im * dt)
    den = lam_re * lam_re + lam_im * lam_im
    cf_re = ((lb_re - 1.0) * lam_re + lb_im * lam_im) / den
    cf_im = (lb_im * lam_re - (lb_re - 1.0) * lam_im) / den
    u16 = s5_ref[...].astype(BF16)
    bu_re = jnp.dot(u16, wb_ref[:, 0:S5_STATES], preferred_element_type=F32)
    bu_im = jnp.dot(u16, wb_ref[:, S5_STATES:2 * S5_STATES], preferred_element_type=F32)
    xr[...] = cf_re * bu_re - cf_im * bu_im
    xi[...] = cf_re * bu_im + cf_im * bu_re
    lbr = jnp.broadcast_to(lb_re, (8, S5_STATES))
    lbi = jnp.broadcast_to(lb_im, (8, S5_STATES))

    def s5_step(hs, row):
        hr, hi = hs
        nr = lbr * hr - lbi * hi + xr[pl.ds(row, 8), :]
        ni = lbr * hi + lbi * hr + xi[pl.ds(row, 8), :]
        xr[pl.ds(row, 8), :] = nr
        xi[pl.ds(row, 8), :] = ni
        return (nr, ni)
    _scan_time_major(t_len, bsz, (s5re_o, s5im_o), s5_step)
    y = (_bdot(xr[...], wcre_ref[...]) - _bdot(xi[...], wcim_ref[...])
         + dskip_ref[...] * s5_ref[...])
    y = jax.nn.gelu(y)
    glu = _bdot(y, wglu_ref[...]) + bglu_ref[...]
    o_ref[:, 0:256] = glu[:, 0:256] * jax.nn.sigmoid(glu[:, 256:512])

    cxs[cb:cb + rows, :] = cv_ref[:, 0:256] * jax.nn.sigmoid(cv_ref[:, 256:512])
    yc = ccb_ref[...] + ccw_ref[0:1, :] * cxs[0:rows, :]
    for k in range(1, CV_K):
        yc = yc + ccw_ref[k:k + 1, :] * cxs[k * bsz:k * bsz + rows, :]
    mu = jnp.mean(yc, axis=-1, keepdims=True)
    ycc = yc - mu
    yn = ycc * lax.rsqrt(jnp.mean(ycc * ycc, axis=-1, keepdims=True) + EPS) * lng_ref[...] + lnb_ref[...]
    o_ref[:, 512:768] = _silu(yn)
    cbuf_o[...] = cxs[rows:rows + cb, :]
    if nsteps > 1:
        cxs[0:cb, :] = cxs[rows:rows + cb, :]


def _bcd_call(zmix, states, lw, t_len, bsz, tb, col_blocks=(COL_LRU_BLK, COL_CV_BLK, COL_S5_BLK)):
    n = zmix.shape[0]
    nsteps = t_len // tb
    rows = tb * bsz
    lb = (LRU_K - 1) * bsz
    cb = (CV_K - 1) * bsz
    assert nsteps == 1 or rows >= cb
    zspec = lambda w, blk: pl.BlockSpec((rows, w), lambda i: (i, blk))
    state_shapes = [(bsz, S5_STATES), (bsz, S5_STATES), (bsz, 256), (lb, 256), (cb, 256)]
    params = [lw["s5_lam_re"], lw["s5_lam_im"], lw["s5_log_step"], lw["s5_wb"], lw["s5_wcre"], lw["s5_wcim"],
              lw["s5_d"], lw["s5_w_glu"], lw["s5_b_glu"],
              lw["lru_conv_w"], lw["lru_conv_b"], lw["lru_wa"], lw["lru_b_a"], lw["lru_wx"], lw["lru_b_x"],
              lw["lru_lam"], lw["cv_conv_w"], lw["cv_conv_b"], lw["cv_ln_g"], lw["cv_ln_b"]]
    return pl.pallas_call(
        functools.partial(_bcd_kernel, tb, bsz, nsteps),
        grid=(nsteps,),
        in_specs=([zspec(512, col_blocks[0]), zspec(512, col_blocks[1]), zspec(256, col_blocks[2])]
                  + [_const_spec(s) for s in state_shapes]
                  + [_const_spec(p.shape) for p in params]),
        out_specs=[pl.BlockSpec((rows, 768), lambda i: (i, 0))]
                  + [pl.BlockSpec(s, lambda i: (0, 0)) for s in state_shapes],
        out_shape=[jax.ShapeDtypeStruct((n, 768), F32)]
                  + [jax.ShapeDtypeStruct(s, F32) for s in state_shapes],
        scratch_shapes=[pltpu.VMEM((rows, S5_STATES), F32), pltpu.VMEM((rows, S5_STATES), F32),
                        pltpu.VMEM((lb + rows, 256), F32), pltpu.VMEM((cb + rows, 256), F32),
                        pltpu.VMEM((rows, 256), F32), pltpu.VMEM((rows, 256), F32)],
        compiler_params=pltpu.CompilerParams(dimension_semantics=("arbitrary",), vmem_limit_bytes=VMEM_LIMIT),
        name="bcd",
    )(zmix, zmix, zmix, *states, *params)


def _layer_weights(l, p):
    def ffn(w_gu, w_down):
        split = lambda w: w.reshape(D_MODEL, FF_NCHUNK, FF_CHUNK).transpose(1, 0, 2).astype(BF16)
        return split(w_gu[:, :D_FF]), split(w_gu[:, D_FF:]), w_down.reshape(FF_NCHUNK, FF_CHUNK, D_MODEL).astype(BF16)

    row = lambda v: v.reshape(1, -1).astype(F32)
    w_in = p["w_in"][l]
    w_mix = jnp.concatenate(
        [w_in[:, 0:1024], w_in[:, 1288:1800], w_in[:, 1800:2312], w_in[:, 1032:1288],
         jnp.repeat(w_in[:, 1024:1028], DK, axis=1), jnp.repeat(w_in[:, 1028:1032], DK, axis=1)], axis=1)
    lw = {}
    lw["ffn1_wg"], lw["ffn1_wu"], lw["ffn1_wd"] = ffn(p["ffn1_w_gu"][l], p["ffn1_w_down"][l])
    lw["ffn2_wg"], lw["ffn2_wu"], lw["ffn2_wd"] = ffn(p["ffn2_w_gu"][l], p["ffn2_w_down"][l])
    lw["ffn1_norm"] = row(p["ffn1_norm"][l])
    lw["ffn2_norm"] = row(p["ffn2_norm"][l])
    lw["mix_norm"] = row(p["mix_norm"][l])
    lw["w_mix"] = w_mix.astype(BF16)
    lw["w_gate"] = w_in[:, 2312:].astype(BF16)
    lw["w_branch"] = p["w_branch"][l].astype(BF16)
    lw["w_out"] = p["w_out"][l].astype(BF16)
    lw["dn_conv_w"] = p["dn_conv_w"][l]
    lw["dn_a_log"] = row(jnp.repeat(p["dn_a_log"][l], DK))
    lw["dn_dt_bias"] = row(jnp.repeat(p["dn_dt_bias"][l], DK))
    lw["dn_norm"] = row(p["dn_norm"][l])
    lw["s5_lam_re"] = row(p["s5_lam_re"][l])
    lw["s5_lam_im"] = row(p["s5_lam_im"][l])
    lw["s5_log_step"] = row(jnp.repeat(p["s5_log_step"][l], 64))
    bre = block_diag(*[p["s5_b_re"][l][g].T for g in range(16)])
    bim = block_diag(*[p["s5_b_im"][l][g].T for g in range(16)])
    lw["s5_wb"] = jnp.concatenate([bre, bim], axis=1).astype(BF16)
    lw["s5_wcre"] = block_diag(*[p["s5_c_re"][l][g].T for g in range(16)]).astype(BF16)
    lw["s5_wcim"] = block_diag(*[p["s5_c_im"][l][g].T for g in range(16)]).astype(BF16)
    lw["s5_d"] = row(p["s5_d"][l])
    lw["s5_w_glu"] = p["s5_w_glu"][l].astype(BF16)
    lw["s5_b_glu"] = row(p["s5_b_glu"][l])
    lw["lru_conv_w"] = p["lru_conv_w"][l]
    lw["lru_conv_b"] = row(p["lru_conv_b"][l])
    lw["lru_wa"] = block_diag(*[p["lru_w_a"][l][n] for n in range(4)]).astype(BF16)
    lw["lru_wx"] = block_diag(*[p["lru_w_x"][l][n] for n in range(4)]).astype(BF16)
    lw["lru_b_a"] = row(p["lru_b_a"][l])
    lw["lru_b_x"] = row(p["lru_b_x"][l])
    lw["lru_lam"] = row(p["lru_lam"][l])
    lw["cv_conv_w"] = p["cv_conv_w"][l]
    lw["cv_conv_b"] = row(p["cv_conv_b"][l])
    lw["cv_ln_g"] = row(p["cv_ln_g"][l])
    lw["cv_ln_b"] = row(p["cv_ln_b"][l])
    return lw


def _to_time_major(a):
    bsz, k, c = a.shape
    return jnp.transpose(a, (1, 0, 2)).reshape(k * bsz, c)


def _from_time_major(a, bsz):
    k = a.shape[0] // bsz
    return jnp.transpose(a.reshape(k, bsz, a.shape[1]), (1, 0, 2))


def _bcd_states(st, bsz):
    _, _, s_re, s_im, s_lru, s_lruc, s_cv = st
    return (s_re.reshape(bsz, S5_STATES), s_im.reshape(bsz, S5_STATES), s_lru,
            _to_time_major(s_lruc), _to_time_major(s_cv))


def _new_states(n_dn, n_dnc, bcd_new, bsz):
    n_re, n_im, n_lru, n_lruc, n_cv = bcd_new
    return (n_dn, n_dnc, n_re.reshape(bsz, 16, 64), n_im.reshape(bsz, 16, 64), n_lru,
            _from_time_major(n_lruc, bsz), _from_time_major(n_cv, bsz))


def _layer_long(x, st, lw, final_norm, final, t_len, bsz, tm, tb):
    x1, zmix = _pre_call(x, lw, tm)
    oa2, n_dnc, n_dn = _delta_call(zmix.reshape(t_len, bsz * MIX_COLS), bsz, st[1], st[0], lw)
    obcd, *bcd_new = _bcd_call(zmix, _bcd_states(st, bsz), lw, t_len, bsz, tb)
    x3 = _post_call(x1, oa2.reshape(t_len * bsz, 256), obcd, lw, final_norm, final, tm)
    return x3, _new_states(n_dn, n_dnc, bcd_new, bsz)


def _layer_short(x, st, lw, final_norm, final, t_len, bsz, tm):
    x1, zmix = _pre_call(x, lw, tm)
    oa, n_dnc, n_dn = _delta_short_call(zmix, t_len, st[1], st[0], lw)
    lo, hi = COL_LRU_BLK * 512, (COL_S5_BLK + 1) * 256
    z_tm = _to_time_major(zmix[:, lo:hi].reshape(bsz, t_len, hi - lo))
    obcd_tm, *bcd_new = _bcd_call(z_tm, _bcd_states(st, bsz), lw, t_len, bsz, t_len,
                                  col_blocks=(0, COL_CV_BLK - COL_LRU_BLK, COL_S5_BLK - 2 * COL_LRU_BLK))
    obcd = _from_time_major(obcd_tm, bsz).reshape(bsz * t_len, 768)
    x3 = _post_call(x1, oa, obcd, lw, final_norm, final, tm)
    return x3, _new_states(n_dn, n_dnc, bcd_new, bsz)


def _zero_state(bsz):
    return (jnp.zeros((bsz, HEADS, DK, DK), F32), jnp.zeros((bsz, DN_K - 1, QKV_W), F32),
            jnp.zeros((bsz, 16, 64), F32), jnp.zeros((bsz, 16, 64), F32), jnp.zeros((bsz, 256), F32),
            jnp.zeros((bsz, LRU_K - 1, 256), F32), jnp.zeros((bsz, CV_K - 1, 256), F32))


def kernel(x_prompt, x_sample, state_delta, state_delta_conv, state_s5_re, state_s5_im, state_lru, state_lru_conv, state_conv, meta_tokens, ffn1_norm, ffn1_w_gu, ffn1_w_down, mix_norm, w_in, dn_conv_w, dn_a_log, dn_dt_bias, dn_norm, s5_lam_re, s5_lam_im, s5_log_step, s5_b_re, s5_b_im, s5_c_re, s5_c_im, s5_d, s5_w_glu, s5_b_glu, lru_conv_w, lru_conv_b, lru_w_a, lru_b_a, lru_w_x, lru_b_x, lru_lam, cv_conv_w, cv_conv_b, cv_ln_g, cv_ln_b, w_branch, w_out, ffn2_norm, ffn2_w_gu, ffn2_w_down, final_norm):
    p = dict(ffn1_norm=ffn1_norm, ffn1_w_gu=ffn1_w_gu, ffn1_w_down=ffn1_w_down, mix_norm=mix_norm, w_in=w_in,
             dn_conv_w=dn_conv_w, dn_a_log=dn_a_log, dn_dt_bias=dn_dt_bias, dn_norm=dn_norm,
             s5_lam_re=s5_lam_re, s5_lam_im=s5_lam_im, s5_log_step=s5_log_step, s5_b_re=s5_b_re,
             s5_b_im=s5_b_im, s5_c_re=s5_c_re, s5_c_im=s5_c_im, s5_d=s5_d, s5_w_glu=s5_w_glu,
             s5_b_glu=s5_b_glu, lru_conv_w=lru_conv_w, lru_conv_b=lru_conv_b, lru_w_a=lru_w_a,
             lru_b_a=lru_b_a, lru_w_x=lru_w_x, lru_b_x=lru_b_x, lru_lam=lru_lam, cv_conv_w=cv_conv_w,
             cv_conv_b=cv_conv_b, cv_ln_g=cv_ln_g, cv_ln_b=cv_ln_b, w_branch=w_branch, w_out=w_out,
             ffn2_norm=ffn2_norm, ffn2_w_gu=ffn2_w_gu, ffn2_w_down=ffn2_w_down)
    depth = w_in.shape[0]
    bp, seq, _ = x_prompt.shape
    bs, dseq, _ = x_sample.shape
    tp = seq + N_META
    fnorm = final_norm.reshape(1, D_MODEL)

    meta = jnp.broadcast_to(meta_tokens[:, None, :], (N_META, bp, D_MODEL))
    xp = jnp.concatenate([meta, jnp.transpose(x_prompt, (1, 0, 2))], axis=0).reshape(tp * bp, D_MODEL)
    xs = x_sample.reshape(bs * dseq, D_MODEL)

    p_new, s_new = [], []
    for l in range(depth):
        lw = _layer_weights(l, p)
        final = l == depth - 1
        xp, st_p = _layer_long(xp, _zero_state(bp), lw, fnorm, final, tp, bp, tm=344, tb=129)
        st_s = (state_delta[l], state_delta_conv[l], state_s5_re[l], state_s5_im[l], state_lru[l],
                state_lru_conv[l], state_conv[l])
        xs, st_s = _layer_short(xs, st_s, lw, fnorm, final, dseq, bs, tm=512)
        p_new.append(st_p)
        s_new.append(st_s)

    y_prompt = jnp.transpose(xp.reshape(tp, bp, D_MODEL)[N_META:], (1, 0, 2))
    y_sample = xs.reshape(bs, dseq, D_MODEL)
    stack = lambda new, i: jnp.stack([st[i] for st in new], axis=0)
    return (y_prompt, y_sample, *[stack(p_new, i) for i in range(7)], *[stack(s_new, i) for i in range(7)])
```

```python
import functools

import jax
import jax.numpy as jnp
from jax import lax
from jax.experimental import pallas as pl
from jax.experimental.pallas import tpu as pltpu

F32 = jnp.float32
BF16 = jnp.bfloat16

LANES = 128
D_MODEL = 1024
D_FF = 2816
N_META = 16
EPS = 1e-6
HEADS = 4
DK = 64
QKV_W = 768
CHUNK = 64
DELTA_BLOCK = 8
PREP_GROUP = 8
SOLVE_BLOCK = 16
SOLVE_PASSES = 1
S5_STATES = 1024
LRU_C = 8.0
CV_K = 31
LRU_K = 4
DN_K = 4

FF_CHUNK = 256
FF_NCHUNK = D_FF // FF_CHUNK
DELTA_COLS = 1536
BCD_COLS = 1280
BCD_COL_BLOCKS = (0, 1, 4)
MIX_COLS = DELTA_COLS + BCD_COLS

VMEM_LIMIT = 56 * 1024 * 1024


def _const_spec(shape):
    nd = len(shape)
    return pl.BlockSpec(shape, lambda *_: (0,) * nd, pipeline_mode=pl.Buffered(1))


def _bdot(a, b):
    return jnp.dot(a.astype(BF16), b.astype(BF16), preferred_element_type=F32)


def _hdot(a, b):
    return jnp.dot(a, b, precision=lax.Precision.HIGHEST, preferred_element_type=F32)


def _rms(x, g):
    return x * lax.rsqrt(jnp.mean(x * x, axis=-1, keepdims=True) + EPS) * g


def _silu(x):
    return x * jax.nn.sigmoid(x)


def _softplus(x):
    return jnp.maximum(x, 0.0) + jnp.log1p(jnp.exp(-jnp.abs(x)))


def _swiglu_residual(x, g_ref, wg_ref, wu_ref, wd_ref):
    xn = _rms(x, g_ref[...]).astype(BF16)
    acc = jnp.zeros_like(x)
    for c in range(FF_NCHUNK):
        gate = jnp.dot(xn, wg_ref[c], preferred_element_type=F32)
        up = jnp.dot(xn, wu_ref[c], preferred_element_type=F32)
        h = (_silu(gate) * up).astype(BF16)
        acc = acc + jnp.dot(h, wd_ref[c], preferred_element_type=F32)
    return x + 0.5 * acc


def _ffn_inproj(x_ref, g1_ref, wg_ref, wu_ref, wd_ref, gm_ref, wmix_ref, x1_ref, zb_ref):
    x1 = _swiglu_residual(x_ref[...], g1_ref, wg_ref, wu_ref, wd_ref)
    x1_ref[...] = x1
    u = _rms(x1, gm_ref[...]).astype(BF16)
    for c in range(BCD_COLS // 256):
        sl = slice(c * 256, (c + 1) * 256)
        zb_ref[:, sl] = jnp.dot(u, wmix_ref[:, DELTA_COLS + c * 256:DELTA_COLS + (c + 1) * 256],
                                preferred_element_type=F32)
    return lambda c: jnp.dot(u, wmix_ref[:, c * 256:(c + 1) * 256], preferred_element_type=F32)


def _pre_kernel(x_ref, g1_ref, wg_ref, wu_ref, wd_ref, gm_ref, wmix_ref, x1_ref, zd_ref, zb_ref):
    zd_block = _ffn_inproj(x_ref, g1_ref, wg_ref, wu_ref, wd_ref, gm_ref, wmix_ref, x1_ref, zb_ref)
    for c in range(DELTA_COLS // 256):
        zd_ref[:, c * 256:(c + 1) * 256] = zd_block(c)


def _pre_tm_kernel(bsz, x_ref, g1_ref, wg_ref, wu_ref, wd_ref, gm_ref, wmix_ref, cw_ref, buf_ref,
                   x1_ref, zd_ref, zb_ref, nbuf_ref, stage, carry):
    tm = x_ref.shape[0]
    lb = (DN_K - 1) * bsz

    @pl.when(pl.program_id(0) == 0)
    def _():
        carry[...] = buf_ref[...]

    zd_block = _ffn_inproj(x_ref, g1_ref, wg_ref, wu_ref, wd_ref, gm_ref, wmix_ref, x1_ref, zb_ref)
    for c in range(DELTA_COLS // 256):
        sl = slice(c * 256, (c + 1) * 256)
        zc = zd_block(c)
        if c < QKV_W // 256:
            win = jnp.concatenate([carry[:, sl], zc], axis=0)
            y = cw_ref[0:1, sl] * win[0:tm]
            for k in range(1, DN_K):
                y = y + cw_ref[k:k + 1, sl] * win[k * bsz:k * bsz + tm]
            carry[:, sl] = zc[tm - lb:, :]
            nbuf_ref[:, sl] = zc[tm - lb:, :]
            zc = _silu(y)
        stage[2 * c] = zc[:, 0:LANES]
        stage[2 * c + 1] = zc[:, LANES:2 * LANES]
    for b in range(bsz):
        for j in range(DELTA_COLS // LANES):
            zd_ref[b, :, j * LANES:(j + 1) * LANES] = stage[j, pl.ds(b, tm // bsz, stride=bsz), :]


def _pre_call(x, lw, tm, bsz=0, conv_buf=None):
    n = x.shape[0]
    row = lambda w: pl.BlockSpec((tm, w), lambda i: (i, 0))
    in_specs = [row(D_MODEL), _const_spec((1, D_MODEL)),
                _const_spec((FF_NCHUNK, D_MODEL, FF_CHUNK)), _const_spec((FF_NCHUNK, D_MODEL, FF_CHUNK)),
                _const_spec((FF_NCHUNK, FF_CHUNK, D_MODEL)), _const_spec((1, D_MODEL)),
                _const_spec((D_MODEL, MIX_COLS))]
    args = [x, lw["ffn1_norm"], lw["ffn1_wg"], lw["ffn1_wu"], lw["ffn1_wd"], lw["mix_norm"], lw["w_mix"]]
    out_specs = [row(D_MODEL), row(DELTA_COLS), row(BCD_COLS)]
    out_shape = [jax.ShapeDtypeStruct((n, D_MODEL), F32), jax.ShapeDtypeStruct((n, DELTA_COLS), F32),
                 jax.ShapeDtypeStruct((n, BCD_COLS), F32)]
    body, scratch = _pre_kernel, []
    if bsz:
        lb = (DN_K - 1) * bsz
        body = functools.partial(_pre_tm_kernel, bsz)
        in_specs += [_const_spec((DN_K, QKV_W)), _const_spec((lb, QKV_W))]
        args += [lw["dn_conv_w"], conv_buf]
        out_specs[1] = pl.BlockSpec((bsz, tm // bsz, DELTA_COLS), lambda i: (0, i, 0))
        out_shape[1] = jax.ShapeDtypeStruct((bsz, n // bsz, DELTA_COLS), F32)
        out_specs.append(pl.BlockSpec((lb, QKV_W), lambda i: (0, 0)))
        out_shape.append(jax.ShapeDtypeStruct((lb, QKV_W), F32))
        scratch = [pltpu.VMEM((DELTA_COLS // LANES, tm, LANES), F32), pltpu.VMEM((lb, QKV_W), F32)]
    return pl.pallas_call(
        body, grid=(n // tm,), in_specs=in_specs, out_specs=out_specs, out_shape=out_shape,
        scratch_shapes=scratch,
        compiler_params=pltpu.CompilerParams(dimension_semantics=("arbitrary",), vmem_limit_bytes=VMEM_LIMIT),
        name="pre",
    )(*args)


def _post_kernel(final, bsz, x1_ref, oa_ref, obcd_ref, gm_ref, wgate_ref, wbr_ref, wout_ref,
                 g2_ref, wg_ref, wu_ref, wd_ref, gf_ref, out_ref, *scratch):
    x1 = x1_ref[...]
    if bsz:
        for b in range(bsz):
            for j in range(256 // LANES):
                scratch[0][j, pl.ds(b, x1.shape[0] // bsz, stride=bsz), :] = oa_ref[b, :, j * LANES:(j + 1) * LANES]
        oa = jnp.concatenate([scratch[0][0], scratch[0][1]], axis=1)
    else:
        oa = oa_ref[...]
    u = _rms(x1, gm_ref[...]).astype(BF16)
    m = jnp.zeros_like(x1)
    for i in range(4):
        gates = jax.nn.sigmoid(jnp.dot(u, wgate_ref[:, i * D_MODEL:(i + 1) * D_MODEL],
                                       preferred_element_type=F32))
        br = oa if i == 0 else obcd_ref[:, (i - 1) * 256:i * 256]
        m = m + gates * _bdot(br, wbr_ref[i])
    x2 = x1 + _bdot(m, wout_ref[...])
    x3 = _swiglu_residual(x2, g2_ref, wg_ref, wu_ref, wd_ref)
    if final:
        x3 = _rms(x3, gf_ref[...])
    out_ref[...] = x3


def _post_call(x1, oa, obcd, lw, final_norm, final, tm, bsz=0):
    n = x1.shape[0]
    row = lambda w: pl.BlockSpec((tm, w), lambda i: (i, 0))
    oa_spec = pl.BlockSpec((bsz, tm // bsz, 256), lambda i: (0, i, 0)) if bsz else row(256)
    return pl.pallas_call(
        functools.partial(_post_kernel, final, bsz),
        grid=(n // tm,),
        scratch_shapes=[pltpu.VMEM((256 // LANES, tm, LANES), F32)] if bsz else [],
        in_specs=[row(D_MODEL), oa_spec, row(768), _const_spec((1, D_MODEL)),
                  _const_spec((D_MODEL, 4 * D_MODEL)), _const_spec((4, 256, D_MODEL)),
                  _const_spec((D_MODEL, D_MODEL)), _const_spec((1, D_MODEL)),
                  _const_spec((FF_NCHUNK, D_MODEL, FF_CHUNK)), _const_spec((FF_NCHUNK, D_MODEL, FF_CHUNK)),
                  _const_spec((FF_NCHUNK, FF_CHUNK, D_MODEL)), _const_spec((1, D_MODEL))],
        out_specs=row(D_MODEL),
        out_shape=jax.ShapeDtypeStruct((n, D_MODEL), F32),
        compiler_params=pltpu.CompilerParams(dimension_semantics=("arbitrary",), vmem_limit_bytes=VMEM_LIMIT),
        name="post",
    )(x1, oa, obcd, lw["mix_norm"], lw["w_gate"], lw["w_branch"], lw["w_out"],
      lw["ffn2_norm"], lw["ffn2_wg"], lw["ffn2_wu"], lw["ffn2_wd"], final_norm)


def _split16(x):
    hi = x.astype(BF16)
    if SOLVE_PASSES == 1:
        return (hi,)
    return hi, (x - hi.astype(F32)).astype(BF16)


def _bmm(a, b, contract=(2, 1)):
    dims = (((contract[0],), (contract[1],)), ((0,), (0,)))
    return lax.dot_general(a, b, dims, preferred_element_type=F32)


def _dot3(a, b):
    out = _bmm(a[0], b[0])
    if SOLVE_PASSES >= 2:
        out = out + _bmm(a[1], b[0])
    if SOLVE_PASSES >= 3:
        out = out + _bmm(a[0], b[1])
    return out


def _chunk_masks(seq_len):
    ri = lax.broadcasted_iota(jnp.int32, (CHUNK, CHUNK), 0)
    ci = lax.broadcasted_iota(jnp.int32, (CHUNK, CHUNK), 1)
    causal = ri >= ci
    strict = ri > ci
    if seq_len < CHUNK:
        same = (ri // seq_len) == (ci // seq_len)
        causal = causal & same
        strict = strict & same
    return dict(causal_f=causal.astype(F32), strict_f=strict.astype(F32), eye_f=(ri == ci).astype(F32),
                blk_f=((ri // SOLVE_BLOCK) == (ci // SOLVE_BLOCK)).astype(F32), rowseq=ri // seq_len,
                rowpos=lax.broadcasted_iota(jnp.int32, (CHUNK, 256), 0) % seq_len)


def _wy_solve(low, rhs, mk, seq_len):
    eye_f = mk["eye_f"]
    if seq_len > SOLVE_BLOCK:
        nd = low * mk["blk_f"]
        off = low - nd
        blk = SOLVE_BLOCK
    else:
        nd, off, blk = low, None, seq_len
    p = eye_f - nd
    ns = _split16(nd)
    for _ in range(blk.bit_length() - 2):
        ns = _split16(_dot3(ns, ns))
        p = p + _dot3(_split16(p), ns)
    ps = _split16(p)
    y = _dot3(ps, _split16(rhs))
    if off is None:
        return y
    assert seq_len // SOLVE_BLOCK == 4
    ms = _split16(_dot3(ps, _split16(off)))
    y2 = y + _dot3(_split16(_dot3(ms, ms)), _split16(y))
    return y2 - _dot3(ms, _split16(y2))


def _delta_prepare(chunks, mk, seq_len, scr, slots):
    u_s, w_s, qe_s, kd_s, a_s, egl_s = scr
    qs, ks, vs, bs, gcs = [], [], [], [], []
    for q_in, k_in, v_in, beta, gl in chunks:
        gc = gl
        shift = 1
        while shift < seq_len:
            gc = gc + jnp.where(mk["rowpos"] >= shift, pltpu.roll(gc, shift, axis=0), 0.0)
            shift *= 2
        for h in range(HEADS):
            sl = slice(h * DK, (h + 1) * DK)
            qs.append(q_in[:, sl])
            ks.append(k_in[:, sl])
            vs.append(v_in[:, sl])
            bs.append(beta[:, sl])
            gcs.append(gc[:, sl])
    q, k, v, b, gc = [jnp.stack(a, axis=0) for a in (qs, ks, vs, bs, gcs)]
    q = q * lax.rsqrt(jnp.sum(q * q, axis=-1, keepdims=True) + EPS) * (DK ** -0.5)
    k = k * lax.rsqrt(jnp.sum(k * k, axis=-1, keepdims=True) + EPS)
    gct = jnp.sum(gc * mk["eye_f"], axis=1, keepdims=True)
    dec = jnp.exp(jnp.minimum(gc - gct, 0.0))
    eg = jnp.exp(gc)
    kb = k * b
    k16 = k.astype(BF16)
    kk = _bmm(kb.astype(BF16), k16, (2, 2))
    qk = _bmm(q.astype(BF16), k16, (2, 2))
    low = kk * (dec * mk["strict_f"])
    x = _wy_solve(low, jnp.concatenate([v * b, kb * eg], axis=2), mk, seq_len)
    if seq_len == CHUNK:
        glast = gc[:, CHUNK - 1:CHUNK, :]
    else:
        glast = jnp.concatenate(
            [jnp.broadcast_to(gc[:, (j + 1) * seq_len - 1:(j + 1) * seq_len, :], (gc.shape[0], seq_len, DK))
             for j in range(CHUNK // seq_len)], axis=1)
    w16 = x[:, :, DK:2 * DK].astype(BF16)
    qe16 = (q * eg).astype(BF16)
    kd16 = (k * jnp.exp(glast - gc)).astype(BF16)
    a16 = (qk * (dec * mk["causal_f"])).astype(BF16)
    egl = jnp.exp(glast)
    for c, slot in enumerate(slots):
        rows = pl.ds(slot * CHUNK, CHUNK)
        ps = slice(c * HEADS, (c + 1) * HEADS)
        u_s[:, rows, :] = x[ps, :, 0:DK]
        w_s[:, rows, :] = w16[ps]
        qe_s[:, rows, :] = qe16[ps]
        kd_s[:, rows, :] = kd16[ps]
        a_s[:, rows, :] = a16[ps]
        egl_s[:, rows, :] = jnp.broadcast_to(egl[ps], (HEADS, CHUNK, DK))


def _gated_norm(o, dz, ng):
    return o * lax.rsqrt(jnp.mean(o * o, axis=-1, keepdims=True) + EPS) * ng * _silu(dz)


def _delta_scratch(nslots):
    n = nslots * CHUNK
    return ([pltpu.VMEM((HEADS, n, DK), F32)] + [pltpu.VMEM((HEADS, n, DK), BF16)] * 4
            + [pltpu.VMEM((HEADS, n, DK), F32)])


TN_DIMS = (((0,), (0,)), ((), ()))


def _delta_kernel(t_len, q_ref, k_ref, v_ref, dz_ref, be_ref, ae_ref, s0_ref, alog_ref, dtb_ref, ng_ref,
                  o_ref, s_ref, *scr):
    t0 = t_len % CHUNK
    nblocks = (t_len // CHUNK) // DELTA_BLOCK
    assert nblocks * DELTA_BLOCK * CHUNK + t0 == t_len
    s_ref[...] = s0_ref[...]
    u_s, w_s, qe_s, kd_s, a_s, egl_s = scr

    mk = _chunk_masks(CHUNK)
    neg_a = -jnp.exp(alog_ref[...])
    dtb = dtb_ref[...]
    ng = ng_ref[...]

    def prep(r0, n):
        rows = pl.ds(r0, n)
        beta = jax.nn.sigmoid(be_ref[rows, :])
        gl = neg_a * _softplus(ae_ref[rows, :] + dtb)
        return q_ref[rows, :], k_ref[rows, :], v_ref[rows, :], beta, gl

    def state_step(slot, r0, n):
        rows = pl.ds(slot * CHUNK, CHUNK)
        s = s_ref[...]
        s16 = s.astype(BF16)
        vnew = u_s[:, rows, :] - _bmm(w_s[:, rows, :], s16)
        vn16 = vnew.astype(BF16)
        o = _bmm(qe_s[:, rows, :], s16) + _bmm(a_s[:, rows, :], vn16)
        s_ref[...] = (s * egl_s[:, pl.ds(slot * CHUNK, 8), :][:, 0:1, :]
                      + _bmm(kd_s[:, rows, :], vn16, (1, 1)))
        for h in range(HEADS):
            sl = slice(h * DK, (h + 1) * DK)
            o_ref[pl.ds(r0, n), sl] = _gated_norm(o[h, CHUNK - n:, :], dz_ref[pl.ds(r0, n), sl], ng)

    if t0:
        pad = CHUNK - t0
        parts = [jnp.concatenate([jnp.zeros((pad, a.shape[1]), F32), a], axis=0) for a in prep(0, t0)]
        _delta_prepare([parts], mk, CHUNK, scr, [0])
        state_step(0, 0, t0)

    def block(blk, carry):
        base = t0 + blk * (DELTA_BLOCK * CHUNK)

        def prepare(g, carry):
            slots = [g * PREP_GROUP + i for i in range(PREP_GROUP)]
            chunks = [prep(pl.multiple_of(base + c * CHUNK, 8), CHUNK) for c in slots]
            _delta_prepare(chunks, mk, CHUNK, scr, slots)
            return carry
        lax.fori_loop(0, DELTA_BLOCK // PREP_GROUP, prepare, 0)

        def step(c, carry):
            state_step(c, pl.multiple_of(base + c * CHUNK, 8), CHUNK)
            return carry
        lax.fori_loop(0, DELTA_BLOCK, step, 0)
        return carry
    if nblocks:
        lax.fori_loop(0, nblocks, block, 0)


def _delta_call(zd, s0, lw):
    bsz, t_len, _ = zd.shape
    sq = pl.Squeezed()
    zspec = lambda blk: pl.BlockSpec((sq, t_len, 256), lambda b: (b, 0, blk))
    sspec = pl.BlockSpec((sq, HEADS, DK, DK), lambda b: (b, 0, 0, 0))
    return pl.pallas_call(
        functools.partial(_delta_kernel, t_len),
        grid=(bsz,),
        in_specs=[zspec(0), zspec(1), zspec(2), zspec(3), zspec(4), zspec(5), sspec,
                  _const_spec((1, 256)), _const_spec((1, 256)), _const_spec((1, DK))],
        out_specs=[pl.BlockSpec((sq, t_len, 256), lambda b: (b, 0, 0)), sspec],
        out_shape=[jax.ShapeDtypeStruct((bsz, t_len, 256), F32),
                   jax.ShapeDtypeStruct((bsz, HEADS, DK, DK), F32)],
        scratch_shapes=_delta_scratch(DELTA_BLOCK),
        compiler_params=pltpu.CompilerParams(dimension_semantics=("arbitrary",), vmem_limit_bytes=VMEM_LIMIT),
        name="delta",
    )(zd, zd, zd, zd, zd, zd, s0, lw["dn_a_log"], lw["dn_dt_bias"], lw["dn_norm"])


def _delta_short_kernel(seq_len, q_ref, k_ref, v_ref, dz_ref, be_ref, ae_ref, buf_ref, s0_ref, cw_ref,
                        alog_ref, dtb_ref, ng_ref, o_ref, nbuf_ref, sfin_ref, xs_ref, *scr):
    nseq = CHUNK // seq_len
    span = seq_len + 8
    mk = _chunk_masks(seq_len)
    cw = cw_ref[...]
    xs_ref[...] = jnp.zeros(xs_ref.shape, F32)
    for j in range(nseq):
        xs_ref[j * span + 5:j * span + 8, :] = buf_ref[j]
        rows = slice(j * seq_len, (j + 1) * seq_len)
        xs_ref[j * span + 8:(j + 1) * span, 0:256] = q_ref[rows, :]
        xs_ref[j * span + 8:(j + 1) * span, 256:512] = k_ref[rows, :]
        xs_ref[j * span + 8:(j + 1) * span, 512:768] = v_ref[rows, :]
    ys = []
    for j in range(nseq):
        win = xs_ref[j * span:(j + 1) * span, :]
        ys.append(cw[3:4] * win[8:span] + cw[2:3] * win[7:span - 1]
                  + cw[1:2] * win[6:span - 2] + cw[0:1] * win[5:span - 3])
        nbuf_ref[j] = win[span - 3:span, :]
    qkv = _silu(jnp.concatenate(ys, axis=0))
    beta = jax.nn.sigmoid(be_ref[...])
    gl = -jnp.exp(alog_ref[...]) * _softplus(ae_ref[...] + dtb_ref[...])
    _delta_prepare([(qkv[:, 0:256], qkv[:, 256:512], qkv[:, 512:768], beta, gl)], mk, seq_len, scr, [0])

    u_s, w_s, qe_s, kd_s, a_s, egl_s = scr
    w = w_s[...]
    qe = qe_s[...]
    kd = kd_s[...].astype(F32)
    u = u_s[...]
    ws, qs = [], []
    for j in range(nseq):
        s16 = s0_ref[j].astype(BF16)
        ws.append(_bmm(w, s16))
        qs.append(_bmm(qe, s16))
    vnew, o = u, jnp.zeros_like(u)
    for j in range(nseq):
        mine = mk["rowseq"] == j
        vnew = jnp.where(mine, u - ws[j], vnew)
        o = jnp.where(mine, qs[j], o)
    vn16 = vnew.astype(BF16)
    o = o + _bmm(a_s[...], vn16)
    for j in range(nseq):
        kdj = jnp.where(mk["rowseq"] == j, kd, 0.0).astype(BF16)
        sfin_ref[j] = (s0_ref[j] * egl_s[:, j * seq_len:(j + 1) * seq_len, :][:, 0:1, :]
                       + _bmm(kdj, vn16, (1, 1)))
    ng = ng_ref[...]
    for h in range(HEADS):
        sl = slice(h * DK, (h + 1) * DK)
        o_ref[:, sl] = _gated_norm(o[h], dz_ref[:, sl], ng)


def _delta_short_call(z, seq_len, buf, s0, lw):
    n = z.shape[0]
    nseq = CHUNK // seq_len
    zspec = lambda blk: pl.BlockSpec((CHUNK, 256), lambda g: (g, blk))
    bspec = pl.BlockSpec((nseq, DN_K - 1, QKV_W), lambda g: (g, 0, 0))
    sspec = pl.BlockSpec((nseq, HEADS, DK, DK), lambda g: (g, 0, 0, 0))
    return pl.pallas_call(
        functools.partial(_delta_short_kernel, seq_len),
        grid=(n // CHUNK,),
        in_specs=[zspec(0), zspec(1), zspec(2), zspec(3), zspec(4), zspec(5), bspec, sspec, _const_spec((DN_K, QKV_W)), _const_spec((1, 256)), _const_spec((1, 256)),
                  _const_spec((1, DK))],
        out_specs=[pl.BlockSpec((CHUNK, 256), lambda g: (g, 0)), bspec, sspec],
        out_shape=[jax.ShapeDtypeStruct((n, 256), F32), jax.ShapeDtypeStruct(buf.shape, F32),
                   jax.ShapeDtypeStruct(s0.shape, F32)],
        scratch_shapes=[pltpu.VMEM((nseq * (seq_len + 8), QKV_W), F32)] + _delta_scratch(1),
        compiler_params=pltpu.CompilerParams(dimension_semantics=("arbitrary",), vmem_limit_bytes=VMEM_LIMIT),
        name="delta_short",
    )(z, z, z, z, z, z, buf, s0, lw["dn_conv_w"], lw["dn_a_log"], lw["dn_dt_bias"], lw["dn_norm"])


def _scan_time_major(t_len, bsz, state_refs, step):
    def run_group(goff):
        hs = tuple(r[pl.ds(goff, 8), :] for r in state_refs)
        if t_len <= 8:
            for t in range(t_len):
                hs = step(hs, t * bsz + goff)
        else:
            def body(t, hs):
                return step(hs, pl.multiple_of(t * bsz + goff, 8))
            hs = lax.fori_loop(0, t_len, body, hs, unroll=3)
        for r, h in zip(state_refs, hs):
            r[pl.ds(goff, 8), :] = h

    if bsz == 8:
        run_group(0)
    else:
        def gbody(g, carry):
            run_group(pl.multiple_of(g * 8, 8))
            return carry
        lax.fori_loop(0, bsz // 8, gbody, 0)


def _bcd_kernel(t_len, bsz, nsteps,
                lru_ref, cv_ref, s5_ref, s5re0, s5im0, lru0, lbuf0, cbuf0,
                lam_re_ref, lam_im_ref, lstep_ref, wb_ref, wcre_ref, wcim_ref, dskip_ref, wglu_ref, bglu_ref,
                lcw_ref, lcb_ref, wa_ref, ba_ref, wx_ref, bx_ref, llam_ref,
                ccw_ref, ccb_ref, lng_ref, lnb_ref,
                o_ref, s5re_o, s5im_o, lru_o, lbuf_o, cbuf_o,
                xr, xi, lxs, cxs, a_s, b_s):
    rows = t_len * bsz
    lb = (LRU_K - 1) * bsz
    cb = (CV_K - 1) * bsz

    @pl.when(pl.program_id(0) == 0)
    def _():
        s5re_o[...] = s5re0[...]
        s5im_o[...] = s5im0[...]
        lru_o[...] = lru0[...]
        lxs[0:lb, :] = lbuf0[...]
        cxs[0:cb, :] = cbuf0[...]

    lxs[lb:lb + rows, :] = lru_ref[:, 0:256]
    xf = lcb_ref[...] + lcw_ref[0:1, :] * lxs[0:rows, :]
    for k in range(1, LRU_K):
        xf = xf + lcw_ref[k:k + 1, :] * lxs[k * bsz:k * bsz + rows, :]
    r = jax.nn.sigmoid(_bdot(xf, wa_ref[...]) + ba_ref[...])
    i = jax.nn.sigmoid(_bdot(xf, wx_ref[...]) + bx_ref[...])
    log_a = (-LRU_C) * r * _softplus(-llam_ref[...])
    a_s[...] = jnp.exp(log_a)
    b_s[...] = jnp.sqrt(1.0 - jnp.exp(2.0 * log_a)) * (i * xf)

    def lru_step(hs, row):
        h = a_s[pl.ds(row, 8), :] * hs[0] + b_s[pl.ds(row, 8), :]
        b_s[pl.ds(row, 8), :] = h
        return (h,)
    _scan_time_major(t_len, bsz, (lru_o,), lru_step)
    o_ref[:, 256:512] = b_s[...] * jax.nn.gelu(lru_ref[:, 256:512])
    lbuf_o[...] = lxs[rows:rows + lb, :]
    if nsteps > 1:
        lxs[0:lb, :] = lxs[rows:rows + lb, :]

    dt = jnp.exp(lstep_ref[...])
    lam_re = lam_re_ref[...]
    lam_im = lam_im_ref[...]
    mag = jnp.exp(lam_re * dt)
    lb_re = mag * jnp.cos(lam_im * dt)
    lb_im = mag * jnp.sin(lam_im * dt)
    den = lam_re * lam_re + lam_im * lam_im
    cf_re = ((lb_re - 1.0) * lam_re + lb_im * lam_im) / den
    cf_im = (lb_im * lam_re - (lb_re - 1.0) * lam_im) / den
    u16 = s5_ref[...].astype(BF16)
    bu_re = jnp.dot(u16, wb_ref[:, 0:S5_STATES], preferred_element_type=F32)
    bu_im = jnp.dot(u16, wb_ref[:, S5_STATES:2 * S5_STATES], preferred_element_type=F32)
    xr[...] = cf_re * bu_re - cf_im * bu_im
    xi[...] = cf_re * bu_im + cf_im * bu_re
    lbr = jnp.broadcast_to(lb_re, (8, S5_STATES))
    lbi = jnp.broadcast_to(lb_im, (8, S5_STATES))

    def s5_step(hs, row):
        hr, hi = hs
        nr = lbr * hr - lbi * hi + xr[pl.ds(row, 8), :]
        ni = lbr * hi + lbi * hr + xi[pl.ds(row, 8), :]
        xr[pl.ds(row, 8), :] = nr
        xi[pl.ds(row, 8), :] = ni
        return (nr, ni)
    _scan_time_major(t_len, bsz, (s5re_o, s5im_o), s5_step)
    y = (_bdot(xr[...], wcre_ref[...]) - _bdot(xi[...], wcim_ref[...])
         + dskip_ref[...] * s5_ref[...])
    y = jax.nn.gelu(y)
    glu = _bdot(y, wglu_ref[...]) + bglu_ref[...]
    o_ref[:, 0:256] = glu[:, 0:256] * jax.nn.sigmoid(glu[:, 256:512])

    cxs[cb:cb + rows, :] = cv_ref[:, 0:256] * jax.nn.sigmoid(cv_ref[:, 256:512])
    yc = ccb_ref[...] + ccw_ref[0:1, :] * cxs[0:rows, :]
    for k in range(1, CV_K):
        yc = yc + ccw_ref[k:k + 1, :] * cxs[k * bsz:k * bsz + rows, :]
    mu = jnp.mean(yc, axis=-1, keepdims=True)
    ycc = yc - mu
    yn = ycc * lax.rsqrt(jnp.mean(ycc * ycc, axis=-1, keepdims=True) + EPS) * lng_ref[...] + lnb_ref[...]
    o_ref[:, 512:768] = _silu(yn)
    cbuf_o[...] = cxs[rows:rows + cb, :]
    if nsteps > 1:
        cxs[0:cb, :] = cxs[rows:rows + cb, :]


def _bcd_call(zmix, states, lw, t_len, bsz, tb):
    col_blocks = BCD_COL_BLOCKS
    n = zmix.shape[0]
    nsteps = t_len // tb
    rows = tb * bsz
    lb = (LRU_K - 1) * bsz
    cb = (CV_K - 1) * bsz
    assert nsteps == 1 or rows >= cb
    zspec = lambda w, blk: pl.BlockSpec((rows, w), lambda i: (i, blk))
    state_shapes = [(bsz, S5_STATES), (bsz, S5_STATES), (bsz, 256), (lb, 256), (cb, 256)]
    params = [lw["s5_lam_re"], lw["s5_lam_im"], lw["s5_log_step"], lw["s5_wb"], lw["s5_wcre"], lw["s5_wcim"],
              lw["s5_d"], lw["s5_w_glu"], lw["s5_b_glu"],
              lw["lru_conv_w"], lw["lru_conv_b"], lw["lru_wa"], lw["lru_b_a"], lw["lru_wx"], lw["lru_b_x"],
              lw["lru_lam"], lw["cv_conv_w"], lw["cv_conv_b"], lw["cv_ln_g"], lw["cv_ln_b"]]
    return pl.pallas_call(
        functools.partial(_bcd_kernel, tb, bsz, nsteps),
        grid=(nsteps,),
        in_specs=([zspec(512, col_blocks[0]), zspec(512, col_blocks[1]), zspec(256, col_blocks[2])]
                  + [_const_spec(s) for s in state_shapes]
                  + [_const_spec(p.shape) for p in params]),
        out_specs=[pl.BlockSpec((rows, 768), lambda i: (i, 0))]
                  + [pl.BlockSpec(s, lambda i: (0, 0)) for s in state_shapes],
        out_shape=[jax.ShapeDtypeStruct((n, 768), F32)]
                  + [jax.ShapeDtypeStruct(s, F32) for s in state_shapes],
        scratch_shapes=[pltpu.VMEM((rows, S5_STATES), F32), pltpu.VMEM((rows, S5_STATES), F32),
                        pltpu.VMEM((lb + rows, 256), F32), pltpu.VMEM((cb + rows, 256), F32),
                        pltpu.VMEM((rows, 256), F32), pltpu.VMEM((rows, 256), F32)],
        compiler_params=pltpu.CompilerParams(dimension_semantics=("arbitrary",), vmem_limit_bytes=VMEM_LIMIT),
        name="bcd",
    )(zmix, zmix, zmix, *states, *params)


def _block_diag(m):
    g, r, c = m.shape
    return (jnp.eye(g, dtype=m.dtype)[:, None, :, None] * m[:, :, None, :]).reshape(g * r, g * c)


def _layer_weights(l, p):
    def ffn(w_gu, w_down):
        split = lambda w: w.reshape(D_MODEL, FF_NCHUNK, FF_CHUNK).transpose(1, 0, 2).astype(BF16)
        return split(w_gu[:, :D_FF]), split(w_gu[:, D_FF:]), w_down.reshape(FF_NCHUNK, FF_CHUNK, D_MODEL).astype(BF16)

    row = lambda v: v.reshape(1, -1).astype(F32)
    w_in = p["w_in"][l]
    w_mix = jnp.concatenate(
        [w_in[:, 0:1024], jnp.repeat(w_in[:, 1024:1028], DK, axis=1), jnp.repeat(w_in[:, 1028:1032], DK, axis=1),
         w_in[:, 1288:1800], w_in[:, 1800:2312], w_in[:, 1032:1288]], axis=1)
    lw = {}
    lw["ffn1_wg"], lw["ffn1_wu"], lw["ffn1_wd"] = ffn(p["ffn1_w_gu"][l], p["ffn1_w_down"][l])
    lw["ffn2_wg"], lw["ffn2_wu"], lw["ffn2_wd"] = ffn(p["ffn2_w_gu"][l], p["ffn2_w_down"][l])
    lw["ffn1_norm"] = row(p["ffn1_norm"][l])
    lw["ffn2_norm"] = row(p["ffn2_norm"][l])
    lw["mix_norm"] = row(p["mix_norm"][l])
    lw["w_mix"] = w_mix.astype(BF16)
    lw["w_gate"] = w_in[:, 2312:].astype(BF16)
    lw["w_branch"] = p["w_branch"][l].astype(BF16)
    lw["w_out"] = p["w_out"][l].astype(BF16)
    lw["dn_conv_w"] = p["dn_conv_w"][l]
    lw["dn_a_log"] = row(jnp.repeat(p["dn_a_log"][l], DK))
    lw["dn_dt_bias"] = row(jnp.repeat(p["dn_dt_bias"][l], DK))
    lw["dn_norm"] = row(p["dn_norm"][l])
    lw["s5_lam_re"] = row(p["s5_lam_re"][l])
    lw["s5_lam_im"] = row(p["s5_lam_im"][l])
    lw["s5_log_step"] = row(jnp.repeat(p["s5_log_step"][l], 64))
    bdt = lambda w: _block_diag(jnp.swapaxes(w, 1, 2)).astype(BF16)
    lw["s5_wb"] = jnp.concatenate([bdt(p["s5_b_re"][l]), bdt(p["s5_b_im"][l])], axis=1)
    lw["s5_wcre"] = bdt(p["s5_c_re"][l])
    lw["s5_wcim"] = bdt(p["s5_c_im"][l])
    lw["s5_d"] = row(p["s5_d"][l])
    lw["s5_w_glu"] = p["s5_w_glu"][l].astype(BF16)
    lw["s5_b_glu"] = row(p["s5_b_glu"][l])
    lw["lru_conv_w"] = p["lru_conv_w"][l]
    lw["lru_conv_b"] = row(p["lru_conv_b"][l])
    lw["lru_wa"] = _block_diag(p["lru_w_a"][l]).astype(BF16)
    lw["lru_wx"] = _block_diag(p["lru_w_x"][l]).astype(BF16)
    lw["lru_b_a"] = row(p["lru_b_a"][l])
    lw["lru_b_x"] = row(p["lru_b_x"][l])
    lw["lru_lam"] = row(p["lru_lam"][l])
    lw["cv_conv_w"] = p["cv_conv_w"][l]
    lw["cv_conv_b"] = row(p["cv_conv_b"][l])
    lw["cv_ln_g"] = row(p["cv_ln_g"][l])
    lw["cv_ln_b"] = row(p["cv_ln_b"][l])
    return lw


def _to_time_major(a):
    bsz, k, c = a.shape
    return jnp.transpose(a, (1, 0, 2)).reshape(k * bsz, c)


def _from_time_major(a, bsz):
    k = a.shape[0] // bsz
    return jnp.transpose(a.reshape(k, bsz, a.shape[1]), (1, 0, 2))


def _bcd_states(st, bsz):
    _, _, s_re, s_im, s_lru, s_lruc, s_cv = st
    return (s_re.reshape(bsz, S5_STATES), s_im.reshape(bsz, S5_STATES), s_lru,
            _to_time_major(s_lruc), _to_time_major(s_cv))


def _new_states(n_dn, n_dnc, bcd_new, bsz):
    n_re, n_im, n_lru, n_lruc, n_cv = bcd_new
    return (n_dn, n_dnc, n_re.reshape(bsz, 16, 64), n_im.reshape(bsz, 16, 64), n_lru,
            _from_time_major(n_lruc, bsz), _from_time_major(n_cv, bsz))


def _layer_long(x, st, lw, final_norm, final, t_len, bsz, tm, tb):
    x1, zd, zb, n_dnc = _pre_call(x, lw, tm, bsz, _to_time_major(st[1]))
    n_dnc = _from_time_major(n_dnc, bsz)
    oa, n_dn = _delta_call(zd, st[0], lw)
    obcd, *bcd_new = _bcd_call(zb, _bcd_states(st, bsz), lw, t_len, bsz, tb)
    x3 = _post_call(x1, oa, obcd, lw, final_norm, final, tm, bsz)
    return x3, _new_states(n_dn, n_dnc, bcd_new, bsz)


def _layer_short(x, st, lw, final_norm, final, t_len, bsz, tm):
    x1, zd, zb = _pre_call(x, lw, tm)
    oa, n_dnc, n_dn = _delta_short_call(zd, t_len, st[1], st[0], lw)
    z_tm = _to_time_major(zb.reshape(bsz, t_len, BCD_COLS))
    obcd_tm, *bcd_new = _bcd_call(z_tm, _bcd_states(st, bsz), lw, t_len, bsz, t_len)
    obcd = _from_time_major(obcd_tm, bsz).reshape(bsz * t_len, 768)
    x3 = _post_call(x1, oa, obcd, lw, final_norm, final, tm)
    return x3, _new_states(n_dn, n_dnc, bcd_new, bsz)


def _zero_state(bsz):
    return (jnp.zeros((bsz, HEADS, DK, DK), F32), jnp.zeros((bsz, DN_K - 1, QKV_W), F32),
            jnp.zeros((bsz, 16, 64), F32), jnp.zeros((bsz, 16, 64), F32), jnp.zeros((bsz, 256), F32),
            jnp.zeros((bsz, LRU_K - 1, 256), F32), jnp.zeros((bsz, CV_K - 1, 256), F32))


def kernel(x_prompt, x_sample, state_delta, state_delta_conv, state_s5_re, state_s5_im, state_lru, state_lru_conv, state_conv, meta_tokens, ffn1_norm, ffn1_w_gu, ffn1_w_down, mix_norm, w_in, dn_conv_w, dn_a_log, dn_dt_bias, dn_norm, s5_lam_re, s5_lam_im, s5_log_step, s5_b_re, s5_b_im, s5_c_re, s5_c_im, s5_d, s5_w_glu, s5_b_glu, lru_conv_w, lru_conv_b, lru_w_a, lru_b_a, lru_w_x, lru_b_x, lru_lam, cv_conv_w, cv_conv_b, cv_ln_g, cv_ln_b, w_branch, w_out, ffn2_norm, ffn2_w_gu, ffn2_w_down, final_norm):
    p = dict(ffn1_norm=ffn1_norm, ffn1_w_gu=ffn1_w_gu, ffn1_w_down=ffn1_w_down, mix_norm=mix_norm, w_in=w_in,
             dn_conv_w=dn_conv_w, dn_a_log=dn_a_log, dn_dt_bias=dn_dt_bias, dn_norm=dn_norm,
             s5_lam_re=s5_lam_re, s5_lam_im=s5_lam_im, s5_log_step=s5_log_step, s5_b_re=s5_b_re,
             s5_b_im=s5_b_im, s5_c_re=s5_c_re, s5_c_im=s5_c_im, s5_d=s5_d, s5_w_glu=s5_w_glu,
             s5_b_glu=s5_b_glu, lru_conv_w=lru_conv_w, lru_conv_b=lru_conv_b, lru_w_a=lru_w_a,
             lru_b_a=lru_b_a, lru_w_x=lru_w_x, lru_b_x=lru_b_x, lru_lam=lru_lam, cv_conv_w=cv_conv_w,
             cv_conv_b=cv_conv_b, cv_ln_g=cv_ln_g, cv_ln_b=cv_ln_b, w_branch=w_branch, w_out=w_out,
             ffn2_norm=ffn2_norm, ffn2_w_gu=ffn2_w_gu, ffn2_w_down=ffn2_w_down)
    depth = w_in.shape[0]
    bp, seq, _ = x_prompt.shape
    bs, dseq, _ = x_sample.shape
    tp = seq + N_META
    fnorm = final_norm.reshape(1, D_MODEL)

    meta = jnp.broadcast_to(meta_tokens[:, None, :], (N_META, bp, D_MODEL))
    xp = jnp.concatenate([meta, jnp.transpose(x_prompt, (1, 0, 2))], axis=0).reshape(tp * bp, D_MODEL)
    xs = x_sample.reshape(bs * dseq, D_MODEL)

    p_new, s_new = [], []
    for l in range(depth):
        lw = _layer_weights(l, p)
        final = l == depth - 1
        xp, st_p = _layer_long(xp, _zero_state(bp), lw, fnorm, final, tp, bp, tm=384, tb=129)
        st_s = (state_delta[l], state_delta_conv[l], state_s5_re[l], state_s5_im[l], state_lru[l],
                state_lru_conv[l], state_conv[l])
        xs, st_s = _layer_short(xs, st_s, lw, fnorm, final, dseq, bs, tm=512)
        p_new.append(st_p)
        s_new.append(st_s)

    y_prompt = jnp.transpose(xp.reshape(tp, bp, D_MODEL)[N_META:], (1, 0, 2))
    y_sample = xs.reshape(bs, dseq, D_MODEL)
    stack = lambda new, i: jnp.stack([st[i] for st in new], axis=0)
    return (y_prompt, y_sample, *[stack(p_new, i) for i in range(7)], *[stack(s_new, i) for i in range(7)])
```

```python
import functools

import jax
import jax.numpy as jnp
from jax import lax
from jax.experimental import pallas as pl
from jax.experimental.pallas import tpu as pltpu

F32 = jnp.float32
BF16 = jnp.bfloat16

LANES = 128
D_MODEL = 1024
D_FF = 2816
N_META = 16
EPS = 1e-6
HEADS = 4
DK = 64
QKV_W = 768
CHUNK = 64
DELTA_BLOCK = 3
SOLVE_BLOCK = 16
SOLVE_PASSES = 1
S5_STATES = 1024
LRU_C = 8.0
CV_K = 31
LRU_K = 4
DN_K = 4

FF_CHUNK = 256
FF_NCHUNK = D_FF // FF_CHUNK
DELTA_COLS = 1536
BCD_COLS = 1280
BCD_COL_BLOCKS = (0, 1, 4)
MIX_COLS = DELTA_COLS + BCD_COLS

VMEM_LIMIT = 56 * 1024 * 1024


def _const_spec(shape):
    nd = len(shape)
    return pl.BlockSpec(shape, lambda *_: (0,) * nd, pipeline_mode=pl.Buffered(1))


def _bdot(a, b):
    return jnp.dot(a.astype(BF16), b.astype(BF16), preferred_element_type=F32)


def _hdot(a, b):
    return jnp.dot(a, b, precision=lax.Precision.HIGHEST, preferred_element_type=F32)


def _rms(x, g):
    return x * lax.rsqrt(jnp.mean(x * x, axis=-1, keepdims=True) + EPS) * g


def _silu(x):
    return x * jax.nn.sigmoid(x)


def _softplus(x):
    return jnp.maximum(x, 0.0) + jnp.log1p(jnp.exp(-jnp.abs(x)))


def _swiglu_residual(x, g_ref, wgu_ref, wd_ref):
    xn = _rms(x, g_ref[...]).astype(BF16)
    acc = jnp.zeros_like(x)
    for c in range(FF_NCHUNK):
        lo, hi = c * FF_CHUNK, (c + 1) * FF_CHUNK
        gate = jnp.dot(xn, wgu_ref[:, lo:hi], preferred_element_type=F32)
        up = jnp.dot(xn, wgu_ref[:, D_FF + lo:D_FF + hi], preferred_element_type=F32)
        h = (_silu(gate) * up).astype(BF16)
        acc = acc + jnp.dot(h, wd_ref[lo:hi, :], preferred_element_type=F32)
    return x + 0.5 * acc


def _ffn_inproj(x_ref, g1_ref, wgu_ref, wd_ref, gm_ref, wmix_ref, x1_ref, zb_ref):
    x1 = _swiglu_residual(x_ref[...], g1_ref, wgu_ref, wd_ref)
    x1_ref[...] = x1
    u = _rms(x1, gm_ref[...]).astype(BF16)
    for c in range(BCD_COLS // 256):
        sl = slice(c * 256, (c + 1) * 256)
        zb_ref[:, sl] = jnp.dot(u, wmix_ref[:, DELTA_COLS + c * 256:DELTA_COLS + (c + 1) * 256],
                                preferred_element_type=F32)
    return lambda c: jnp.dot(u, wmix_ref[:, c * 256:(c + 1) * 256], preferred_element_type=F32)


def _pre_kernel(x_ref, g1_ref, wgu_ref, wd_ref, gm_ref, wmix_ref, x1_ref, zd_ref, zb_ref):
    zd_block = _ffn_inproj(x_ref, g1_ref, wgu_ref, wd_ref, gm_ref, wmix_ref, x1_ref, zb_ref)
    for c in range(DELTA_COLS // 256):
        zd_ref[:, c * 256:(c + 1) * 256] = zd_block(c)


def _pre_tm_kernel(bsz, lead, x_ref, g1_ref, wgu_ref, wd_ref, gm_ref, wmix_ref, cw_ref, buf_ref,
                   x1_ref, zd_ref, zb_ref, nbuf_ref, stage, carry):
    tm = x_ref.shape[0]
    lb = (DN_K - 1) * bsz
    step = pl.program_id(0)

    @pl.when(step < lead)
    def _():
        zd_ref[...] = jnp.zeros(zd_ref.shape, F32)

    @pl.when(step == lead)
    def _():
        carry[...] = buf_ref[...]

    @pl.when(step >= lead)
    def _():
        zd_block = _ffn_inproj(x_ref, g1_ref, wgu_ref, wd_ref, gm_ref, wmix_ref, x1_ref, zb_ref)
        for c in range(DELTA_COLS // 256):
            sl = slice(c * 256, (c + 1) * 256)
            zc = zd_block(c)
            if c < QKV_W // 256:
                win = jnp.concatenate([carry[:, sl], zc], axis=0)
                y = cw_ref[0:1, sl] * win[0:tm]
                for k in range(1, DN_K):
                    y = y + cw_ref[k:k + 1, sl] * win[k * bsz:k * bsz + tm]
                carry[:, sl] = zc[tm - lb:, :]
                nbuf_ref[:, sl] = zc[tm - lb:, :]
                zc = _silu(y)
            stage[2 * c] = zc[:, 0:LANES]
            stage[2 * c + 1] = zc[:, LANES:2 * LANES]
        for b in range(bsz):
            for j in range(DELTA_COLS // LANES):
                zd_ref[b, :, j * LANES:(j + 1) * LANES] = stage[j, pl.ds(b, tm // bsz, stride=bsz), :]


def _pre_call(x, lw, tm, bsz=0, conv_buf=None, lead=0):
    n = x.shape[0]
    tile = lambda i: jnp.maximum(i - lead, 0)
    row = lambda w: pl.BlockSpec((tm, w), lambda i: (tile(i), 0))
    in_specs = [row(D_MODEL), _const_spec((1, D_MODEL)), _const_spec((D_MODEL, 2 * D_FF)),
                _const_spec((D_FF, D_MODEL)), _const_spec((1, D_MODEL)), _const_spec((D_MODEL, MIX_COLS))]
    args = [x, lw["ffn1_norm"], lw["ffn1_wgu"], lw["ffn1_wd"], lw["mix_norm"], lw["w_mix"]]
    out_specs = [row(D_MODEL), row(DELTA_COLS), row(BCD_COLS)]
    out_shape = [jax.ShapeDtypeStruct((n, D_MODEL), F32), jax.ShapeDtypeStruct((n, DELTA_COLS), F32),
                 jax.ShapeDtypeStruct((n, BCD_COLS), F32)]
    body, scratch = _pre_kernel, []
    if bsz:
        lb = (DN_K - 1) * bsz
        steps = tm // bsz
        body = functools.partial(_pre_tm_kernel, bsz, lead)
        in_specs += [_const_spec((DN_K, QKV_W)), _const_spec((lb, QKV_W))]
        args += [lw["dn_conv_w"], conv_buf]
        out_specs[1] = pl.BlockSpec((bsz, steps, DELTA_COLS), lambda i: (0, i, 0))
        out_shape[1] = jax.ShapeDtypeStruct((bsz, lead * steps + n // bsz, DELTA_COLS), F32)
        out_specs.append(pl.BlockSpec((lb, QKV_W), lambda i: (0, 0)))
        out_shape.append(jax.ShapeDtypeStruct((lb, QKV_W), F32))
        scratch = [pltpu.VMEM((DELTA_COLS // LANES, tm, LANES), F32), pltpu.VMEM((lb, QKV_W), F32)]
    return pl.pallas_call(
        body, grid=(lead + n // tm,), in_specs=in_specs, out_specs=out_specs, out_shape=out_shape,
        scratch_shapes=scratch,
        compiler_params=pltpu.CompilerParams(dimension_semantics=("arbitrary",), vmem_limit_bytes=VMEM_LIMIT),
        name="pre",
    )(*args)


def _post_kernel(final, bsz, x1_ref, oa_ref, obcd_ref, gm_ref, wgate_ref, wbr_ref, wout_ref,
                 g2_ref, wgu_ref, wd_ref, gf_ref, out_ref, *scratch):
    x1 = x1_ref[...]
    if bsz:
        for b in range(bsz):
            for j in range(256 // LANES):
                scratch[0][j, pl.ds(b, x1.shape[0] // bsz, stride=bsz), :] = oa_ref[b, :, j * LANES:(j + 1) * LANES]
        oa = jnp.concatenate([scratch[0][0], scratch[0][1]], axis=1)
    else:
        oa = oa_ref[...]
    u = _rms(x1, gm_ref[...]).astype(BF16)
    m = jnp.zeros_like(x1)
    for i in range(4):
        gates = jax.nn.sigmoid(jnp.dot(u, wgate_ref[:, i * D_MODEL:(i + 1) * D_MODEL],
                                       preferred_element_type=F32))
        br = oa if i == 0 else obcd_ref[:, (i - 1) * 256:i * 256]
        m = m + gates * _bdot(br, wbr_ref[i])
    x2 = x1 + _bdot(m, wout_ref[...])
    x3 = _swiglu_residual(x2, g2_ref, wgu_ref, wd_ref)
    if final:
        x3 = _rms(x3, gf_ref[...])
    out_ref[...] = x3


def _post_call(x1, oa, obcd, lw, final_norm, final, tm, bsz=0, lead=0):
    n = x1.shape[0]
    row = lambda w: pl.BlockSpec((tm, w), lambda i: (i, 0))
    oa_spec = pl.BlockSpec((bsz, tm // bsz, 256), lambda i: (0, i + lead, 0)) if bsz else row(256)
    return pl.pallas_call(
        functools.partial(_post_kernel, final, bsz),
        grid=(n // tm,),
        scratch_shapes=[pltpu.VMEM((256 // LANES, tm, LANES), F32)] if bsz else [],
        in_specs=[row(D_MODEL), oa_spec, row(768), _const_spec((1, D_MODEL)),
                  _const_spec((D_MODEL, 4 * D_MODEL)), _const_spec((4, 256, D_MODEL)),
                  _const_spec((D_MODEL, D_MODEL)), _const_spec((1, D_MODEL)),
                  _const_spec((D_MODEL, 2 * D_FF)), _const_spec((D_FF, D_MODEL)), _const_spec((1, D_MODEL))],
        out_specs=row(D_MODEL),
        out_shape=jax.ShapeDtypeStruct((n, D_MODEL), F32),
        compiler_params=pltpu.CompilerParams(dimension_semantics=("arbitrary",), vmem_limit_bytes=VMEM_LIMIT),
        name="post",
    )(x1, oa, obcd, lw["mix_norm"], lw["w_gate"], lw["w_branch"], lw["w_out"],
      lw["ffn2_norm"], lw["ffn2_wgu"], lw["ffn2_wd"], final_norm)


def _split16(x):
    hi = x.astype(BF16)
    if SOLVE_PASSES == 1:
        return (hi,)
    return hi, (x - hi.astype(F32)).astype(BF16)


def _bmm(a, b, contract=(2, 1)):
    dims = (((contract[0],), (contract[1],)), ((0,), (0,)))
    return lax.dot_general(a, b, dims, preferred_element_type=F32)


def _dot3(a, b):
    out = _bmm(a[0], b[0])
    if SOLVE_PASSES >= 2:
        out = out + _bmm(a[1], b[0])
    if SOLVE_PASSES >= 3:
        out = out + _bmm(a[0], b[1])
    return out


def _chunk_masks(seq_len):
    ri = lax.broadcasted_iota(jnp.int32, (CHUNK, CHUNK), 0)
    ci = lax.broadcasted_iota(jnp.int32, (CHUNK, CHUNK), 1)
    causal = ri >= ci
    strict = ri > ci
    if seq_len < CHUNK:
        same = (ri // seq_len) == (ci // seq_len)
        causal = causal & same
        strict = strict & same
    return dict(causal_f=causal.astype(F32), strict_f=strict.astype(F32), eye_f=(ri == ci).astype(F32),
                blk_f=((ri // SOLVE_BLOCK) == (ci // SOLVE_BLOCK)).astype(F32), rowseq=ri // seq_len,
                rowpos=lax.broadcasted_iota(jnp.int32, (CHUNK, 256), 0) % seq_len)


def _wy_solve(low, rhs, mk, seq_len):
    eye_f = mk["eye_f"]
    if seq_len > SOLVE_BLOCK:
        nd = low * mk["blk_f"]
        off = low - nd
        blk = SOLVE_BLOCK
    else:
        nd, off, blk = low, None, seq_len
    p = eye_f - nd
    ns = _split16(nd)
    for _ in range(blk.bit_length() - 2):
        ns = _split16(_dot3(ns, ns))
        p = p + _dot3(_split16(p), ns)
    ps = _split16(p)
    y = _dot3(ps, _split16(rhs))
    if off is None:
        return y
    assert seq_len // SOLVE_BLOCK == 4
    ms = _split16(_dot3(ps, _split16(off)))
    y2 = y + _dot3(_split16(_dot3(ms, ms)), _split16(y))
    return y2 - _dot3(ms, _split16(y2))


def _delta_prepare(chunks, mk, seq_len, scr, slots):
    u_s, w_s, qe_s, kd_s, a_s, egl_s = scr
    qs, ks, vs, bs, gcs = [], [], [], [], []
    for q_in, k_in, v_in, beta, gl in chunks:
        gc = gl
        shift = 1
        while shift < seq_len:
            gc = gc + jnp.where(mk["rowpos"] >= shift, pltpu.roll(gc, shift, axis=0), 0.0)
            shift *= 2
        for h in range(HEADS):
            sl = slice(h * DK, (h + 1) * DK)
            qs.append(q_in[:, sl])
            ks.append(k_in[:, sl])
            vs.append(v_in[:, sl])
            bs.append(beta[:, sl])
            gcs.append(gc[:, sl])
    q, k, v, b, gc = [jnp.stack(a, axis=0) for a in (qs, ks, vs, bs, gcs)]
    q = q * lax.rsqrt(jnp.sum(q * q, axis=-1, keepdims=True) + EPS) * (DK ** -0.5)
    k = k * lax.rsqrt(jnp.sum(k * k, axis=-1, keepdims=True) + EPS)
    gct = jnp.sum(gc * mk["eye_f"], axis=1, keepdims=True)
    dec = jnp.exp(jnp.minimum(gc - gct, 0.0))
    eg = jnp.exp(gc)
    kb = k * b
    k16 = k.astype(BF16)
    kk = _bmm(kb.astype(BF16), k16, (2, 2))
    qk = _bmm(q.astype(BF16), k16, (2, 2))
    low = kk * (dec * mk["strict_f"])
    x = _wy_solve(low, jnp.concatenate([v * b, kb * eg], axis=2), mk, seq_len)
    if seq_len == CHUNK:
        glast = gc[:, CHUNK - 1:CHUNK, :]
    else:
        glast = jnp.concatenate(
            [jnp.broadcast_to(gc[:, (j + 1) * seq_len - 1:(j + 1) * seq_len, :], (gc.shape[0], seq_len, DK))
             for j in range(CHUNK // seq_len)], axis=1)
    w16 = x[:, :, DK:2 * DK].astype(BF16)
    qe16 = (q * eg).astype(BF16)
    kd16 = (k * jnp.exp(glast - gc)).astype(BF16)
    a16 = (qk * (dec * mk["causal_f"])).astype(BF16)
    egl = jnp.exp(glast)
    for c, (lead, slot) in enumerate(slots):
        dst = (pl.ds(lead, HEADS), pl.ds(slot * CHUNK, CHUNK), slice(None))
        ps = slice(c * HEADS, (c + 1) * HEADS)
        u_s[dst] = x[ps, :, 0:DK]
        w_s[dst] = w16[ps]
        qe_s[dst] = qe16[ps]
        kd_s[dst] = kd16[ps]
        a_s[dst] = a16[ps]
        egl_s[dst] = jnp.broadcast_to(egl[ps], (HEADS, CHUNK, DK))


def _gated_norm(o, dz, ng):
    return o * lax.rsqrt(jnp.mean(o * o, axis=-1, keepdims=True) + EPS) * ng * _silu(dz)


def _delta_scratch(lead, nslots):
    n = nslots * CHUNK
    return ([pltpu.VMEM((lead, n, DK), F32)] + [pltpu.VMEM((lead, n, DK), BF16)] * 4
            + [pltpu.VMEM((lead, n, DK), F32)])


def _delta_kernel(bsz, nchunk, q_ref, k_ref, v_ref, dz_ref, be_ref, ae_ref, s0_ref, alog_ref, dtb_ref, ng_ref,
                  o_ref, s_ref, *scr):
    @pl.when(pl.program_id(0) == 0)
    def _():
        s_ref[...] = s0_ref[...]

    u_s, w_s, qe_s, kd_s, a_s, egl_s = scr
    mk = _chunk_masks(CHUNK)
    neg_a = -jnp.exp(alog_ref[...])
    dtb = dtb_ref[...]
    ng = ng_ref[...]

    def prepare(c, carry):
        rows = pl.ds(pl.multiple_of(c * CHUNK, CHUNK), CHUNK)
        chunks = []
        for b in range(bsz):
            beta = jax.nn.sigmoid(be_ref[b, rows, :])
            gl = neg_a * _softplus(ae_ref[b, rows, :] + dtb)
            chunks.append((q_ref[b, rows, :], k_ref[b, rows, :], v_ref[b, rows, :], beta, gl))
        _delta_prepare(chunks, mk, CHUNK, scr, [(b * HEADS, c) for b in range(bsz)])
        return carry
    lax.fori_loop(0, nchunk, prepare, 0)

    def step(c, carry):
        rows = pl.ds(pl.multiple_of(c * CHUNK, CHUNK), CHUNK)
        s = s_ref[...]
        s16 = s.astype(BF16)
        vnew = u_s[:, rows, :] - _bmm(w_s[:, rows, :], s16)
        vn16 = vnew.astype(BF16)
        o = _bmm(qe_s[:, rows, :], s16) + _bmm(a_s[:, rows, :], vn16)
        s_ref[...] = (s * egl_s[:, pl.ds(pl.multiple_of(c * CHUNK, CHUNK), 8), :][:, 0:1, :]
                      + _bmm(kd_s[:, rows, :], vn16, (1, 1)))
        for b in range(bsz):
            for h in range(HEADS):
                sl = slice(h * DK, (h + 1) * DK)
                o_ref[b, rows, sl] = _gated_norm(o[b * HEADS + h], dz_ref[b, rows, sl], ng)
        return carry
    lax.fori_loop(0, nchunk, step, 0)


def _delta_call(zd, s0, lw, nchunk=DELTA_BLOCK):
    bsz, t_len, _ = zd.shape
    tb = nchunk * CHUNK
    assert t_len % tb == 0
    zspec = lambda blk: pl.BlockSpec((bsz, tb, 256), lambda i: (0, i, blk))
    sspec = pl.BlockSpec((bsz * HEADS, DK, DK), lambda i: (0, 0, 0))
    o, s = pl.pallas_call(
        functools.partial(_delta_kernel, bsz, nchunk),
        grid=(t_len // tb,),
        in_specs=[zspec(0), zspec(1), zspec(2), zspec(3), zspec(4), zspec(5), sspec,
                  _const_spec((1, 256)), _const_spec((1, 256)), _const_spec((1, DK))],
        out_specs=[pl.BlockSpec((bsz, tb, 256), lambda i: (0, i, 0)), sspec],
        out_shape=[jax.ShapeDtypeStruct((bsz, t_len, 256), F32),
                   jax.ShapeDtypeStruct((bsz * HEADS, DK, DK), F32)],
        scratch_shapes=_delta_scratch(bsz * HEADS, nchunk),
        compiler_params=pltpu.CompilerParams(dimension_semantics=("arbitrary",), vmem_limit_bytes=VMEM_LIMIT),
        name="delta",
    )(zd, zd, zd, zd, zd, zd, s0.reshape(bsz * HEADS, DK, DK), lw["dn_a_log"], lw["dn_dt_bias"], lw["dn_norm"])
    return o, s.reshape(bsz, HEADS, DK, DK)


def _delta_short_kernel(seq_len, q_ref, k_ref, v_ref, dz_ref, be_ref, ae_ref, buf_ref, s0_ref, cw_ref,
                        alog_ref, dtb_ref, ng_ref, o_ref, nbuf_ref, sfin_ref, xs_ref, *scr):
    nseq = CHUNK // seq_len
    span = seq_len + 8
    mk = _chunk_masks(seq_len)
    cw = cw_ref[...]
    xs_ref[...] = jnp.zeros(xs_ref.shape, F32)
    for j in range(nseq):
        xs_ref[j * span + 5:j * span + 8, :] = buf_ref[j]
        rows = slice(j * seq_len, (j + 1) * seq_len)
        xs_ref[j * span + 8:(j + 1) * span, 0:256] = q_ref[rows, :]
        xs_ref[j * span + 8:(j + 1) * span, 256:512] = k_ref[rows, :]
        xs_ref[j * span + 8:(j + 1) * span, 512:768] = v_ref[rows, :]
    ys = []
    for j in range(nseq):
        win = xs_ref[j * span:(j + 1) * span, :]
        ys.append(cw[3:4] * win[8:span] + cw[2:3] * win[7:span - 1]
                  + cw[1:2] * win[6:span - 2] + cw[0:1] * win[5:span - 3])
        nbuf_ref[j] = win[span - 3:span, :]
    qkv = _silu(jnp.concatenate(ys, axis=0))
    beta = jax.nn.sigmoid(be_ref[...])
    gl = -jnp.exp(alog_ref[...]) * _softplus(ae_ref[...] + dtb_ref[...])
    _delta_prepare([(qkv[:, 0:256], qkv[:, 256:512], qkv[:, 512:768], beta, gl)], mk, seq_len, scr,
                   [(0, 0)])

    u_s, w_s, qe_s, kd_s, a_s, egl_s = scr
    w = w_s[...]
    qe = qe_s[...]
    kd = kd_s[...].astype(F32)
    u = u_s[...]
    ws, qs = [], []
    for j in range(nseq):
        s16 = s0_ref[j].astype(BF16)
        ws.append(_bmm(w, s16))
        qs.append(_bmm(qe, s16))
    vnew, o = u, jnp.zeros_like(u)
    for j in range(nseq):
        mine = mk["rowseq"] == j
        vnew = jnp.where(mine, u - ws[j], vnew)
        o = jnp.where(mine, qs[j], o)
    vn16 = vnew.astype(BF16)
    o = o + _bmm(a_s[...], vn16)
    for j in range(nseq):
        kdj = jnp.where(mk["rowseq"] == j, kd, 0.0).astype(BF16)
        sfin_ref[j] = (s0_ref[j] * egl_s[:, j * seq_len:(j + 1) * seq_len, :][:, 0:1, :]
                       + _bmm(kdj, vn16, (1, 1)))
    ng = ng_ref[...]
    for h in range(HEADS):
        sl = slice(h * DK, (h + 1) * DK)
        o_ref[:, sl] = _gated_norm(o[h], dz_ref[:, sl], ng)


def _delta_short_call(z, seq_len, buf, s0, lw):
    n = z.shape[0]
    nseq = CHUNK // seq_len
    zspec = lambda blk: pl.BlockSpec((CHUNK, 256), lambda g: (g, blk))
    bspec = pl.BlockSpec((nseq, DN_K - 1, QKV_W), lambda g: (g, 0, 0))
    sspec = pl.BlockSpec((nseq, HEADS, DK, DK), lambda g: (g, 0, 0, 0))
    return pl.pallas_call(
        functools.partial(_delta_short_kernel, seq_len),
        grid=(n // CHUNK,),
        in_specs=[zspec(0), zspec(1), zspec(2), zspec(3), zspec(4), zspec(5), bspec, sspec, _const_spec((DN_K, QKV_W)), _const_spec((1, 256)), _const_spec((1, 256)),
                  _const_spec((1, DK))],
        out_specs=[pl.BlockSpec((CHUNK, 256), lambda g: (g, 0)), bspec, sspec],
        out_shape=[jax.ShapeDtypeStruct((n, 256), F32), jax.ShapeDtypeStruct(buf.shape, F32),
                   jax.ShapeDtypeStruct(s0.shape, F32)],
        scratch_shapes=[pltpu.VMEM((nseq * (seq_len + 8), QKV_W), F32)] + _delta_scratch(HEADS, 1),
        compiler_params=pltpu.CompilerParams(dimension_semantics=("arbitrary",), vmem_limit_bytes=VMEM_LIMIT),
        name="delta_short",
    )(z, z, z, z, z, z, buf, s0, lw["dn_conv_w"], lw["dn_a_log"], lw["dn_dt_bias"], lw["dn_norm"])


def _scan_time_major(t_len, bsz, state_refs, step):
    def run_group(goff):
        hs = tuple(r[pl.ds(goff, 8), :] for r in state_refs)
        if t_len <= 8:
            for t in range(t_len):
                hs = step(hs, t * bsz + goff)
        else:
            def body(t, hs):
                return step(hs, pl.multiple_of(t * bsz + goff, 8))
            hs = lax.fori_loop(0, t_len, body, hs, unroll=3)
        for r, h in zip(state_refs, hs):
            r[pl.ds(goff, 8), :] = h

    if bsz == 8:
        run_group(0)
    else:
        def gbody(g, carry):
            run_group(pl.multiple_of(g * 8, 8))
            return carry
        lax.fori_loop(0, bsz // 8, gbody, 0)


def _bcd_kernel(t_len, bsz, nsteps,
                lru_ref, cv_ref, s5_ref, s5re0, s5im0, lru0, lbuf0, cbuf0,
                lam_re_ref, lam_im_ref, lstep_ref, wb_ref, wcre_ref, wcim_ref, dskip_ref, wglu_ref, bglu_ref,
                lcw_ref, lcb_ref, wa_ref, ba_ref, wx_ref, bx_ref, llam_ref,
                ccw_ref, ccb_ref, lng_ref, lnb_ref,
                o_ref, s5re_o, s5im_o, lru_o, lbuf_o, cbuf_o,
                xr, xi, lxs, cxs, a_s, b_s):
    rows = t_len * bsz
    lb = (LRU_K - 1) * bsz
    cb = (CV_K - 1) * bsz

    @pl.when(pl.program_id(0) == 0)
    def _():
        s5re_o[...] = s5re0[...]
        s5im_o[...] = s5im0[...]
        lru_o[...] = lru0[...]
        lxs[0:lb, :] = lbuf0[...]
        cxs[0:cb, :] = cbuf0[...]

    lxs[lb:lb + rows, :] = lru_ref[:, 0:256]
    xf = lcb_ref[...] + lcw_ref[0:1, :] * lxs[0:rows, :]
    for k in range(1, LRU_K):
        xf = xf + lcw_ref[k:k + 1, :] * lxs[k * bsz:k * bsz + rows, :]
    r = jax.nn.sigmoid(_bdot(xf, wa_ref[...]) + ba_ref[...])
    i = jax.nn.sigmoid(_bdot(xf, wx_ref[...]) + bx_ref[...])
    log_a = (-LRU_C) * r * _softplus(-llam_ref[...])
    a_s[...] = jnp.exp(log_a)
    b_s[...] = jnp.sqrt(1.0 - jnp.exp(2.0 * log_a)) * (i * xf)

    def lru_step(hs, row):
        h = a_s[pl.ds(row, 8), :] * hs[0] + b_s[pl.ds(row, 8), :]
        b_s[pl.ds(row, 8), :] = h
        return (h,)
    _scan_time_major(t_len, bsz, (lru_o,), lru_step)
    o_ref[:, 256:512] = b_s[...] * jax.nn.gelu(lru_ref[:, 256:512])
    lbuf_o[...] = lxs[rows:rows + lb, :]
    if nsteps > 1:
        lxs[0:lb, :] = lxs[rows:rows + lb, :]

    dt = jnp.exp(lstep_ref[...])
    lam_re = lam_re_ref[...]
    lam_im = lam_im_ref[...]
    mag = jnp.exp(lam_re * dt)
    lb_re = mag * jnp.cos(lam_im * dt)
    lb_im = mag * jnp.sin(lam_im * dt)
    den = lam_re * lam_re + lam_im * lam_im
    cf_re = ((lb_re - 1.0) * lam_re + lb_im * lam_im) / den
    cf_im = (lb_im * lam_re - (lb_re - 1.0) * lam_im) / den
    u16 = s5_ref[...].astype(BF16)
    bu_re = jnp.dot(u16, wb_ref[:, 0:S5_STATES], preferred_element_type=F32)
    bu_im = jnp.dot(u16, wb_ref[:, S5_STATES:2 * S5_STATES], preferred_element_type=F32)
    xr[...] = cf_re * bu_re - cf_im * bu_im
    xi[...] = cf_re * bu_im + cf_im * bu_re
    lbr = jnp.broadcast_to(lb_re, (8, S5_STATES))
    lbi = jnp.broadcast_to(lb_im, (8, S5_STATES))

    def s5_step(hs, row):
        hr, hi = hs
        nr = lbr * hr - lbi * hi + xr[pl.ds(row, 8), :]
        ni = lbr * hi + lbi * hr + xi[pl.ds(row, 8), :]
        xr[pl.ds(row, 8), :] = nr
        xi[pl.ds(row, 8), :] = ni
        return (nr, ni)
    _scan_time_major(t_len, bsz, (s5re_o, s5im_o), s5_step)
    y = (_bdot(xr[...], wcre_ref[...]) - _bdot(xi[...], wcim_ref[...])
         + dskip_ref[...] * s5_ref[...])
    y = jax.nn.gelu(y)
    glu = _bdot(y, wglu_ref[...]) + bglu_ref[...]
    o_ref[:, 0:256] = glu[:, 0:256] * jax.nn.sigmoid(glu[:, 256:512])

    cxs[cb:cb + rows, :] = cv_ref[:, 0:256] * jax.nn.sigmoid(cv_ref[:, 256:512])
    yc = ccb_ref[...] + ccw_ref[0:1, :] * cxs[0:rows, :]
    for k in range(1, CV_K):
        yc = yc + ccw_ref[k:k + 1, :] * cxs[k * bsz:k * bsz + rows, :]
    mu = jnp.mean(yc, axis=-1, keepdims=True)
    ycc = yc - mu
    yn = ycc * lax.rsqrt(jnp.mean(ycc * ycc, axis=-1, keepdims=True) + EPS) * lng_ref[...] + lnb_ref[...]
    o_ref[:, 512:768] = _silu(yn)
    cbuf_o[...] = cxs[rows:rows + cb, :]
    if nsteps > 1:
        cxs[0:cb, :] = cxs[rows:rows + cb, :]


def _bcd_call(zmix, states, lw, t_len, bsz, tb):
    col_blocks = BCD_COL_BLOCKS
    n = zmix.shape[0]
    nsteps = t_len // tb
    rows = tb * bsz
    lb = (LRU_K - 1) * bsz
    cb = (CV_K - 1) * bsz
    assert nsteps == 1 or rows >= cb
    zspec = lambda w, blk: pl.BlockSpec((rows, w), lambda i: (i, blk))
    state_shapes = [(bsz, S5_STATES), (bsz, S5_STATES), (bsz, 256), (lb, 256), (cb, 256)]
    params = [lw["s5_lam_re"], lw["s5_lam_im"], lw["s5_log_step"], lw["s5_wb"], lw["s5_wcre"], lw["s5_wcim"],
              lw["s5_d"], lw["s5_w_glu"], lw["s5_b_glu"],
              lw["lru_conv_w"], lw["lru_conv_b"], lw["lru_wa"], lw["lru_b_a"], lw["lru_wx"], lw["lru_b_x"],
              lw["lru_lam"], lw["cv_conv_w"], lw["cv_conv_b"], lw["cv_ln_g"], lw["cv_ln_b"]]
    return pl.pallas_call(
        functools.partial(_bcd_kernel, tb, bsz, nsteps),
        grid=(nsteps,),
        in_specs=([zspec(512, col_blocks[0]), zspec(512, col_blocks[1]), zspec(256, col_blocks[2])]
                  + [_const_spec(s) for s in state_shapes]
                  + [_const_spec(p.shape) for p in params]),
        out_specs=[pl.BlockSpec((rows, 768), lambda i: (i, 0))]
                  + [pl.BlockSpec(s, lambda i: (0, 0)) for s in state_shapes],
        out_shape=[jax.ShapeDtypeStruct((n, 768), F32)]
                  + [jax.ShapeDtypeStruct(s, F32) for s in state_shapes],
        scratch_shapes=[pltpu.VMEM((rows, S5_STATES), F32), pltpu.VMEM((rows, S5_STATES), F32),
                        pltpu.VMEM((lb + rows, 256), F32), pltpu.VMEM((cb + rows, 256), F32),
                        pltpu.VMEM((rows, 256), F32), pltpu.VMEM((rows, 256), F32)],
        compiler_params=pltpu.CompilerParams(dimension_semantics=("arbitrary",), vmem_limit_bytes=VMEM_LIMIT),
        name="bcd",
    )(zmix, zmix, zmix, *states, *params)


def _block_diag(m):
    g, r, c = m.shape
    return (jnp.eye(g, dtype=m.dtype)[:, None, :, None] * m[:, :, None, :]).reshape(g * r, g * c)


def _layer_weights(l, p):
    row = lambda v: v.reshape(1, -1).astype(F32)
    w_in = p["w_in"][l]
    w_mix = jnp.concatenate(
        [w_in[:, 0:1024], jnp.repeat(w_in[:, 1024:1028], DK, axis=1), jnp.repeat(w_in[:, 1028:1032], DK, axis=1),
         w_in[:, 1288:1800], w_in[:, 1800:2312], w_in[:, 1032:1288]], axis=1)
    lw = {}
    lw["ffn1_wgu"], lw["ffn1_wd"] = p["ffn1_w_gu"][l].astype(BF16), p["ffn1_w_down"][l].astype(BF16)
    lw["ffn2_wgu"], lw["ffn2_wd"] = p["ffn2_w_gu"][l].astype(BF16), p["ffn2_w_down"][l].astype(BF16)
    lw["ffn1_norm"] = row(p["ffn1_norm"][l])
    lw["ffn2_norm"] = row(p["ffn2_norm"][l])
    lw["mix_norm"] = row(p["mix_norm"][l])
    lw["w_mix"] = w_mix.astype(BF16)
    lw["w_gate"] = w_in[:, 2312:].astype(BF16)
    lw["w_branch"] = p["w_branch"][l].astype(BF16)
    lw["w_out"] = p["w_out"][l].astype(BF16)
    lw["dn_conv_w"] = p["dn_conv_w"][l]
    lw["dn_a_log"] = row(jnp.repeat(p["dn_a_log"][l], DK))
    lw["dn_dt_bias"] = row(jnp.repeat(p["dn_dt_bias"][l], DK))
    lw["dn_norm"] = row(p["dn_norm"][l])
    lw["s5_lam_re"] = row(p["s5_lam_re"][l])
    lw["s5_lam_im"] = row(p["s5_lam_im"][l])
    lw["s5_log_step"] = row(jnp.repeat(p["s5_log_step"][l], 64))
    bdt = lambda w: _block_diag(jnp.swapaxes(w, 1, 2)).astype(BF16)
    lw["s5_wb"] = jnp.concatenate([bdt(p["s5_b_re"][l]), bdt(p["s5_b_im"][l])], axis=1)
    lw["s5_wcre"] = bdt(p["s5_c_re"][l])
    lw["s5_wcim"] = bdt(p["s5_c_im"][l])
    lw["s5_d"] = row(p["s5_d"][l])
    lw["s5_w_glu"] = p["s5_w_glu"][l].astype(BF16)
    lw["s5_b_glu"] = row(p["s5_b_glu"][l])
    lw["lru_conv_w"] = p["lru_conv_w"][l]
    lw["lru_conv_b"] = row(p["lru_conv_b"][l])
    lw["lru_wa"] = _block_diag(p["lru_w_a"][l]).astype(BF16)
    lw["lru_wx"] = _block_diag(p["lru_w_x"][l]).astype(BF16)
    lw["lru_b_a"] = row(p["lru_b_a"][l])
    lw["lru_b_x"] = row(p["lru_b_x"][l])
    lw["lru_lam"] = row(p["lru_lam"][l])
    lw["cv_conv_w"] = p["cv_conv_w"][l]
    lw["cv_conv_b"] = row(p["cv_conv_b"][l])
    lw["cv_ln_g"] = row(p["cv_ln_g"][l])
    lw["cv_ln_b"] = row(p["cv_ln_b"][l])
    return lw


def _to_time_major(a):
    bsz, k, c = a.shape
    return jnp.transpose(a, (1, 0, 2)).reshape(k * bsz, c)


def _from_time_major(a, bsz):
    k = a.shape[0] // bsz
    return jnp.transpose(a.reshape(k, bsz, a.shape[1]), (1, 0, 2))


def _bcd_states(st, bsz):
    _, _, s_re, s_im, s_lru, s_lruc, s_cv = st
    return (s_re.reshape(bsz, S5_STATES), s_im.reshape(bsz, S5_STATES), s_lru,
            _to_time_major(s_lruc), _to_time_major(s_cv))


def _new_states(n_dn, n_dnc, bcd_new, bsz):
    n_re, n_im, n_lru, n_lruc, n_cv = bcd_new
    return (n_dn, n_dnc, n_re.reshape(bsz, 16, 64), n_im.reshape(bsz, 16, 64), n_lru,
            _from_time_major(n_lruc, bsz), _from_time_major(n_cv, bsz))


def _layer_long(x, st, lw, final_norm, final, t_len, bsz, tm, tb):
    pad = (-t_len) % (DELTA_BLOCK * CHUNK)
    lead, rem = divmod(pad * bsz, tm)
    assert rem == 0
    x1, zd, zb, n_dnc = _pre_call(x, lw, tm, bsz, _to_time_major(st[1]), lead)
    n_dnc = _from_time_major(n_dnc, bsz)
    oa, n_dn = _delta_call(zd, st[0], lw)
    obcd, *bcd_new = _bcd_call(zb, _bcd_states(st, bsz), lw, t_len, bsz, tb)
    x3 = _post_call(x1, oa, obcd, lw, final_norm, final, tm, bsz, lead)
    return x3, _new_states(n_dn, n_dnc, bcd_new, bsz)


def _layer_short(x, st, lw, final_norm, final, t_len, bsz, tm):
    x1, zd, zb = _pre_call(x, lw, tm)
    oa, n_dnc, n_dn = _delta_short_call(zd, t_len, st[1], st[0], lw)
    z_tm = _to_time_major(zb.reshape(bsz, t_len, BCD_COLS))
    obcd_tm, *bcd_new = _bcd_call(z_tm, _bcd_states(st, bsz), lw, t_len, bsz, t_len)
    obcd = _from_time_major(obcd_tm, bsz).reshape(bsz * t_len, 768)
    x3 = _post_call(x1, oa, obcd, lw, final_norm, final, tm)
    return x3, _new_states(n_dn, n_dnc, bcd_new, bsz)


def _zero_state(bsz):
    return (jnp.zeros((bsz, HEADS, DK, DK), F32), jnp.zeros((bsz, DN_K - 1, QKV_W), F32),
            jnp.zeros((bsz, 16, 64), F32), jnp.zeros((bsz, 16, 64), F32), jnp.zeros((bsz, 256), F32),
            jnp.zeros((bsz, LRU_K - 1, 256), F32), jnp.zeros((bsz, CV_K - 1, 256), F32))


def kernel(x_prompt, x_sample, state_delta, state_delta_conv, state_s5_re, state_s5_im, state_lru, state_lru_conv, state_conv, meta_tokens, ffn1_norm, ffn1_w_gu, ffn1_w_down, mix_norm, w_in, dn_conv_w, dn_a_log, dn_dt_bias, dn_norm, s5_lam_re, s5_lam_im, s5_log_step, s5_b_re, s5_b_im, s5_c_re, s5_c_im, s5_d, s5_w_glu, s5_b_glu, lru_conv_w, lru_conv_b, lru_w_a, lru_b_a, lru_w_x, lru_b_x, lru_lam, cv_conv_w, cv_conv_b, cv_ln_g, cv_ln_b, w_branch, w_out, ffn2_norm, ffn2_w_gu, ffn2_w_down, final_norm):
    p = dict(ffn1_norm=ffn1_norm, ffn1_w_gu=ffn1_w_gu, ffn1_w_down=ffn1_w_down, mix_norm=mix_norm, w_in=w_in,
             dn_conv_w=dn_conv_w, dn_a_log=dn_a_log, dn_dt_bias=dn_dt_bias, dn_norm=dn_norm,
             s5_lam_re=s5_lam_re, s5_lam_im=s5_lam_im, s5_log_step=s5_log_step, s5_b_re=s5_b_re,
             s5_b_im=s5_b_im, s5_c_re=s5_c_re, s5_c_im=s5_c_im, s5_d=s5_d, s5_w_glu=s5_w_glu,
             s5_b_glu=s5_b_glu, lru_conv_w=lru_conv_w, lru_conv_b=lru_conv_b, lru_w_a=lru_w_a,
             lru_b_a=lru_b_a, lru_w_x=lru_w_x, lru_b_x=lru_b_x, lru_lam=lru_lam, cv_conv_w=cv_conv_w,
             cv_conv_b=cv_conv_b, cv_ln_g=cv_ln_g, cv_ln_b=cv_ln_b, w_branch=w_branch, w_out=w_out,
             ffn2_norm=ffn2_norm, ffn2_w_gu=ffn2_w_gu, ffn2_w_down=ffn2_w_down)
    depth = w_in.shape[0]
    bp, seq, _ = x_prompt.shape
    bs, dseq, _ = x_sample.shape
    tp = seq + N_META
    fnorm = final_norm.reshape(1, D_MODEL)

    meta = jnp.broadcast_to(meta_tokens[:, None, :], (N_META, bp, D_MODEL))
    xp = jnp.concatenate([meta, jnp.transpose(x_prompt, (1, 0, 2))], axis=0).reshape(tp * bp, D_MODEL)
    xs = x_sample.reshape(bs * dseq, D_MODEL)

    p_new, s_new = [], []
    for l in range(depth):
        lw = _layer_weights(l, p)
        final = l == depth - 1
        xp, st_p = _layer_long(xp, _zero_state(bp), lw, fnorm, final, tp, bp, tm=384, tb=129)
        st_s = (state_delta[l], state_delta_conv[l], state_s5_re[l], state_s5_im[l], state_lru[l],
                state_lru_conv[l], state_conv[l])
        xs, st_s = _layer_short(xs, st_s, lw, fnorm, final, dseq, bs, tm=512)
        p_new.append(st_p)
        s_new.append(st_s)

    y_prompt = jnp.transpose(xp.reshape(tp, bp, D_MODEL)[N_META:], (1, 0, 2))
    y_sample = xs.reshape(bs, dseq, D_MODEL)
    stack = lambda new, i: jnp.stack([st[i] for st in new], axis=0)
    return (y_prompt, y_sample, *[stack(p_new, i) for i in range(7)], *[stack(s_new, i) for i in range(7)])
```

```python
import functools

import jax
import jax.numpy as jnp
from jax import lax
from jax.experimental import pallas as pl
from jax.experimental.pallas import tpu as pltpu

F32 = jnp.float32
BF16 = jnp.bfloat16

LANES = 128
D_MODEL = 1024
D_FF = 2816
N_META = 16
EPS = 1e-6
HEADS = 4
DK = 64
QKV_W = 768
CHUNK = 64
DELTA_BLOCK = 3
SOLVE_BLOCK = 16
SOLVE_PASSES = 1
S5_STATES = 1024
LRU_C = 8.0
CV_K = 31
LRU_K = 4
DN_K = 4

FF_CHUNK = 256
FF_NCHUNK = D_FF // FF_CHUNK
DELTA_COLS = 1536
BCD_COLS = 1280
BCD_COL_BLOCKS = (0, 1, 4)
MIX_COLS = DELTA_COLS + BCD_COLS

VMEM_LIMIT = 56 * 1024 * 1024


def _const_spec(shape):
    nd = len(shape)
    return pl.BlockSpec(shape, lambda *_: (0,) * nd, pipeline_mode=pl.Buffered(1))


def _bdot(a, b):
    return jnp.dot(a.astype(BF16), b.astype(BF16), preferred_element_type=F32)


def _hdot(a, b):
    return jnp.dot(a, b, precision=lax.Precision.HIGHEST, preferred_element_type=F32)


def _rms(x, g):
    return x * lax.rsqrt(jnp.mean(x * x, axis=-1, keepdims=True) + EPS) * g


def _silu(x):
    return x * jax.nn.sigmoid(x)


def _softplus(x):
    return jnp.maximum(x, 0.0) + jnp.log1p(jnp.exp(-jnp.abs(x)))


def _swiglu_residual(x, g_ref, wgu_ref, wd_ref):
    xn = _rms(x, g_ref[...]).astype(BF16)
    acc = jnp.zeros_like(x)
    for c in range(FF_NCHUNK):
        lo, hi = c * FF_CHUNK, (c + 1) * FF_CHUNK
        gate = jnp.dot(xn, wgu_ref[:, lo:hi], preferred_element_type=F32)
        up = jnp.dot(xn, wgu_ref[:, D_FF + lo:D_FF + hi], preferred_element_type=F32)
        h = (_silu(gate) * up).astype(BF16)
        acc = acc + jnp.dot(h, wd_ref[lo:hi, :], preferred_element_type=F32)
    return x + 0.5 * acc


def _to_time_major_rows(src_ref, stage, bsz):
    steps, ntile = src_ref.shape[1], src_ref.shape[2] // LANES
    for b in range(bsz):
        for j in range(ntile):
            stage[j, pl.ds(b, steps, stride=bsz), :] = src_ref[b, :, j * LANES:(j + 1) * LANES]
    return jnp.concatenate([stage[j] for j in range(ntile)], axis=1)


def _from_time_major_rows(stage, dst_ref, bsz):
    steps, ntile = dst_ref.shape[1], dst_ref.shape[2] // LANES
    for b in range(bsz):
        for j in range(ntile):
            dst_ref[b, :, j * LANES:(j + 1) * LANES] = stage[j, pl.ds(b, steps, stride=bsz), :]


def _ffn_inproj(x, g1_ref, wgu_ref, wd_ref, gm_ref, wmix_ref, x1_ref, zb_ref):
    x1 = _swiglu_residual(x, g1_ref, wgu_ref, wd_ref)
    x1_ref[...] = x1
    u = _rms(x1, gm_ref[...]).astype(BF16)
    for c in range(BCD_COLS // 256):
        sl = slice(c * 256, (c + 1) * 256)
        zb_ref[:, sl] = jnp.dot(u, wmix_ref[:, DELTA_COLS + c * 256:DELTA_COLS + (c + 1) * 256],
                                preferred_element_type=F32)
    return lambda c: jnp.dot(u, wmix_ref[:, c * 256:(c + 1) * 256], preferred_element_type=F32)


def _pre_kernel(x_ref, g1_ref, wgu_ref, wd_ref, gm_ref, wmix_ref, x1_ref, zd_ref, zb_ref):
    zd_block = _ffn_inproj(x_ref[...], g1_ref, wgu_ref, wd_ref, gm_ref, wmix_ref, x1_ref, zb_ref)
    for c in range(DELTA_COLS // 256):
        zd_ref[:, c * 256:(c + 1) * 256] = zd_block(c)


def _pre_tm_kernel(bsz, lead, x_bm, x_ref, g1_ref, wgu_ref, wd_ref, gm_ref, wmix_ref, cw_ref, buf_ref,
                   x1_ref, zd_ref, zb_ref, nbuf_ref, stage, carry, *stage_x):
    tm = x1_ref.shape[0]
    lb = (DN_K - 1) * bsz
    step = pl.program_id(0)

    @pl.when(step < lead)
    def _():
        zd_ref[...] = jnp.zeros(zd_ref.shape, F32)

    @pl.when(step == lead)
    def _():
        carry[...] = buf_ref[...]

    @pl.when(step >= lead)
    def _():
        x = _to_time_major_rows(x_ref, stage_x[0], bsz) if x_bm else x_ref[...]
        zd_block = _ffn_inproj(x, g1_ref, wgu_ref, wd_ref, gm_ref, wmix_ref, x1_ref, zb_ref)
        for c in range(DELTA_COLS // 256):
            sl = slice(c * 256, (c + 1) * 256)
            zc = zd_block(c)
            if c < QKV_W // 256:
                win = jnp.concatenate([carry[:, sl], zc], axis=0)
                y = cw_ref[0:1, sl] * win[0:tm]
                for k in range(1, DN_K):
                    y = y + cw_ref[k:k + 1, sl] * win[k * bsz:k * bsz + tm]
                carry[:, sl] = zc[tm - lb:, :]
                nbuf_ref[:, sl] = zc[tm - lb:, :]
                zc = _silu(y)
            stage[2 * c] = zc[:, 0:LANES]
            stage[2 * c + 1] = zc[:, LANES:2 * LANES]
        _from_time_major_rows(stage, zd_ref, bsz)


def _pre_call(x, lw, tm, bsz=0, conv_buf=None, lead=0):
    x_bm = x.ndim == 3
    n = x.shape[0] * x.shape[1] if x_bm else x.shape[0]
    tile = lambda i: jnp.maximum(i - lead, 0)
    row = lambda w: pl.BlockSpec((tm, w), lambda i: (tile(i), 0))
    x_spec = pl.BlockSpec((bsz, tm // bsz, D_MODEL), lambda i: (0, tile(i), 0)) if x_bm else row(D_MODEL)
    in_specs = [x_spec, _const_spec((1, D_MODEL)), _const_spec((D_MODEL, 2 * D_FF)),
                _const_spec((D_FF, D_MODEL)), _const_spec((1, D_MODEL)), _const_spec((D_MODEL, MIX_COLS))]
    args = [x, lw["ffn1_norm"], lw["ffn1_wgu"], lw["ffn1_wd"], lw["mix_norm"], lw["w_mix"]]
    out_specs = [row(D_MODEL), row(DELTA_COLS), row(BCD_COLS)]
    out_shape = [jax.ShapeDtypeStruct((n, D_MODEL), F32), jax.ShapeDtypeStruct((n, DELTA_COLS), F32),
                 jax.ShapeDtypeStruct((n, BCD_COLS), F32)]
    body, scratch = _pre_kernel, []
    if bsz:
        lb = (DN_K - 1) * bsz
        steps = tm // bsz
        body = functools.partial(_pre_tm_kernel, bsz, lead, x_bm)
        in_specs += [_const_spec((DN_K, QKV_W)), _const_spec((lb, QKV_W))]
        args += [lw["dn_conv_w"], conv_buf]
        out_specs[1] = pl.BlockSpec((bsz, steps, DELTA_COLS), lambda i: (0, i, 0))
        out_shape[1] = jax.ShapeDtypeStruct((bsz, lead * steps + n // bsz, DELTA_COLS), F32)
        out_specs.append(pl.BlockSpec((lb, QKV_W), lambda i: (0, 0)))
        out_shape.append(jax.ShapeDtypeStruct((lb, QKV_W), F32))
        scratch = [pltpu.VMEM((DELTA_COLS // LANES, tm, LANES), F32), pltpu.VMEM((lb, QKV_W), F32)]
        if x_bm:
            scratch.append(pltpu.VMEM((D_MODEL // LANES, tm, LANES), F32))
    return pl.pallas_call(
        body, grid=(lead + n // tm,), in_specs=in_specs, out_specs=out_specs, out_shape=out_shape,
        scratch_shapes=scratch,
        compiler_params=pltpu.CompilerParams(dimension_semantics=("arbitrary",), vmem_limit_bytes=VMEM_LIMIT),
        name="pre",
    )(*args)


def _post_kernel(final, bsz, out_bm, x1_ref, oa_ref, obcd_ref, gm_ref, wgate_ref, wbr_ref, wout_ref,
                 g2_ref, wgu_ref, wd_ref, gf_ref, out_ref, *scratch):
    x1 = x1_ref[...]
    oa = _to_time_major_rows(oa_ref, scratch[0], bsz) if bsz else oa_ref[...]
    u = _rms(x1, gm_ref[...]).astype(BF16)
    m = jnp.zeros_like(x1)
    for i in range(4):
        gates = jax.nn.sigmoid(jnp.dot(u, wgate_ref[:, i * D_MODEL:(i + 1) * D_MODEL],
                                       preferred_element_type=F32))
        br = oa if i == 0 else obcd_ref[:, (i - 1) * 256:i * 256]
        m = m + gates * _bdot(br, wbr_ref[i])
    x2 = x1 + _bdot(m, wout_ref[...])
    x3 = _swiglu_residual(x2, g2_ref, wgu_ref, wd_ref)
    if final:
        x3 = _rms(x3, gf_ref[...])
    if out_bm:
        for j in range(D_MODEL // LANES):
            scratch[1][j] = x3[:, j * LANES:(j + 1) * LANES]
        _from_time_major_rows(scratch[1], out_ref, bsz)
    else:
        out_ref[...] = x3


def _post_call(x1, oa, obcd, lw, final_norm, final, tm, bsz=0, lead=0, out_bm=False):
    n = x1.shape[0]
    row = lambda w: pl.BlockSpec((tm, w), lambda i: (i, 0))
    oa_spec = pl.BlockSpec((bsz, tm // bsz, 256), lambda i: (0, i + lead, 0)) if bsz else row(256)
    scratch = [pltpu.VMEM((256 // LANES, tm, LANES), F32)] if bsz else []
    out_spec, out_shape = row(D_MODEL), jax.ShapeDtypeStruct((n, D_MODEL), F32)
    if out_bm:
        scratch.append(pltpu.VMEM((D_MODEL // LANES, tm, LANES), F32))
        out_spec = pl.BlockSpec((bsz, tm // bsz, D_MODEL), lambda i: (0, i, 0))
        out_shape = jax.ShapeDtypeStruct((bsz, n // bsz, D_MODEL), F32)
    return pl.pallas_call(
        functools.partial(_post_kernel, final, bsz, out_bm),
        grid=(n // tm,),
        scratch_shapes=scratch,
        in_specs=[row(D_MODEL), oa_spec, row(768), _const_spec((1, D_MODEL)),
                  _const_spec((D_MODEL, 4 * D_MODEL)), _const_spec((4, 256, D_MODEL)),
                  _const_spec((D_MODEL, D_MODEL)), _const_spec((1, D_MODEL)),
                  _const_spec((D_MODEL, 2 * D_FF)), _const_spec((D_FF, D_MODEL)), _const_spec((1, D_MODEL))],
        out_specs=out_spec,
        out_shape=out_shape,
        compiler_params=pltpu.CompilerParams(dimension_semantics=("arbitrary",), vmem_limit_bytes=VMEM_LIMIT),
        name="post",
    )(x1, oa, obcd, lw["mix_norm"], lw["w_gate"], lw["w_branch"], lw["w_out"],
      lw["ffn2_norm"], lw["ffn2_wgu"], lw["ffn2_wd"], final_norm)


def _split16(x):
    hi = x.astype(BF16)
    if SOLVE_PASSES == 1:
        return (hi,)
    return hi, (x - hi.astype(F32)).astype(BF16)


def _bmm(a, b, contract=(2, 1)):
    dims = (((contract[0],), (contract[1],)), ((0,), (0,)))
    return lax.dot_general(a, b, dims, preferred_element_type=F32)


def _dot3(a, b):
    out = _bmm(a[0], b[0])
    if SOLVE_PASSES >= 2:
        out = out + _bmm(a[1], b[0])
    if SOLVE_PASSES >= 3:
        out = out + _bmm(a[0], b[1])
    return out


def _chunk_masks(seq_len):
    ri = lax.broadcasted_iota(jnp.int32, (CHUNK, CHUNK), 0)
    ci = lax.broadcasted_iota(jnp.int32, (CHUNK, CHUNK), 1)
    causal = ri >= ci
    strict = ri > ci
    if seq_len < CHUNK:
        same = (ri // seq_len) == (ci // seq_len)
        causal = causal & same
        strict = strict & same
    return dict(causal_f=causal.astype(F32), strict_f=strict.astype(F32), eye_f=(ri == ci).astype(F32),
                blk_f=((ri // SOLVE_BLOCK) == (ci // SOLVE_BLOCK)).astype(F32), rowseq=ri // seq_len,
                rowpos=lax.broadcasted_iota(jnp.int32, (CHUNK, 256), 0) % seq_len)


def _wy_solve(low, rhs, mk, seq_len):
    eye_f = mk["eye_f"]
    if seq_len > SOLVE_BLOCK:
        nd = low * mk["blk_f"]
        off = low - nd
        blk = SOLVE_BLOCK
    else:
        nd, off, blk = low, None, seq_len
    p = eye_f - nd
    ns = _split16(nd)
    for _ in range(blk.bit_length() - 2):
        ns = _split16(_dot3(ns, ns))
        p = p + _dot3(_split16(p), ns)
    ps = _split16(p)
    y = _dot3(ps, _split16(rhs))
    if off is None:
        return y
    assert seq_len // SOLVE_BLOCK == 4
    ms = _split16(_dot3(ps, _split16(off)))
    y2 = y + _dot3(_split16(_dot3(ms, ms)), _split16(y))
    return y2 - _dot3(ms, _split16(y2))


def _delta_prepare(chunks, mk, seq_len, scr, slots):
    u_s, w_s, qe_s, kd_s, a_s, egl_s = scr
    qs, ks, vs, bs, gcs = [], [], [], [], []
    for q_in, k_in, v_in, beta, gl in chunks:
        gc = gl
        shift = 1
        while shift < seq_len:
            gc = gc + jnp.where(mk["rowpos"] >= shift, pltpu.roll(gc, shift, axis=0), 0.0)
            shift *= 2
        for h in range(HEADS):
            sl = slice(h * DK, (h + 1) * DK)
            qs.append(q_in[:, sl])
            ks.append(k_in[:, sl])
            vs.append(v_in[:, sl])
            bs.append(beta[:, sl])
            gcs.append(gc[:, sl])
    q, k, v, b, gc = [jnp.stack(a, axis=0) for a in (qs, ks, vs, bs, gcs)]
    q = q * lax.rsqrt(jnp.sum(q * q, axis=-1, keepdims=True) + EPS) * (DK ** -0.5)
    k = k * lax.rsqrt(jnp.sum(k * k, axis=-1, keepdims=True) + EPS)
    gct = jnp.sum(gc * mk["eye_f"], axis=1, keepdims=True)
    dec = jnp.exp(jnp.minimum(gc - gct, 0.0))
    eg = jnp.exp(gc)
    kb = k * b
    k16 = k.astype(BF16)
    kk = _bmm(kb.astype(BF16), k16, (2, 2))
    qk = _bmm(q.astype(BF16), k16, (2, 2))
    low = kk * (dec * mk["strict_f"])
    x = _wy_solve(low, jnp.concatenate([v * b, kb * eg], axis=2), mk, seq_len)
    if seq_len == CHUNK:
        glast = gc[:, CHUNK - 1:CHUNK, :]
    else:
        glast = jnp.concatenate(
            [jnp.broadcast_to(gc[:, (j + 1) * seq_len - 1:(j + 1) * seq_len, :], (gc.shape[0], seq_len, DK))
             for j in range(CHUNK // seq_len)], axis=1)
    w16 = x[:, :, DK:2 * DK].astype(BF16)
    qe16 = (q * eg).astype(BF16)
    kd16 = (k * jnp.exp(glast - gc)).astype(BF16)
    a16 = (qk * (dec * mk["causal_f"])).astype(BF16)
    egl = jnp.exp(glast)
    for c, (lead, slot) in enumerate(slots):
        dst = (pl.ds(lead, HEADS), pl.ds(slot * CHUNK, CHUNK), slice(None))
        ps = slice(c * HEADS, (c + 1) * HEADS)
        u_s[dst] = x[ps, :, 0:DK]
        w_s[dst] = w16[ps]
        qe_s[dst] = qe16[ps]
        kd_s[dst] = kd16[ps]
        a_s[dst] = a16[ps]
        egl_s[dst] = jnp.broadcast_to(egl[ps], (HEADS, CHUNK, DK))


def _gated_norm(o, dz, ng):
    return o * lax.rsqrt(jnp.mean(o * o, axis=-1, keepdims=True) + EPS) * ng * _silu(dz)


def _delta_scratch(lead, nslots):
    n = nslots * CHUNK
    return ([pltpu.VMEM((lead, n, DK), F32)] + [pltpu.VMEM((lead, n, DK), BF16)] * 4
            + [pltpu.VMEM((lead, n, DK), F32)])


def _delta_kernel(bsz, nchunk, q_ref, k_ref, v_ref, dz_ref, be_ref, ae_ref, s0_ref, alog_ref, dtb_ref, ng_ref,
                  o_ref, s_ref, *scr):
    @pl.when(pl.program_id(0) == 0)
    def _():
        s_ref[...] = s0_ref[...]

    u_s, w_s, qe_s, kd_s, a_s, egl_s = scr
    mk = _chunk_masks(CHUNK)
    neg_a = -jnp.exp(alog_ref[...])
    dtb = dtb_ref[...]
    ng = ng_ref[...]

    def prepare(c, carry):
        rows = pl.ds(pl.multiple_of(c * CHUNK, CHUNK), CHUNK)
        chunks = []
        for b in range(bsz):
            beta = jax.nn.sigmoid(be_ref[b, rows, :])
            gl = neg_a * _softplus(ae_ref[b, rows, :] + dtb)
            chunks.append((q_ref[b, rows, :], k_ref[b, rows, :], v_ref[b, rows, :], beta, gl))
        _delta_prepare(chunks, mk, CHUNK, scr, [(b * HEADS, c) for b in range(bsz)])
        return carry
    lax.fori_loop(0, nchunk, prepare, 0)

    def step(c, carry):
        rows = pl.ds(pl.multiple_of(c * CHUNK, CHUNK), CHUNK)
        s = s_ref[...]
        s16 = s.astype(BF16)
        vnew = u_s[:, rows, :] - _bmm(w_s[:, rows, :], s16)
        vn16 = vnew.astype(BF16)
        o = _bmm(qe_s[:, rows, :], s16) + _bmm(a_s[:, rows, :], vn16)
        s_ref[...] = (s * egl_s[:, pl.ds(pl.multiple_of(c * CHUNK, CHUNK), 8), :][:, 0:1, :]
                      + _bmm(kd_s[:, rows, :], vn16, (1, 1)))
        for b in range(bsz):
            for h in range(HEADS):
                sl = slice(h * DK, (h + 1) * DK)
                o_ref[b, rows, sl] = _gated_norm(o[b * HEADS + h], dz_ref[b, rows, sl], ng)
        return carry
    lax.fori_loop(0, nchunk, step, 0)


def _delta_call(zd, s0, lw, nchunk=DELTA_BLOCK):
    bsz, t_len, _ = zd.shape
    tb = nchunk * CHUNK
    assert t_len % tb == 0
    zspec = lambda blk: pl.BlockSpec((bsz, tb, 256), lambda i: (0, i, blk))
    sspec = pl.BlockSpec((bsz * HEADS, DK, DK), lambda i: (0, 0, 0))
    o, s = pl.pallas_call(
        functools.partial(_delta_kernel, bsz, nchunk),
        grid=(t_len // tb,),
        in_specs=[zspec(0), zspec(1), zspec(2), zspec(3), zspec(4), zspec(5), sspec,
                  _const_spec((1, 256)), _const_spec((1, 256)), _const_spec((1, DK))],
        out_specs=[pl.BlockSpec((bsz, tb, 256), lambda i: (0, i, 0)), sspec],
        out_shape=[jax.ShapeDtypeStruct((bsz, t_len, 256), F32),
                   jax.ShapeDtypeStruct((bsz * HEADS, DK, DK), F32)],
        scratch_shapes=_delta_scratch(bsz * HEADS, nchunk),
        compiler_params=pltpu.CompilerParams(dimension_semantics=("arbitrary",), vmem_limit_bytes=VMEM_LIMIT),
        name="delta",
    )(zd, zd, zd, zd, zd, zd, s0.reshape(bsz * HEADS, DK, DK), lw["dn_a_log"], lw["dn_dt_bias"], lw["dn_norm"])
    return o, s.reshape(bsz, HEADS, DK, DK)


def _delta_short_kernel(seq_len, q_ref, k_ref, v_ref, dz_ref, be_ref, ae_ref, buf_ref, s0_ref, cw_ref,
                        alog_ref, dtb_ref, ng_ref, o_ref, nbuf_ref, sfin_ref, xs_ref, *scr):
    nseq = CHUNK // seq_len
    span = seq_len + 8
    mk = _chunk_masks(seq_len)
    cw = cw_ref[...]
    xs_ref[...] = jnp.zeros(xs_ref.shape, F32)
    for j in range(nseq):
        xs_ref[j * span + 5:j * span + 8, :] = buf_ref[j]
        rows = slice(j * seq_len, (j + 1) * seq_len)
        xs_ref[j * span + 8:(j + 1) * span, 0:256] = q_ref[rows, :]
        xs_ref[j * span + 8:(j + 1) * span, 256:512] = k_ref[rows, :]
        xs_ref[j * span + 8:(j + 1) * span, 512:768] = v_ref[rows, :]
    ys = []
    for j in range(nseq):
        win = xs_ref[j * span:(j + 1) * span, :]
        ys.append(cw[3:4] * win[8:span] + cw[2:3] * win[7:span - 1]
                  + cw[1:2] * win[6:span - 2] + cw[0:1] * win[5:span - 3])
        nbuf_ref[j] = win[span - 3:span, :]
    qkv = _silu(jnp.concatenate(ys, axis=0))
    beta = jax.nn.sigmoid(be_ref[...])
    gl = -jnp.exp(alog_ref[...]) * _softplus(ae_ref[...] + dtb_ref[...])
    _delta_prepare([(qkv[:, 0:256], qkv[:, 256:512], qkv[:, 512:768], beta, gl)], mk, seq_len, scr,
                   [(0, 0)])

    u_s, w_s, qe_s, kd_s, a_s, egl_s = scr
    w = w_s[...]
    qe = qe_s[...]
    kd = kd_s[...].astype(F32)
    u = u_s[...]
    ws, qs = [], []
    for j in range(nseq):
        s16 = s0_ref[j].astype(BF16)
        ws.append(_bmm(w, s16))
        qs.append(_bmm(qe, s16))
    vnew, o = u, jnp.zeros_like(u)
    for j in range(nseq):
        mine = mk["rowseq"] == j
        vnew = jnp.where(mine, u - ws[j], vnew)
        o = jnp.where(mine, qs[j], o)
    vn16 = vnew.astype(BF16)
    o = o + _bmm(a_s[...], vn16)
    for j in range(nseq):
        kdj = jnp.where(mk["rowseq"] == j, kd, 0.0).astype(BF16)
        sfin_ref[j] = (s0_ref[j] * egl_s[:, j * seq_len:(j + 1) * seq_len, :][:, 0:1, :]
                       + _bmm(kdj, vn16, (1, 1)))
    ng = ng_ref[...]
    for h in range(HEADS):
        sl = slice(h * DK, (h + 1) * DK)
        o_ref[:, sl] = _gated_norm(o[h], dz_ref[:, sl], ng)


def _delta_short_call(z, seq_len, buf, s0, lw):
    n = z.shape[0]
    nseq = CHUNK // seq_len
    zspec = lambda blk: pl.BlockSpec((CHUNK, 256), lambda g: (g, blk))
    bspec = pl.BlockSpec((nseq, DN_K - 1, QKV_W), lambda g: (g, 0, 0))
    sspec = pl.BlockSpec((nseq, HEADS, DK, DK), lambda g: (g, 0, 0, 0))
    return pl.pallas_call(
        functools.partial(_delta_short_kernel, seq_len),
        grid=(n // CHUNK,),
        in_specs=[zspec(0), zspec(1), zspec(2), zspec(3), zspec(4), zspec(5), bspec, sspec, _const_spec((DN_K, QKV_W)), _const_spec((1, 256)), _const_spec((1, 256)),
                  _const_spec((1, DK))],
        out_specs=[pl.BlockSpec((CHUNK, 256), lambda g: (g, 0)), bspec, sspec],
        out_shape=[jax.ShapeDtypeStruct((n, 256), F32), jax.ShapeDtypeStruct(buf.shape, F32),
                   jax.ShapeDtypeStruct(s0.shape, F32)],
        scratch_shapes=[pltpu.VMEM((nseq * (seq_len + 8), QKV_W), F32)] + _delta_scratch(HEADS, 1),
        compiler_params=pltpu.CompilerParams(dimension_semantics=("arbitrary",), vmem_limit_bytes=VMEM_LIMIT),
        name="delta_short",
    )(z, z, z, z, z, z, buf, s0, lw["dn_conv_w"], lw["dn_a_log"], lw["dn_dt_bias"], lw["dn_norm"])


def _scan_time_major(t_len, bsz, state_refs, step):
    def run_group(goff):
        hs = tuple(r[pl.ds(goff, 8), :] for r in state_refs)
        if t_len <= 8:
            for t in range(t_len):
                hs = step(hs, t * bsz + goff)
        else:
            def body(t, hs):
                return step(hs, pl.multiple_of(t * bsz + goff, 8))
            hs = lax.fori_loop(0, t_len, body, hs, unroll=3)
        for r, h in zip(state_refs, hs):
            r[pl.ds(goff, 8), :] = h

    if bsz == 8:
        run_group(0)
    else:
        def gbody(g, carry):
            run_group(pl.multiple_of(g * 8, 8))
            return carry
        lax.fori_loop(0, bsz // 8, gbody, 0)


def _bcd_kernel(t_len, bsz, nsteps,
                lru_ref, cv_ref, s5_ref, s5re0, s5im0, lru0, lbuf0, cbuf0,
                lam_re_ref, lam_im_ref, lstep_ref, wb_ref, wcre_ref, wcim_ref, dskip_ref, wglu_ref, bglu_ref,
                lcw_ref, lcb_ref, wa_ref, ba_ref, wx_ref, bx_ref, llam_ref,
                ccw_ref, ccb_ref, lng_ref, lnb_ref,
                o_ref, s5re_o, s5im_o, lru_o, lbuf_o, cbuf_o,
                xr, xi, lxs, cxs, a_s, b_s):
    rows = t_len * bsz
    lb = (LRU_K - 1) * bsz
    cb = (CV_K - 1) * bsz

    @pl.when(pl.program_id(0) == 0)
    def _():
        s5re_o[...] = s5re0[...]
        s5im_o[...] = s5im0[...]
        lru_o[...] = lru0[...]
        lxs[0:lb, :] = lbuf0[...]
        cxs[0:cb, :] = cbuf0[...]

    lxs[lb:lb + rows, :] = lru_ref[:, 0:256]
    xf = lcb_ref[...] + lcw_ref[0:1, :] * lxs[0:rows, :]
    for k in range(1, LRU_K):
        xf = xf + lcw_ref[k:k + 1, :] * lxs[k * bsz:k * bsz + rows, :]
    r = jax.nn.sigmoid(_bdot(xf, wa_ref[...]) + ba_ref[...])
    i = jax.nn.sigmoid(_bdot(xf, wx_ref[...]) + bx_ref[...])
    log_a = (-LRU_C) * r * _softplus(-llam_ref[...])
    a_s[...] = jnp.exp(log_a)
    b_s[...] = jnp.sqrt(1.0 - jnp.exp(2.0 * log_a)) * (i * xf)

    def lru_step(hs, row):
        h = a_s[pl.ds(row, 8), :] * hs[0] + b_s[pl.ds(row, 8), :]
        b_s[pl.ds(row, 8), :] = h
        return (h,)
    _scan_time_major(t_len, bsz, (lru_o,), lru_step)
    o_ref[:, 256:512] = b_s[...] * jax.nn.gelu(lru_ref[:, 256:512])
    lbuf_o[...] = lxs[rows:rows + lb, :]
    if nsteps > 1:
        lxs[0:lb, :] = lxs[rows:rows + lb, :]

    dt = jnp.exp(lstep_ref[...])
    lam_re = lam_re_ref[...]
    lam_im = lam_im_ref[...]
    mag = jnp.exp(lam_re * dt)
    lb_re = mag * jnp.cos(lam_im * dt)
    lb_im = mag * jnp.sin(lam_im * dt)
    den = lam_re * lam_re + lam_im * lam_im
    cf_re = ((lb_re - 1.0) * lam_re + lb_im * lam_im) / den
    cf_im = (lb_im * lam_re - (lb_re - 1.0) * lam_im) / den
    u16 = s5_ref[...].astype(BF16)
    bu_re = jnp.dot(u16, wb_ref[:, 0:S5_STATES], preferred_element_type=F32)
    bu_im = jnp.dot(u16, wb_ref[:, S5_STATES:2 * S5_STATES], preferred_element_type=F32)
    xr[...] = cf_re * bu_re - cf_im * bu_im
    xi[...] = cf_re * bu_im + cf_im * bu_re
    lbr = jnp.broadcast_to(lb_re, (8, S5_STATES))
    lbi = jnp.broadcast_to(lb_im, (8, S5_STATES))

    def s5_step(hs, row):
        hr, hi = hs
        nr = lbr * hr - lbi * hi + xr[pl.ds(row, 8), :]
        ni = lbr * hi + lbi * hr + xi[pl.ds(row, 8), :]
        xr[pl.ds(row, 8), :] = nr
        xi[pl.ds(row, 8), :] = ni
        return (nr, ni)
    _scan_time_major(t_len, bsz, (s5re_o, s5im_o), s5_step)
    y = (_bdot(xr[...], wcre_ref[...]) - _bdot(xi[...], wcim_ref[...])
         + dskip_ref[...] * s5_ref[...])
    y = jax.nn.gelu(y)
    glu = _bdot(y, wglu_ref[...]) + bglu_ref[...]
    o_ref[:, 0:256] = glu[:, 0:256] * jax.nn.sigmoid(glu[:, 256:512])

    cxs[cb:cb + rows, :] = cv_ref[:, 0:256] * jax.nn.sigmoid(cv_ref[:, 256:512])
    yc = ccb_ref[...] + ccw_ref[0:1, :] * cxs[0:rows, :]
    for k in range(1, CV_K):
        yc = yc + ccw_ref[k:k + 1, :] * cxs[k * bsz:k * bsz + rows, :]
    mu = jnp.mean(yc, axis=-1, keepdims=True)
    ycc = yc - mu
    yn = ycc * lax.rsqrt(jnp.mean(ycc * ycc, axis=-1, keepdims=True) + EPS) * lng_ref[...] + lnb_ref[...]
    o_ref[:, 512:768] = _silu(yn)
    cbuf_o[...] = cxs[rows:rows + cb, :]
    if nsteps > 1:
        cxs[0:cb, :] = cxs[rows:rows + cb, :]


def _bcd_call(zmix, states, lw, t_len, bsz, tb):
    col_blocks = BCD_COL_BLOCKS
    n = zmix.shape[0]
    nsteps = t_len // tb
    rows = tb * bsz
    lb = (LRU_K - 1) * bsz
    cb = (CV_K - 1) * bsz
    assert nsteps == 1 or rows >= cb
    zspec = lambda w, blk: pl.BlockSpec((rows, w), lambda i: (i, blk))
    state_shapes = [(bsz, S5_STATES), (bsz, S5_STATES), (bsz, 256), (lb, 256), (cb, 256)]
    params = [lw["s5_lam_re"], lw["s5_lam_im"], lw["s5_log_step"], lw["s5_wb"], lw["s5_wcre"], lw["s5_wcim"],
              lw["s5_d"], lw["s5_w_glu"], lw["s5_b_glu"],
              lw["lru_conv_w"], lw["lru_conv_b"], lw["lru_wa"], lw["lru_b_a"], lw["lru_wx"], lw["lru_b_x"],
              lw["lru_lam"], lw["cv_conv_w"], lw["cv_conv_b"], lw["cv_ln_g"], lw["cv_ln_b"]]
    return pl.pallas_call(
        functools.partial(_bcd_kernel, tb, bsz, nsteps),
        grid=(nsteps,),
        in_specs=([zspec(512, col_blocks[0]), zspec(512, col_blocks[1]), zspec(256, col_blocks[2])]
                  + [_const_spec(s) for s in state_shapes]
                  + [_const_spec(p.shape) for p in params]),
        out_specs=[pl.BlockSpec((rows, 768), lambda i: (i, 0))]
                  + [pl.BlockSpec(s, lambda i: (0, 0)) for s in state_shapes],
        out_shape=[jax.ShapeDtypeStruct((n, 768), F32)]
                  + [jax.ShapeDtypeStruct(s, F32) for s in state_shapes],
        scratch_shapes=[pltpu.VMEM((rows, S5_STATES), F32), pltpu.VMEM((rows, S5_STATES), F32),
                        pltpu.VMEM((lb + rows, 256), F32), pltpu.VMEM((cb + rows, 256), F32),
                        pltpu.VMEM((rows, 256), F32), pltpu.VMEM((rows, 256), F32)],
        compiler_params=pltpu.CompilerParams(dimension_semantics=("arbitrary",), vmem_limit_bytes=VMEM_LIMIT),
        name="bcd",
    )(zmix, zmix, zmix, *states, *params)


def _block_diag(m):
    g, r, c = m.shape
    return (jnp.eye(g, dtype=m.dtype)[:, None, :, None] * m[:, :, None, :]).reshape(g * r, g * c)


def _layer_weights(l, p):
    row = lambda v: v.reshape(1, -1).astype(F32)
    w_in = p["w_in"][l]
    w_mix = jnp.concatenate(
        [w_in[:, 0:1024], jnp.repeat(w_in[:, 1024:1028], DK, axis=1), jnp.repeat(w_in[:, 1028:1032], DK, axis=1),
         w_in[:, 1288:1800], w_in[:, 1800:2312], w_in[:, 1032:1288]], axis=1)
    lw = {}
    lw["ffn1_wgu"], lw["ffn1_wd"] = p["ffn1_w_gu"][l].astype(BF16), p["ffn1_w_down"][l].astype(BF16)
    lw["ffn2_wgu"], lw["ffn2_wd"] = p["ffn2_w_gu"][l].astype(BF16), p["ffn2_w_down"][l].astype(BF16)
    lw["ffn1_norm"] = row(p["ffn1_norm"][l])
    lw["ffn2_norm"] = row(p["ffn2_norm"][l])
    lw["mix_norm"] = row(p["mix_norm"][l])
    lw["w_mix"] = w_mix.astype(BF16)
    lw["w_gate"] = w_in[:, 2312:].astype(BF16)
    lw["w_branch"] = p["w_branch"][l].astype(BF16)
    lw["w_out"] = p["w_out"][l].astype(BF16)
    lw["dn_conv_w"] = p["dn_conv_w"][l]
    lw["dn_a_log"] = row(jnp.repeat(p["dn_a_log"][l], DK))
    lw["dn_dt_bias"] = row(jnp.repeat(p["dn_dt_bias"][l], DK))
    lw["dn_norm"] = row(p["dn_norm"][l])
    lw["s5_lam_re"] = row(p["s5_lam_re"][l])
    lw["s5_lam_im"] = row(p["s5_lam_im"][l])
    lw["s5_log_step"] = row(jnp.repeat(p["s5_log_step"][l], 64))
    bdt = lambda w: _block_diag(jnp.swapaxes(w, 1, 2)).astype(BF16)
    lw["s5_wb"] = jnp.concatenate([bdt(p["s5_b_re"][l]), bdt(p["s5_b_im"][l])], axis=1)
    lw["s5_wcre"] = bdt(p["s5_c_re"][l])
    lw["s5_wcim"] = bdt(p["s5_c_im"][l])
    lw["s5_d"] = row(p["s5_d"][l])
    lw["s5_w_glu"] = p["s5_w_glu"][l].astype(BF16)
    lw["s5_b_glu"] = row(p["s5_b_glu"][l])
    lw["lru_conv_w"] = p["lru_conv_w"][l]
    lw["lru_conv_b"] = row(p["lru_conv_b"][l])
    lw["lru_wa"] = _block_diag(p["lru_w_a"][l]).astype(BF16)
    lw["lru_wx"] = _block_diag(p["lru_w_x"][l]).astype(BF16)
    lw["lru_b_a"] = row(p["lru_b_a"][l])
    lw["lru_b_x"] = row(p["lru_b_x"][l])
    lw["lru_lam"] = row(p["lru_lam"][l])
    lw["cv_conv_w"] = p["cv_conv_w"][l]
    lw["cv_conv_b"] = row(p["cv_conv_b"][l])
    lw["cv_ln_g"] = row(p["cv_ln_g"][l])
    lw["cv_ln_b"] = row(p["cv_ln_b"][l])
    return lw


def _to_time_major(a):
    bsz, k, c = a.shape
    return jnp.transpose(a, (1, 0, 2)).reshape(k * bsz, c)


def _from_time_major(a, bsz):
    k = a.shape[0] // bsz
    return jnp.transpose(a.reshape(k, bsz, a.shape[1]), (1, 0, 2))


def _bcd_states(st, bsz):
    _, _, s_re, s_im, s_lru, s_lruc, s_cv = st
    return (s_re.reshape(bsz, S5_STATES), s_im.reshape(bsz, S5_STATES), s_lru,
            _to_time_major(s_lruc), _to_time_major(s_cv))


def _new_states(n_dn, n_dnc, bcd_new, bsz):
    n_re, n_im, n_lru, n_lruc, n_cv = bcd_new
    return (n_dn, n_dnc, n_re.reshape(bsz, 16, 64), n_im.reshape(bsz, 16, 64), n_lru,
            _from_time_major(n_lruc, bsz), _from_time_major(n_cv, bsz))


def _layer_long(x, st, lw, final_norm, final, t_len, bsz, tm, tb):
    pad = (-t_len) % (DELTA_BLOCK * CHUNK)
    lead, rem = divmod(pad * bsz, tm)
    assert rem == 0
    x1, zd, zb, n_dnc = _pre_call(x, lw, tm, bsz, _to_time_major(st[1]), lead)
    n_dnc = _from_time_major(n_dnc, bsz)
    oa, n_dn = _delta_call(zd, st[0], lw)
    obcd, *bcd_new = _bcd_call(zb, _bcd_states(st, bsz), lw, t_len, bsz, tb)
    x3 = _post_call(x1, oa, obcd, lw, final_norm, final, tm, bsz, lead, out_bm=final)
    return x3, _new_states(n_dn, n_dnc, bcd_new, bsz)


def _layer_short(x, st, lw, final_norm, final, t_len, bsz, tm):
    x1, zd, zb = _pre_call(x, lw, tm)
    oa, n_dnc, n_dn = _delta_short_call(zd, t_len, st[1], st[0], lw)
    z_tm = _to_time_major(zb.reshape(bsz, t_len, BCD_COLS))
    obcd_tm, *bcd_new = _bcd_call(z_tm, _bcd_states(st, bsz), lw, t_len, bsz, t_len)
    obcd = _from_time_major(obcd_tm, bsz).reshape(bsz * t_len, 768)
    x3 = _post_call(x1, oa, obcd, lw, final_norm, final, tm)
    return x3, _new_states(n_dn, n_dnc, bcd_new, bsz)


def _zero_state(bsz):
    return (jnp.zeros((bsz, HEADS, DK, DK), F32), jnp.zeros((bsz, DN_K - 1, QKV_W), F32),
            jnp.zeros((bsz, 16, 64), F32), jnp.zeros((bsz, 16, 64), F32), jnp.zeros((bsz, 256), F32),
            jnp.zeros((bsz, LRU_K - 1, 256), F32), jnp.zeros((bsz, CV_K - 1, 256), F32))


def kernel(x_prompt, x_sample, state_delta, state_delta_conv, state_s5_re, state_s5_im, state_lru, state_lru_conv, state_conv, meta_tokens, ffn1_norm, ffn1_w_gu, ffn1_w_down, mix_norm, w_in, dn_conv_w, dn_a_log, dn_dt_bias, dn_norm, s5_lam_re, s5_lam_im, s5_log_step, s5_b_re, s5_b_im, s5_c_re, s5_c_im, s5_d, s5_w_glu, s5_b_glu, lru_conv_w, lru_conv_b, lru_w_a, lru_b_a, lru_w_x, lru_b_x, lru_lam, cv_conv_w, cv_conv_b, cv_ln_g, cv_ln_b, w_branch, w_out, ffn2_norm, ffn2_w_gu, ffn2_w_down, final_norm):
    p = dict(ffn1_norm=ffn1_norm, ffn1_w_gu=ffn1_w_gu, ffn1_w_down=ffn1_w_down, mix_norm=mix_norm, w_in=w_in,
             dn_conv_w=dn_conv_w, dn_a_log=dn_a_log, dn_dt_bias=dn_dt_bias, dn_norm=dn_norm,
             s5_lam_re=s5_lam_re, s5_lam_im=s5_lam_im, s5_log_step=s5_log_step, s5_b_re=s5_b_re,
             s5_b_im=s5_b_im, s5_c_re=s5_c_re, s5_c_im=s5_c_im, s5_d=s5_d, s5_w_glu=s5_w_glu,
             s5_b_glu=s5_b_glu, lru_conv_w=lru_conv_w, lru_conv_b=lru_conv_b, lru_w_a=lru_w_a,
             lru_b_a=lru_b_a, lru_w_x=lru_w_x, lru_b_x=lru_b_x, lru_lam=lru_lam, cv_conv_w=cv_conv_w,
             cv_conv_b=cv_conv_b, cv_ln_g=cv_ln_g, cv_ln_b=cv_ln_b, w_branch=w_branch, w_out=w_out,
             ffn2_norm=ffn2_norm, ffn2_w_gu=ffn2_w_gu, ffn2_w_down=ffn2_w_down)
    depth = w_in.shape[0]
    bp, seq, _ = x_prompt.shape
    bs, dseq, _ = x_sample.shape
    tp = seq + N_META
    fnorm = final_norm.reshape(1, D_MODEL)

    meta = jnp.broadcast_to(meta_tokens[None], (bp, N_META, D_MODEL))
    xp = jnp.concatenate([meta, x_prompt], axis=1)
    xs = x_sample.reshape(bs * dseq, D_MODEL)

    p_new, s_new = [], []
    for l in range(depth):
        lw = _layer_weights(l, p)
        final = l == depth - 1
        xp, st_p = _layer_long(xp, _zero_state(bp), lw, fnorm, final, tp, bp, tm=384, tb=129)
        st_s = (state_delta[l], state_delta_conv[l], state_s5_re[l], state_s5_im[l], state_lru[l],
                state_lru_conv[l], state_conv[l])
        xs, st_s = _layer_short(xs, st_s, lw, fnorm, final, dseq, bs, tm=512)
        p_new.append(st_p)
        s_new.append(st_s)

    y_prompt = xp[:, N_META:]
    y_sample = xs.reshape(bs, dseq, D_MODEL)
    stack = lambda new, i: jnp.stack([st[i] for st in new], axis=0)
    return (y_prompt, y_sample, *[stack(p_new, i) for i in range(7)], *[stack(s_new, i) for i in range(7)])
```

```python
import functools

import jax
import jax.numpy as jnp
from jax import lax
from jax.experimental import pallas as pl
from jax.experimental.pallas import tpu as pltpu

F32 = jnp.float32
BF16 = jnp.bfloat16

LANES = 128
D_MODEL = 1024
D_FF = 2816
N_META = 16
EPS = 1e-6
HEADS = 4
DK = 64
QKV_W = 768
CHUNK = 64
DELTA_BLOCK = 3
SOLVE_BLOCK = 16
SOLVE_PASSES = 1
S5_STATES = 1024
LRU_C = 8.0
CV_K = 31
LRU_K = 4
DN_K = 4

FF_CHUNK = 256
FF_NCHUNK = D_FF // FF_CHUNK
DELTA_COLS = 1536
BCD_COLS = 1280
BCD_COL_BLOCKS = (0, 1, 4)
MIX_COLS = DELTA_COLS + BCD_COLS

VMEM_LIMIT = 56 * 1024 * 1024


def _const_spec(shape):
    nd = len(shape)
    return pl.BlockSpec(shape, lambda *_: (0,) * nd, pipeline_mode=pl.Buffered(1))


def _bdot(a, b):
    return jnp.dot(a.astype(BF16), b.astype(BF16), preferred_element_type=F32)


def _hdot(a, b):
    return jnp.dot(a, b, precision=lax.Precision.HIGHEST, preferred_element_type=F32)


def _rms(x, g):
    return x * lax.rsqrt(jnp.mean(x * x, axis=-1, keepdims=True) + EPS) * g


def _silu(x):
    return x * jax.nn.sigmoid(x)


def _softplus(x):
    return jnp.maximum(x, 0.0) + jnp.log1p(jnp.exp(-jnp.abs(x)))


def _swiglu_residual(x, g_ref, wgu_ref, wd_ref):
    xn = _rms(x, g_ref[...]).astype(BF16)
    acc = jnp.zeros_like(x)
    for c in range(FF_NCHUNK):
        lo, hi = c * FF_CHUNK, (c + 1) * FF_CHUNK
        gate = jnp.dot(xn, wgu_ref[:, lo:hi], preferred_element_type=F32)
        up = jnp.dot(xn, wgu_ref[:, D_FF + lo:D_FF + hi], preferred_element_type=F32)
        h = (_silu(gate) * up).astype(BF16)
        acc = acc + jnp.dot(h, wd_ref[lo:hi, :], preferred_element_type=F32)
    return x + 0.5 * acc


def _to_time_major_rows(src_ref, stage, bsz):
    steps, ntile = src_ref.shape[1], src_ref.shape[2] // LANES
    for b in range(bsz):
        for j in range(ntile):
            stage[j, pl.ds(b, steps, stride=bsz), :] = src_ref[b, :, j * LANES:(j + 1) * LANES]
    return jnp.concatenate([stage[j] for j in range(ntile)], axis=1)


def _from_time_major_rows(stage, dst_ref, bsz):
    steps, ntile = dst_ref.shape[1], dst_ref.shape[2] // LANES
    for b in range(bsz):
        for j in range(ntile):
            dst_ref[b, :, j * LANES:(j + 1) * LANES] = stage[j, pl.ds(b, steps, stride=bsz), :]


def _ffn_inproj(x, g1_ref, wgu_ref, wd_ref, gm_ref, wmix_ref, x1_ref, zb_ref):
    x1 = _swiglu_residual(x, g1_ref, wgu_ref, wd_ref)
    x1_ref[...] = x1
    u = _rms(x1, gm_ref[...]).astype(BF16)
    for c in range(BCD_COLS // 256):
        sl = slice(c * 256, (c + 1) * 256)
        zb_ref[:, sl] = jnp.dot(u, wmix_ref[:, DELTA_COLS + c * 256:DELTA_COLS + (c + 1) * 256],
                                preferred_element_type=F32)
    return lambda c: jnp.dot(u, wmix_ref[:, c * 256:(c + 1) * 256], preferred_element_type=F32)


def _pre_kernel(x_ref, g1_ref, wgu_ref, wd_ref, gm_ref, wmix_ref, x1_ref, zd_ref, zb_ref):
    zd_block = _ffn_inproj(x_ref[...], g1_ref, wgu_ref, wd_ref, gm_ref, wmix_ref, x1_ref, zb_ref)
    for c in range(DELTA_COLS // 256):
        zd_ref[:, c * 256:(c + 1) * 256] = zd_block(c)


def _pre_tm_kernel(bsz, lead, x_bm, x_ref, g1_ref, wgu_ref, wd_ref, gm_ref, wmix_ref, cw_ref, buf_ref,
                   x1_ref, zd_ref, zb_ref, nbuf_ref, stage, carry, *stage_x):
    tm = x1_ref.shape[0]
    lb = (DN_K - 1) * bsz
    step = pl.program_id(0)

    @pl.when(step < lead)
    def _():
        zd_ref[...] = jnp.zeros(zd_ref.shape, F32)

    @pl.when(step == lead)
    def _():
        carry[...] = buf_ref[...]

    @pl.when(step >= lead)
    def _():
        x = _to_time_major_rows(x_ref, stage_x[0], bsz) if x_bm else x_ref[...]
        zd_block = _ffn_inproj(x, g1_ref, wgu_ref, wd_ref, gm_ref, wmix_ref, x1_ref, zb_ref)
        for c in range(DELTA_COLS // 256):
            sl = slice(c * 256, (c + 1) * 256)
            zc = zd_block(c)
            if c < QKV_W // 256:
                win = jnp.concatenate([carry[:, sl], zc], axis=0)
                y = cw_ref[0:1, sl] * win[0:tm]
                for k in range(1, DN_K):
                    y = y + cw_ref[k:k + 1, sl] * win[k * bsz:k * bsz + tm]
                carry[:, sl] = zc[tm - lb:, :]
                nbuf_ref[:, sl] = zc[tm - lb:, :]
                zc = _silu(y)
            stage[2 * c] = zc[:, 0:LANES]
            stage[2 * c + 1] = zc[:, LANES:2 * LANES]
        _from_time_major_rows(stage, zd_ref, bsz)


def _pre_call(x, lw, tm, bsz=0, conv_buf=None, lead=0):
    x_bm = x.ndim == 3
    n = x.shape[0] * x.shape[1] if x_bm else x.shape[0]
    tile = lambda i: jnp.maximum(i - lead, 0)
    row = lambda w: pl.BlockSpec((tm, w), lambda i: (tile(i), 0))
    x_spec = pl.BlockSpec((bsz, tm // bsz, D_MODEL), lambda i: (0, tile(i), 0)) if x_bm else row(D_MODEL)
    in_specs = [x_spec, _const_spec((1, D_MODEL)), _const_spec((D_MODEL, 2 * D_FF)),
                _const_spec((D_FF, D_MODEL)), _const_spec((1, D_MODEL)), _const_spec((D_MODEL, MIX_COLS))]
    args = [x, lw["ffn1_norm"], lw["ffn1_wgu"], lw["ffn1_wd"], lw["mix_norm"], lw["w_mix"]]
    out_specs = [row(D_MODEL), row(DELTA_COLS), row(BCD_COLS)]
    out_shape = [jax.ShapeDtypeStruct((n, D_MODEL), F32), jax.ShapeDtypeStruct((n, DELTA_COLS), F32),
                 jax.ShapeDtypeStruct((n, BCD_COLS), F32)]
    body, scratch = _pre_kernel, []
    if bsz:
        lb = (DN_K - 1) * bsz
        steps = tm // bsz
        body = functools.partial(_pre_tm_kernel, bsz, lead, x_bm)
        in_specs += [_const_spec((DN_K, QKV_W)), _const_spec((lb, QKV_W))]
        args += [lw["dn_conv_w"], conv_buf]
        out_specs[1] = pl.BlockSpec((bsz, steps, DELTA_COLS), lambda i: (0, i, 0))
        out_shape[1] = jax.ShapeDtypeStruct((bsz, lead * steps + n // bsz, DELTA_COLS), F32)
        out_specs.append(pl.BlockSpec((lb, QKV_W), lambda i: (0, 0)))
        out_shape.append(jax.ShapeDtypeStruct((lb, QKV_W), F32))
        scratch = [pltpu.VMEM((DELTA_COLS // LANES, tm, LANES), F32), pltpu.VMEM((lb, QKV_W), F32)]
        if x_bm:
            scratch.append(pltpu.VMEM((D_MODEL // LANES, tm, LANES), F32))
    return pl.pallas_call(
        body, grid=(lead + n // tm,), in_specs=in_specs, out_specs=out_specs, out_shape=out_shape,
        scratch_shapes=scratch,
        compiler_params=pltpu.CompilerParams(dimension_semantics=("arbitrary",), vmem_limit_bytes=VMEM_LIMIT),
        name="pre",
    )(*args)


def _post_kernel(final, bsz, out_bm, x1_ref, oa_ref, obcd_ref, gm_ref, wgate_ref, wbr_ref, wout_ref,
                 g2_ref, wgu_ref, wd_ref, gf_ref, out_ref, *scratch):
    x1 = x1_ref[...]
    oa = _to_time_major_rows(oa_ref, scratch[0], bsz) if bsz else oa_ref[...]
    u = _rms(x1, gm_ref[...]).astype(BF16)
    m = jnp.zeros_like(x1)
    for i in range(4):
        gates = jax.nn.sigmoid(jnp.dot(u, wgate_ref[:, i * D_MODEL:(i + 1) * D_MODEL],
                                       preferred_element_type=F32))
        br = oa if i == 0 else obcd_ref[:, (i - 1) * 256:i * 256]
        m = m + gates * _bdot(br, wbr_ref[i])
    x2 = x1 + _bdot(m, wout_ref[...])
    x3 = _swiglu_residual(x2, g2_ref, wgu_ref, wd_ref)
    if final:
        x3 = _rms(x3, gf_ref[...])
    if out_bm:
        for j in range(D_MODEL // LANES):
            scratch[1][j] = x3[:, j * LANES:(j + 1) * LANES]
        _from_time_major_rows(scratch[1], out_ref, bsz)
    else:
        out_ref[...] = x3


def _post_call(x1, oa, obcd, lw, final_norm, final, tm, bsz=0, lead=0, out_bm=False):
    n = x1.shape[0]
    row = lambda w: pl.BlockSpec((tm, w), lambda i: (i, 0))
    oa_spec = pl.BlockSpec((bsz, tm // bsz, 256), lambda i: (0, i + lead, 0)) if bsz else row(256)
    scratch = [pltpu.VMEM((256 // LANES, tm, LANES), F32)] if bsz else []
    out_spec, out_shape = row(D_MODEL), jax.ShapeDtypeStruct((n, D_MODEL), F32)
    if out_bm:
        scratch.append(pltpu.VMEM((D_MODEL // LANES, tm, LANES), F32))
        out_spec = pl.BlockSpec((bsz, tm // bsz, D_MODEL), lambda i: (0, i, 0))
        out_shape = jax.ShapeDtypeStruct((bsz, n // bsz, D_MODEL), F32)
    return pl.pallas_call(
        functools.partial(_post_kernel, final, bsz, out_bm),
        grid=(n // tm,),
        scratch_shapes=scratch,
        in_specs=[row(D_MODEL), oa_spec, row(768), _const_spec((1, D_MODEL)),
                  _const_spec((D_MODEL, 4 * D_MODEL)), _const_spec((4, 256, D_MODEL)),
                  _const_spec((D_MODEL, D_MODEL)), _const_spec((1, D_MODEL)),
                  _const_spec((D_MODEL, 2 * D_FF)), _const_spec((D_FF, D_MODEL)), _const_spec((1, D_MODEL))],
        out_specs=out_spec,
        out_shape=out_shape,
        compiler_params=pltpu.CompilerParams(dimension_semantics=("arbitrary",), vmem_limit_bytes=VMEM_LIMIT),
        name="post",
    )(x1, oa, obcd, lw["mix_norm"], lw["w_gate"], lw["w_branch"], lw["w_out"],
      lw["ffn2_norm"], lw["ffn2_wgu"], lw["ffn2_wd"], final_norm)


def _split16(x):
    hi = x.astype(BF16)
    if SOLVE_PASSES == 1:
        return (hi,)
    return hi, (x - hi.astype(F32)).astype(BF16)


def _bmm(a, b, contract=(2, 1)):
    dims = (((contract[0],), (contract[1],)), ((0,), (0,)))
    return lax.dot_general(a, b, dims, preferred_element_type=F32)


def _dot3(a, b):
    out = _bmm(a[0], b[0])
    if SOLVE_PASSES >= 2:
        out = out + _bmm(a[1], b[0])
    if SOLVE_PASSES >= 3:
        out = out + _bmm(a[0], b[1])
    return out


def _chunk_masks(seq_len):
    ri = lax.broadcasted_iota(jnp.int32, (CHUNK, CHUNK), 0)
    ci = lax.broadcasted_iota(jnp.int32, (CHUNK, CHUNK), 1)
    causal = ri >= ci
    strict = ri > ci
    if seq_len < CHUNK:
        same = (ri // seq_len) == (ci // seq_len)
        causal = causal & same
        strict = strict & same
    return dict(causal_f=causal.astype(F32), strict_f=strict.astype(F32), eye_f=(ri == ci).astype(F32),
                blk_f=((ri // SOLVE_BLOCK) == (ci // SOLVE_BLOCK)).astype(F32), rowseq=ri // seq_len,
                rowpos=lax.broadcasted_iota(jnp.int32, (CHUNK, 256), 0) % seq_len)


def _wy_solve(low, rhs, mk, seq_len):
    eye_f = mk["eye_f"]
    if seq_len > SOLVE_BLOCK:
        nd = low * mk["blk_f"]
        off = low - nd
        blk = SOLVE_BLOCK
    else:
        nd, off, blk = low, None, seq_len
    p = eye_f - nd
    ns = _split16(nd)
    for _ in range(blk.bit_length() - 2):
        ns = _split16(_dot3(ns, ns))
        p = p + _dot3(_split16(p), ns)
    ps = _split16(p)
    y = _dot3(ps, _split16(rhs))
    if off is None:
        return y
    assert seq_len // SOLVE_BLOCK == 4
    ms = _split16(_dot3(ps, _split16(off)))
    y2 = y + _dot3(_split16(_dot3(ms, ms)), _split16(y))
    return y2 - _dot3(ms, _split16(y2))


def _delta_elementwise(chunks, mk, seq_len):
    qs, ks, vs, bs, gcs = [], [], [], [], []
    for q_in, k_in, v_in, beta, gl in chunks:
        gc = gl
        shift = 1
        while shift < seq_len:
            gc = gc + jnp.where(mk["rowpos"] >= shift, pltpu.roll(gc, shift, axis=0), 0.0)
            shift *= 2
        for h in range(HEADS):
            sl = slice(h * DK, (h + 1) * DK)
            qs.append(q_in[:, sl])
            ks.append(k_in[:, sl])
            vs.append(v_in[:, sl])
            bs.append(beta[:, sl])
            gcs.append(gc[:, sl])
    q, k, v, b, gc = [jnp.stack(a, axis=0) for a in (qs, ks, vs, bs, gcs)]
    q = q * lax.rsqrt(jnp.sum(q * q, axis=-1, keepdims=True) + EPS) * (DK ** -0.5)
    k = k * lax.rsqrt(jnp.sum(k * k, axis=-1, keepdims=True) + EPS)
    gct = jnp.sum(gc * mk["eye_f"], axis=1, keepdims=True)
    dec = jnp.exp(jnp.minimum(gc - gct, 0.0))
    eg = jnp.exp(gc)
    kb = k * b
    if seq_len == CHUNK:
        glast = gc[:, CHUNK - 1:CHUNK, :]
    else:
        glast = jnp.concatenate(
            [jnp.broadcast_to(gc[:, (j + 1) * seq_len - 1:(j + 1) * seq_len, :], (gc.shape[0], seq_len, DK))
             for j in range(CHUNK // seq_len)], axis=1)
    return dict(kb16=kb.astype(BF16), k16=k.astype(BF16), q16=q.astype(BF16), dec=dec,
                rhs=jnp.concatenate([v * b, kb * eg], axis=2), qe16=(q * eg).astype(BF16),
                kd16=(k * jnp.exp(glast - gc)).astype(BF16), egl=jnp.exp(glast))


def _delta_solve_store(ew, mk, seq_len, scr, slots):
    u_s, w_s, qe_s, kd_s, a_s, egl_s = scr
    kk = _bmm(ew["kb16"], ew["k16"], (2, 2))
    qk = _bmm(ew["q16"], ew["k16"], (2, 2))
    low = kk * (ew["dec"] * mk["strict_f"])
    x = _wy_solve(low, ew["rhs"], mk, seq_len)
    w16 = x[:, :, DK:2 * DK].astype(BF16)
    a16 = (qk * (ew["dec"] * mk["causal_f"])).astype(BF16)
    qe16, kd16, egl = ew["qe16"], ew["kd16"], ew["egl"]
    for c, (lead, slot) in enumerate(slots):
        dst = (pl.ds(lead, HEADS), pl.ds(slot * CHUNK, CHUNK), slice(None))
        ps = slice(c * HEADS, (c + 1) * HEADS)
        u_s[dst] = x[ps, :, 0:DK]
        w_s[dst] = w16[ps]
        qe_s[dst] = qe16[ps]
        kd_s[dst] = kd16[ps]
        a_s[dst] = a16[ps]
        egl_s[dst] = jnp.broadcast_to(egl[ps], (HEADS, CHUNK, DK))


def _gated_norm(o, dz, ng):
    return o * lax.rsqrt(jnp.mean(o * o, axis=-1, keepdims=True) + EPS) * ng * _silu(dz)


def _delta_scratch(lead, nslots):
    n = nslots * CHUNK
    return ([pltpu.VMEM((lead, n, DK), F32)] + [pltpu.VMEM((lead, n, DK), BF16)] * 4
            + [pltpu.VMEM((lead, n, DK), F32)])


def _delta_kernel(bsz, nchunk, q_ref, k_ref, v_ref, dz_ref, be_ref, ae_ref, s0_ref, alog_ref, dtb_ref, ng_ref,
                  o_ref, s_ref, *scr):
    @pl.when(pl.program_id(0) == 0)
    def _():
        s_ref[...] = s0_ref[...]

    u_s, w_s, qe_s, kd_s, a_s, egl_s = scr
    mk = _chunk_masks(CHUNK)
    neg_a = -jnp.exp(alog_ref[...])
    dtb = dtb_ref[...]
    ng = ng_ref[...]

    def elementwise(c):
        rows = pl.ds(c * CHUNK, CHUNK)
        chunks = []
        for b in range(bsz):
            beta = jax.nn.sigmoid(be_ref[b, rows, :])
            gl = neg_a * _softplus(ae_ref[b, rows, :] + dtb)
            chunks.append((q_ref[b, rows, :], k_ref[b, rows, :], v_ref[b, rows, :], beta, gl))
        return _delta_elementwise(chunks, mk, CHUNK)

    ew = elementwise(0)
    for c in range(nchunk):
        ew_next = elementwise(c + 1) if c + 1 < nchunk else None
        _delta_solve_store(ew, mk, CHUNK, scr, [(b * HEADS, c) for b in range(bsz)])
        ew = ew_next

    def step(c, carry):
        rows = pl.ds(pl.multiple_of(c * CHUNK, CHUNK), CHUNK)
        s = s_ref[...]
        s16 = s.astype(BF16)
        vnew = u_s[:, rows, :] - _bmm(w_s[:, rows, :], s16)
        vn16 = vnew.astype(BF16)
        o = _bmm(qe_s[:, rows, :], s16) + _bmm(a_s[:, rows, :], vn16)
        s_ref[...] = (s * egl_s[:, pl.ds(pl.multiple_of(c * CHUNK, CHUNK), 8), :][:, 0:1, :]
                      + _bmm(kd_s[:, rows, :], vn16, (1, 1)))
        for b in range(bsz):
            for h in range(HEADS):
                sl = slice(h * DK, (h + 1) * DK)
                o_ref[b, rows, sl] = _gated_norm(o[b * HEADS + h], dz_ref[b, rows, sl], ng)
        return carry
    lax.fori_loop(0, nchunk, step, 0)


def _delta_call(zd, s0, lw, nchunk=DELTA_BLOCK):
    bsz, t_len, _ = zd.shape
    tb = nchunk * CHUNK
    assert t_len % tb == 0
    zspec = lambda blk: pl.BlockSpec((bsz, tb, 256), lambda i: (0, i, blk))
    sspec = pl.BlockSpec((bsz * HEADS, DK, DK), lambda i: (0, 0, 0))
    o, s = pl.pallas_call(
        functools.partial(_delta_kernel, bsz, nchunk),
        grid=(t_len // tb,),
        in_specs=[zspec(0), zspec(1), zspec(2), zspec(3), zspec(4), zspec(5), sspec,
                  _const_spec((1, 256)), _const_spec((1, 256)), _const_spec((1, DK))],
        out_specs=[pl.BlockSpec((bsz, tb, 256), lambda i: (0, i, 0)), sspec],
        out_shape=[jax.ShapeDtypeStruct((bsz, t_len, 256), F32),
                   jax.ShapeDtypeStruct((bsz * HEADS, DK, DK), F32)],
        scratch_shapes=_delta_scratch(bsz * HEADS, nchunk),
        compiler_params=pltpu.CompilerParams(dimension_semantics=("arbitrary",), vmem_limit_bytes=VMEM_LIMIT),
        name="delta",
    )(zd, zd, zd, zd, zd, zd, s0.reshape(bsz * HEADS, DK, DK), lw["dn_a_log"], lw["dn_dt_bias"], lw["dn_norm"])
    return o, s.reshape(bsz, HEADS, DK, DK)


def _delta_short_kernel(seq_len, q_ref, k_ref, v_ref, dz_ref, be_ref, ae_ref, buf_ref, s0_ref, cw_ref,
                        alog_ref, dtb_ref, ng_ref, o_ref, nbuf_ref, sfin_ref, xs_ref, *scr):
    nseq = CHUNK // seq_len
    span = seq_len + 8
    mk = _chunk_masks(seq_len)
    cw = cw_ref[...]
    xs_ref[...] = jnp.zeros(xs_ref.shape, F32)
    for j in range(nseq):
        xs_ref[j * span + 5:j * span + 8, :] = buf_ref[j]
        rows = slice(j * seq_len, (j + 1) * seq_len)
        xs_ref[j * span + 8:(j + 1) * span, 0:256] = q_ref[rows, :]
        xs_ref[j * span + 8:(j + 1) * span, 256:512] = k_ref[rows, :]
        xs_ref[j * span + 8:(j + 1) * span, 512:768] = v_ref[rows, :]
    ys = []
    for j in range(nseq):
        win = xs_ref[j * span:(j + 1) * span, :]
        ys.append(cw[3:4] * win[8:span] + cw[2:3] * win[7:span - 1]
                  + cw[1:2] * win[6:span - 2] + cw[0:1] * win[5:span - 3])
        nbuf_ref[j] = win[span - 3:span, :]
    qkv = _silu(jnp.concatenate(ys, axis=0))
    beta = jax.nn.sigmoid(be_ref[...])
    gl = -jnp.exp(alog_ref[...]) * _softplus(ae_ref[...] + dtb_ref[...])
    ew = _delta_elementwise([(qkv[:, 0:256], qkv[:, 256:512], qkv[:, 512:768], beta, gl)], mk, seq_len)
    _delta_solve_store(ew, mk, seq_len, scr, [(0, 0)])

    u_s, w_s, qe_s, kd_s, a_s, egl_s = scr
    w = w_s[...]
    qe = qe_s[...]
    kd = kd_s[...].astype(F32)
    u = u_s[...]
    ws, qs = [], []
    for j in range(nseq):
        s16 = s0_ref[j].astype(BF16)
        ws.append(_bmm(w, s16))
        qs.append(_bmm(qe, s16))
    vnew, o = u, jnp.zeros_like(u)
    for j in range(nseq):
        mine = mk["rowseq"] == j
        vnew = jnp.where(mine, u - ws[j], vnew)
        o = jnp.where(mine, qs[j], o)
    vn16 = vnew.astype(BF16)
    o = o + _bmm(a_s[...], vn16)
    for j in range(nseq):
        kdj = jnp.where(mk["rowseq"] == j, kd, 0.0).astype(BF16)
        sfin_ref[j] = (s0_ref[j] * egl_s[:, j * seq_len:(j + 1) * seq_len, :][:, 0:1, :]
                       + _bmm(kdj, vn16, (1, 1)))
    ng = ng_ref[...]
    for h in range(HEADS):
        sl = slice(h * DK, (h + 1) * DK)
        o_ref[:, sl] = _gated_norm(o[h], dz_ref[:, sl], ng)


def _delta_short_call(z, seq_len, buf, s0, lw):
    n = z.shape[0]
    nseq = CHUNK // seq_len
    zspec = lambda blk: pl.BlockSpec((CHUNK, 256), lambda g: (g, blk))
    bspec = pl.BlockSpec((nseq, DN_K - 1, QKV_W), lambda g: (g, 0, 0))
    sspec = pl.BlockSpec((nseq, HEADS, DK, DK), lambda g: (g, 0, 0, 0))
    return pl.pallas_call(
        functools.partial(_delta_short_kernel, seq_len),
        grid=(n // CHUNK,),
        in_specs=[zspec(0), zspec(1), zspec(2), zspec(3), zspec(4), zspec(5), bspec, sspec, _const_spec((DN_K, QKV_W)), _const_spec((1, 256)), _const_spec((1, 256)),
                  _const_spec((1, DK))],
        out_specs=[pl.BlockSpec((CHUNK, 256), lambda g: (g, 0)), bspec, sspec],
        out_shape=[jax.ShapeDtypeStruct((n, 256), F32), jax.ShapeDtypeStruct(buf.shape, F32),
                   jax.ShapeDtypeStruct(s0.shape, F32)],
        scratch_shapes=[pltpu.VMEM((nseq * (seq_len + 8), QKV_W), F32)] + _delta_scratch(HEADS, 1),
        compiler_params=pltpu.CompilerParams(dimension_semantics=("arbitrary",), vmem_limit_bytes=VMEM_LIMIT),
        name="delta_short",
    )(z, z, z, z, z, z, buf, s0, lw["dn_conv_w"], lw["dn_a_log"], lw["dn_dt_bias"], lw["dn_norm"])


def _scan_time_major(t_len, bsz, state_refs, step):
    def run_group(goff):
        hs = tuple(r[pl.ds(goff, 8), :] for r in state_refs)
        if t_len <= 8:
            for t in range(t_len):
                hs = step(hs, t * bsz + goff)
        else:
            def body(t, hs):
                return step(hs, pl.multiple_of(t * bsz + goff, 8))
            hs = lax.fori_loop(0, t_len, body, hs, unroll=3)
        for r, h in zip(state_refs, hs):
            r[pl.ds(goff, 8), :] = h

    if bsz == 8:
        run_group(0)
    else:
        def gbody(g, carry):
            run_group(pl.multiple_of(g * 8, 8))
            return carry
        lax.fori_loop(0, bsz // 8, gbody, 0)


def _bcd_kernel(t_len, bsz, nsteps,
                lru_ref, cv_ref, s5_ref, s5re0, s5im0, lru0, lbuf0, cbuf0,
                lam_re_ref, lam_im_ref, lstep_ref, wb_ref, wcre_ref, wcim_ref, dskip_ref, wglu_ref, bglu_ref,
                lcw_ref, lcb_ref, wa_ref, ba_ref, wx_ref, bx_ref, llam_ref,
                ccw_ref, ccb_ref, lng_ref, lnb_ref,
                o_ref, s5re_o, s5im_o, lru_o, lbuf_o, cbuf_o,
                xr, xi, lxs, cxs, a_s, b_s, wbp):
    rows = t_len * bsz
    lb = (LRU_K - 1) * bsz
    cb = (CV_K - 1) * bsz

    @pl.when(pl.program_id(0) == 0)
    def _():
        s5re_o[...] = s5re0[...]
        s5im_o[...] = s5im0[...]
        lru_o[...] = lru0[...]
        lxs[0:lb, :] = lbuf0[...]
        cxs[0:cb, :] = cbuf0[...]

    lxs[lb:lb + rows, :] = lru_ref[:, 0:256]
    xf = lcb_ref[...] + lcw_ref[0:1, :] * lxs[0:rows, :]
    for k in range(1, LRU_K):
        xf = xf + lcw_ref[k:k + 1, :] * lxs[k * bsz:k * bsz + rows, :]
    r = jax.nn.sigmoid(_bdot(xf, wa_ref[...]) + ba_ref[...])
    i = jax.nn.sigmoid(_bdot(xf, wx_ref[...]) + bx_ref[...])
    log_a = (-LRU_C) * r * _softplus(-llam_ref[...])
    a_s[...] = jnp.exp(log_a)
    b_s[...] = jnp.sqrt(1.0 - jnp.exp(2.0 * log_a)) * (i * xf)

    def lru_step(hs, row):
        h = a_s[pl.ds(row, 8), :] * hs[0] + b_s[pl.ds(row, 8), :]
        b_s[pl.ds(row, 8), :] = h
        return (h,)
    _scan_time_major(t_len, bsz, (lru_o,), lru_step)
    o_ref[:, 256:512] = b_s[...] * jax.nn.gelu(lru_ref[:, 256:512])
    lbuf_o[...] = lxs[rows:rows + lb, :]
    if nsteps > 1:
        lxs[0:lb, :] = lxs[rows:rows + lb, :]

    dt = jnp.exp(lstep_ref[...])
    lam_re = lam_re_ref[...]
    lam_im = lam_im_ref[...]
    mag = jnp.exp(lam_re * dt)
    lb_re = mag * jnp.cos(lam_im * dt)
    lb_im = mag * jnp.sin(lam_im * dt)
    den = lam_re * lam_re + lam_im * lam_im
    cf_re = ((lb_re - 1.0) * lam_re + lb_im * lam_im) / den
    cf_im = (lb_im * lam_re - (lb_re - 1.0) * lam_im) / den

    @pl.when(pl.program_id(0) == 0)
    def _():
        b_re = wb_ref[:, 0:S5_STATES]
        b_im = wb_ref[:, S5_STATES:2 * S5_STATES]
        wbp[:, 0:S5_STATES] = (cf_re * b_re - cf_im * b_im).astype(BF16)
        wbp[:, S5_STATES:2 * S5_STATES] = (cf_re * b_im + cf_im * b_re).astype(BF16)

    u16 = s5_ref[...].astype(BF16)
    xr[...] = jnp.dot(u16, wbp[:, 0:S5_STATES], preferred_element_type=F32)
    xi[...] = jnp.dot(u16, wbp[:, S5_STATES:2 * S5_STATES], preferred_element_type=F32)
    lbr = jnp.broadcast_to(lb_re, (8, S5_STATES))
    lbi = jnp.broadcast_to(lb_im, (8, S5_STATES))

    def s5_step(hs, row):
        hr, hi = hs
        nr = lbr * hr - lbi * hi + xr[pl.ds(row, 8), :]
        ni = lbr * hi + lbi * hr + xi[pl.ds(row, 8), :]
        xr[pl.ds(row, 8), :] = nr
        xi[pl.ds(row, 8), :] = ni
        return (nr, ni)
    _scan_time_major(t_len, bsz, (s5re_o, s5im_o), s5_step)
    y = (_bdot(xr[...], wcre_ref[...]) - _bdot(xi[...], wcim_ref[...])
         + dskip_ref[...] * s5_ref[...])
    y = jax.nn.gelu(y)
    glu = _bdot(y, wglu_ref[...]) + bglu_ref[...]
    o_ref[:, 0:256] = glu[:, 0:256] * jax.nn.sigmoid(glu[:, 256:512])

    cxs[cb:cb + rows, :] = cv_ref[:, 0:256] * jax.nn.sigmoid(cv_ref[:, 256:512])
    yc = ccb_ref[...] + ccw_ref[0:1, :] * cxs[0:rows, :]
    for k in range(1, CV_K):
        yc = yc + ccw_ref[k:k + 1, :] * cxs[k * bsz:k * bsz + rows, :]
    mu = jnp.mean(yc, axis=-1, keepdims=True)
    ycc = yc - mu
    yn = ycc * lax.rsqrt(jnp.mean(ycc * ycc, axis=-1, keepdims=True) + EPS) * lng_ref[...] + lnb_ref[...]
    o_ref[:, 512:768] = _silu(yn)
    cbuf_o[...] = cxs[rows:rows + cb, :]
    if nsteps > 1:
        cxs[0:cb, :] = cxs[rows:rows + cb, :]


def _bcd_call(zmix, states, lw, t_len, bsz, tb):
    col_blocks = BCD_COL_BLOCKS
    n = zmix.shape[0]
    nsteps = t_len // tb
    rows = tb * bsz
    lb = (LRU_K - 1) * bsz
    cb = (CV_K - 1) * bsz
    assert nsteps == 1 or rows >= cb
    zspec = lambda w, blk: pl.BlockSpec((rows, w), lambda i: (i, blk))
    state_shapes = [(bsz, S5_STATES), (bsz, S5_STATES), (bsz, 256), (lb, 256), (cb, 256)]
    params = [lw["s5_lam_re"], lw["s5_lam_im"], lw["s5_log_step"], lw["s5_wb"], lw["s5_wcre"], lw["s5_wcim"],
              lw["s5_d"], lw["s5_w_glu"], lw["s5_b_glu"],
              lw["lru_conv_w"], lw["lru_conv_b"], lw["lru_wa"], lw["lru_b_a"], lw["lru_wx"], lw["lru_b_x"],
              lw["lru_lam"], lw["cv_conv_w"], lw["cv_conv_b"], lw["cv_ln_g"], lw["cv_ln_b"]]
    return pl.pallas_call(
        functools.partial(_bcd_kernel, tb, bsz, nsteps),
        grid=(nsteps,),
        in_specs=([zspec(512, col_blocks[0]), zspec(512, col_blocks[1]), zspec(256, col_blocks[2])]
                  + [_const_spec(s) for s in state_shapes]
                  + [_const_spec(p.shape) for p in params]),
        out_specs=[pl.BlockSpec((rows, 768), lambda i: (i, 0))]
                  + [pl.BlockSpec(s, lambda i: (0, 0)) for s in state_shapes],
        out_shape=[jax.ShapeDtypeStruct((n, 768), F32)]
                  + [jax.ShapeDtypeStruct(s, F32) for s in state_shapes],
        scratch_shapes=[pltpu.VMEM((rows, S5_STATES), F32), pltpu.VMEM((rows, S5_STATES), F32),
                        pltpu.VMEM((lb + rows, 256), F32), pltpu.VMEM((cb + rows, 256), F32),
                        pltpu.VMEM((rows, 256), F32), pltpu.VMEM((rows, 256), F32),
                        pltpu.VMEM((256, 2 * S5_STATES), BF16)],
        compiler_params=pltpu.CompilerParams(dimension_semantics=("arbitrary",), vmem_limit_bytes=VMEM_LIMIT),
        name="bcd",
    )(zmix, zmix, zmix, *states, *params)


def _block_diag(m):
    g, r, c = m.shape
    return (jnp.eye(g, dtype=m.dtype)[:, None, :, None] * m[:, :, None, :]).reshape(g * r, g * c)


def _layer_weights(l, p):
    row = lambda v: v.reshape(1, -1).astype(F32)
    w_in = p["w_in"][l]
    w_mix = jnp.concatenate(
        [w_in[:, 0:1024], jnp.repeat(w_in[:, 1024:1028], DK, axis=1), jnp.repeat(w_in[:, 1028:1032], DK, axis=1),
         w_in[:, 1288:1800], w_in[:, 1800:2312], w_in[:, 1032:1288]], axis=1)
    lw = {}
    lw["ffn1_wgu"], lw["ffn1_wd"] = p["ffn1_w_gu"][l].astype(BF16), p["ffn1_w_down"][l].astype(BF16)
    lw["ffn2_wgu"], lw["ffn2_wd"] = p["ffn2_w_gu"][l].astype(BF16), p["ffn2_w_down"][l].astype(BF16)
    lw["ffn1_norm"] = row(p["ffn1_norm"][l])
    lw["ffn2_norm"] = row(p["ffn2_norm"][l])
    lw["mix_norm"] = row(p["mix_norm"][l])
    lw["w_mix"] = w_mix.astype(BF16)
    lw["w_gate"] = w_in[:, 2312:].astype(BF16)
    lw["w_branch"] = p["w_branch"][l].astype(BF16)
    lw["w_out"] = p["w_out"][l].astype(BF16)
    lw["dn_conv_w"] = p["dn_conv_w"][l]
    lw["dn_a_log"] = row(jnp.repeat(p["dn_a_log"][l], DK))
    lw["dn_dt_bias"] = row(jnp.repeat(p["dn_dt_bias"][l], DK))
    lw["dn_norm"] = row(p["dn_norm"][l])
    lw["s5_lam_re"] = row(p["s5_lam_re"][l])
    lw["s5_lam_im"] = row(p["s5_lam_im"][l])
    lw["s5_log_step"] = row(jnp.repeat(p["s5_log_step"][l], 64))
    bdt = lambda w: _block_diag(jnp.swapaxes(w, 1, 2))
    lw["s5_wb"] = jnp.concatenate([bdt(p["s5_b_re"][l]), bdt(p["s5_b_im"][l])], axis=1).astype(F32)
    lw["s5_wcre"] = bdt(p["s5_c_re"][l]).astype(BF16)
    lw["s5_wcim"] = bdt(p["s5_c_im"][l]).astype(BF16)
    lw["s5_d"] = row(p["s5_d"][l])
    lw["s5_w_glu"] = p["s5_w_glu"][l].astype(BF16)
    lw["s5_b_glu"] = row(p["s5_b_glu"][l])
    lw["lru_conv_w"] = p["lru_conv_w"][l]
    lw["lru_conv_b"] = row(p["lru_conv_b"][l])
    lw["lru_wa"] = _block_diag(p["lru_w_a"][l]).astype(BF16)
    lw["lru_wx"] = _block_diag(p["lru_w_x"][l]).astype(BF16)
    lw["lru_b_a"] = row(p["lru_b_a"][l])
    lw["lru_b_x"] = row(p["lru_b_x"][l])
    lw["lru_lam"] = row(p["lru_lam"][l])
    lw["cv_conv_w"] = p["cv_conv_w"][l]
    lw["cv_conv_b"] = row(p["cv_conv_b"][l])
    lw["cv_ln_g"] = row(p["cv_ln_g"][l])
    lw["cv_ln_b"] = row(p["cv_ln_b"][l])
    return lw


def _to_time_major(a):
    bsz, k, c = a.shape
    return jnp.transpose(a, (1, 0, 2)).reshape(k * bsz, c)


def _from_time_major(a, bsz):
    k = a.shape[0] // bsz
    return jnp.transpose(a.reshape(k, bsz, a.shape[1]), (1, 0, 2))


def _bcd_states(st, bsz):
    _, _, s_re, s_im, s_lru, s_lruc, s_cv = st
    return (s_re.reshape(bsz, S5_STATES), s_im.reshape(bsz, S5_STATES), s_lru,
            _to_time_major(s_lruc), _to_time_major(s_cv))


def _new_states(n_dn, n_dnc, bcd_new, bsz):
    n_re, n_im, n_lru, n_lruc, n_cv = bcd_new
    return (n_dn, n_dnc, n_re.reshape(bsz, 16, 64), n_im.reshape(bsz, 16, 64), n_lru,
            _from_time_major(n_lruc, bsz), _from_time_major(n_cv, bsz))


def _layer_long(x, st, lw, final_norm, final, t_len, bsz, tm, tb):
    pad = (-t_len) % (DELTA_BLOCK * CHUNK)
    lead, rem = divmod(pad * bsz, tm)
    assert rem == 0
    x1, zd, zb, n_dnc = _pre_call(x, lw, tm, bsz, _to_time_major(st[1]), lead)
    n_dnc = _from_time_major(n_dnc, bsz)
    oa, n_dn = _delta_call(zd, st[0], lw)
    obcd, *bcd_new = _bcd_call(zb, _bcd_states(st, bsz), lw, t_len, bsz, tb)
    x3 = _post_call(x1, oa, obcd, lw, final_norm, final, tm, bsz, lead, out_bm=final)
    return x3, _new_states(n_dn, n_dnc, bcd_new, bsz)


def _layer_short(x, st, lw, final_norm, final, t_len, bsz, tm):
    x1, zd, zb = _pre_call(x, lw, tm)
    oa, n_dnc, n_dn = _delta_short_call(zd, t_len, st[1], st[0], lw)
    z_tm = _to_time_major(zb.reshape(bsz, t_len, BCD_COLS))
    obcd_tm, *bcd_new = _bcd_call(z_tm, _bcd_states(st, bsz), lw, t_len, bsz, t_len)
    obcd = _from_time_major(obcd_tm, bsz).reshape(bsz * t_len, 768)
    x3 = _post_call(x1, oa, obcd, lw, final_norm, final, tm)
    return x3, _new_states(n_dn, n_dnc, bcd_new, bsz)


def _zero_state(bsz):
    return (jnp.zeros((bsz, HEADS, DK, DK), F32), jnp.zeros((bsz, DN_K - 1, QKV_W), F32),
            jnp.zeros((bsz, 16, 64), F32), jnp.zeros((bsz, 16, 64), F32), jnp.zeros((bsz, 256), F32),
            jnp.zeros((bsz, LRU_K - 1, 256), F32), jnp.zeros((bsz, CV_K - 1, 256), F32))


def kernel(x_prompt, x_sample, state_delta, state_delta_conv, state_s5_re, state_s5_im, state_lru, state_lru_conv, state_conv, meta_tokens, ffn1_norm, ffn1_w_gu, ffn1_w_down, mix_norm, w_in, dn_conv_w, dn_a_log, dn_dt_bias, dn_norm, s5_lam_re, s5_lam_im, s5_log_step, s5_b_re, s5_b_im, s5_c_re, s5_c_im, s5_d, s5_w_glu, s5_b_glu, lru_conv_w, lru_conv_b, lru_w_a, lru_b_a, lru_w_x, lru_b_x, lru_lam, cv_conv_w, cv_conv_b, cv_ln_g, cv_ln_b, w_branch, w_out, ffn2_norm, ffn2_w_gu, ffn2_w_down, final_norm):
    p = dict(ffn1_norm=ffn1_norm, ffn1_w_gu=ffn1_w_gu, ffn1_w_down=ffn1_w_down, mix_norm=mix_norm, w_in=w_in,
             dn_conv_w=dn_conv_w, dn_a_log=dn_a_log, dn_dt_bias=dn_dt_bias, dn_norm=dn_norm,
             s5_lam_re=s5_lam_re, s5_lam_im=s5_lam_im, s5_log_step=s5_log_step, s5_b_re=s5_b_re,
             s5_b_im=s5_b_im, s5_c_re=s5_c_re, s5_c_im=s5_c_im, s5_d=s5_d, s5_w_glu=s5_w_glu,
             s5_b_glu=s5_b_glu, lru_conv_w=lru_conv_w, lru_conv_b=lru_conv_b, lru_w_a=lru_w_a,
             lru_b_a=lru_b_a, lru_w_x=lru_w_x, lru_b_x=lru_b_x, lru_lam=lru_lam, cv_conv_w=cv_conv_w,
             cv_conv_b=cv_conv_b, cv_ln_g=cv_ln_g, cv_ln_b=cv_ln_b, w_branch=w_branch, w_out=w_out,
             ffn2_norm=ffn2_norm, ffn2_w_gu=ffn2_w_gu, ffn2_w_down=ffn2_w_down)
    depth = w_in.shape[0]
    bp, seq, _ = x_prompt.shape
    bs, dseq, _ = x_sample.shape
    tp = seq + N_META
    fnorm = final_norm.reshape(1, D_MODEL)

    meta = jnp.broadcast_to(meta_tokens[None], (bp, N_META, D_MODEL))
    xp = jnp.concatenate([meta, x_prompt], axis=1)
    xs = x_sample.reshape(bs * dseq, D_MODEL)

    p_new, s_new = [], []
    for l in range(depth):
        lw = _layer_weights(l, p)
        final = l == depth - 1
        xp, st_p = _layer_long(xp, _zero_state(bp), lw, fnorm, final, tp, bp, tm=384, tb=129)
        st_s = (state_delta[l], state_delta_conv[l], state_s5_re[l], state_s5_im[l], state_lru[l],
                state_lru_conv[l], state_conv[l])
        xs, st_s = _layer_short(xs, st_s, lw, fnorm, final, dseq, bs, tm=512)
        p_new.append(st_p)
        s_new.append(st_s)

    y_prompt = xp[:, N_META:]
    y_sample = xs.reshape(bs, dseq, D_MODEL)
    stack = lambda new, i: jnp.stack([st[i] for st in new], axis=0)
    return (y_prompt, y_sample, *[stack(p_new, i) for i in range(7)], *[stack(s_new, i) for i in range(7)])
```

```python
import functools

import jax
import jax.numpy as jnp
from jax import lax
from jax.experimental import pallas as pl
from jax.experimental.pallas import tpu as pltpu

F32 = jnp.float32
BF16 = jnp.bfloat16

LANES = 128
D_MODEL = 1024
D_FF = 2816
N_META = 16
EPS = 1e-6
HEADS = 4
DK = 64
QKV_W = 768
CHUNK = 64
DELTA_BLOCK = 3
SOLVE_BLOCK = 16
SOLVE_PASSES = 1
S5_STATES = 1024
LRU_C = 8.0
CV_K = 31
LRU_K = 4
DN_K = 4

FF_CHUNK = 256
FF_NCHUNK = D_FF // FF_CHUNK
DELTA_COLS = 1536
BCD_COLS = 1280
BCD_COL_BLOCKS = (0, 1, 4)
MIX_COLS = DELTA_COLS + BCD_COLS

VMEM_LIMIT = 56 * 1024 * 1024


def _const_spec(shape):
    nd = len(shape)
    return pl.BlockSpec(shape, lambda *_: (0,) * nd, pipeline_mode=pl.Buffered(1))


def _bdot(a, b):
    return jnp.dot(a.astype(BF16), b.astype(BF16), preferred_element_type=F32)


def _hdot(a, b):
    return jnp.dot(a, b, precision=lax.Precision.HIGHEST, preferred_element_type=F32)


def _rms(x, g):
    return x * lax.rsqrt(jnp.mean(x * x, axis=-1, keepdims=True) + EPS) * g


def _silu(x):
    return x * jax.nn.sigmoid(x)


def _softplus(x):
    return jnp.maximum(x, 0.0) + jnp.log1p(jnp.exp(-jnp.abs(x)))


def _swiglu_residual(x, g_ref, wgu_ref, wd_ref):
    xn = _rms(x, g_ref[...]).astype(BF16)
    acc = jnp.zeros_like(x)
    for c in range(FF_NCHUNK):
        lo, hi = c * FF_CHUNK, (c + 1) * FF_CHUNK
        gate = jnp.dot(xn, wgu_ref[:, lo:hi], preferred_element_type=F32)
        up = jnp.dot(xn, wgu_ref[:, D_FF + lo:D_FF + hi], preferred_element_type=F32)
        h = (_silu(gate) * up).astype(BF16)
        acc = acc + jnp.dot(h, wd_ref[lo:hi, :], preferred_element_type=F32)
    return x + 0.5 * acc


def _stage_time_major_rows(src_ref, stage, bsz, row0=0):
    steps, ntile = src_ref.shape[1], src_ref.shape[2] // LANES
    for b in range(bsz):
        for j in range(ntile):
            stage[j, pl.ds(row0 + b, steps, stride=bsz), :] = src_ref[b, :, j * LANES:(j + 1) * LANES]


def _to_time_major_rows(src_ref, stage, bsz):
    _stage_time_major_rows(src_ref, stage, bsz)
    return jnp.concatenate([stage[j] for j in range(src_ref.shape[2] // LANES)], axis=1)


def _from_time_major_rows(stage, dst_ref, bsz, tiles=None):
    steps = dst_ref.shape[1]
    for j in (range(dst_ref.shape[2] // LANES) if tiles is None else tiles):
        for b in range(bsz):
            dst_ref[b, :, j * LANES:(j + 1) * LANES] = stage[j, pl.ds(b, steps, stride=bsz), :]


def _ffn_inproj(x, g1_ref, wgu_ref, wd_ref, gm_ref, wmix_ref, x1_ref):
    x1 = _swiglu_residual(x, g1_ref, wgu_ref, wd_ref)
    x1_ref[...] = x1
    u = _rms(x1, gm_ref[...]).astype(BF16)
    return lambda c: jnp.dot(u, wmix_ref[:, c * 256:(c + 1) * 256], preferred_element_type=F32)


def _pre_kernel(x_ref, g1_ref, wgu_ref, wd_ref, gm_ref, wmix_ref, x1_ref, zd_ref, zb_ref):
    z_block = _ffn_inproj(x_ref[...], g1_ref, wgu_ref, wd_ref, gm_ref, wmix_ref, x1_ref)
    for c in range(DELTA_COLS // 256):
        zd_ref[:, c * 256:(c + 1) * 256] = z_block(c)
    for c in range(BCD_COLS // 256):
        zb_ref[:, c * 256:(c + 1) * 256] = z_block(DELTA_COLS // 256 + c)


def _pre_tm_kernel(bsz, lead, nx, *refs):
    x_refs, refs = refs[:max(nx, 1)], refs[max(nx, 1):]
    if nx:
        prefix_ref, refs = refs[0], refs[1:]
    (g1_ref, wgu_ref, wd_ref, gm_ref, wmix_ref, cw_ref, buf_ref,
     x1_ref, zd_ref, zb_ref, nbuf_ref, stage, carry, *stage_x) = refs
    tm = x1_ref.shape[0]
    lb = (DN_K - 1) * bsz
    step = pl.program_id(0)

    @pl.when(step < lead)
    def _():
        zd_ref[...] = jnp.zeros(zd_ref.shape, F32)

    @pl.when(step == lead)
    def _():
        carry[...] = buf_ref[...]

    if nx:
        sub = tm // nx
        for j, ref in enumerate(x_refs):
            _stage_time_major_rows(ref, stage_x[0], bsz, j * sub)

        @pl.when(step == lead)
        def _():
            for b in range(bsz):
                for j in range(D_MODEL // LANES):
                    stage_x[0][j, pl.ds(b, sub // bsz, stride=bsz), :] = prefix_ref[:, j * LANES:(j + 1) * LANES]

    @pl.when(step >= lead)
    def _():
        if nx:
            x = jnp.concatenate([stage_x[0][j] for j in range(D_MODEL // LANES)], axis=1)
        else:
            x = x_refs[0][...]
        z_block = _ffn_inproj(x, g1_ref, wgu_ref, wd_ref, gm_ref, wmix_ref, x1_ref)
        for c in range(DELTA_COLS // 256):
            sl = slice(c * 256, (c + 1) * 256)
            zc = z_block(c)
            if c < QKV_W // 256:
                win = jnp.concatenate([carry[:, sl], zc], axis=0)
                y = cw_ref[0:1, sl] * win[0:tm]
                for k in range(1, DN_K):
                    y = y + cw_ref[k:k + 1, sl] * win[k * bsz:k * bsz + tm]
                carry[:, sl] = zc[tm - lb:, :]
                nbuf_ref[:, sl] = zc[tm - lb:, :]
                zc = _silu(y)
            stage[2 * c] = zc[:, 0:LANES]
            stage[2 * c + 1] = zc[:, LANES:2 * LANES]
        nb, nt = BCD_COLS // 256, DELTA_COLS // LANES
        for c in range(nb):
            zb_ref[:, c * 256:(c + 1) * 256] = z_block(DELTA_COLS // 256 + c)
            _from_time_major_rows(stage, zd_ref, bsz, range(c * nt // nb, (c + 1) * nt // nb))


def _pre_call(x, lw, tm, bsz=0, conv_buf=None, lead=0, prefix=None):
    tile = lambda i: jnp.maximum(i - lead, 0)
    row = lambda w: pl.BlockSpec((tm, w), lambda i: (tile(i), 0))
    if prefix is None:
        n, nx = x.shape[0], 0
        x_specs, x_args = [row(D_MODEL)], [x]
    else:
        npre, steps = prefix.shape[0], tm // bsz
        n = (x.shape[1] + npre) * bsz
        nx, nblk = steps // npre, x.shape[1] // npre
        assert nx * npre == steps and nblk * npre == x.shape[1]
        piece = lambda j: pl.BlockSpec((bsz, npre, D_MODEL),
                                       lambda i: (0, jnp.clip(tile(i) * nx + j - 1, 0, nblk - 1), 0))
        x_specs = [piece(j) for j in range(nx)] + [_const_spec((npre, D_MODEL))]
        x_args = [x] * nx + [prefix]
    in_specs = x_specs + [_const_spec((1, D_MODEL)), _const_spec((D_MODEL, 2 * D_FF)),
                          _const_spec((D_FF, D_MODEL)), _const_spec((1, D_MODEL)), _const_spec((D_MODEL, MIX_COLS))]
    args = x_args + [lw["ffn1_norm"], lw["ffn1_wgu"], lw["ffn1_wd"], lw["mix_norm"], lw["w_mix"]]
    out_specs = [row(D_MODEL), row(DELTA_COLS), row(BCD_COLS)]
    out_shape = [jax.ShapeDtypeStruct((n, D_MODEL), F32), jax.ShapeDtypeStruct((n, DELTA_COLS), F32),
                 jax.ShapeDtypeStruct((n, BCD_COLS), F32)]
    body, scratch = _pre_kernel, []
    if bsz:
        lb = (DN_K - 1) * bsz
        steps = tm // bsz
        body = functools.partial(_pre_tm_kernel, bsz, lead, nx)
        in_specs += [_const_spec((DN_K, QKV_W)), _const_spec((lb, QKV_W))]
        args += [lw["dn_conv_w"], conv_buf]
        out_specs[1] = pl.BlockSpec((bsz, steps, DELTA_COLS), lambda i: (0, i, 0))
        out_shape[1] = jax.ShapeDtypeStruct((bsz, lead * steps + n // bsz, DELTA_COLS), F32)
        out_specs.append(pl.BlockSpec((lb, QKV_W), lambda i: (0, 0)))
        out_shape.append(jax.ShapeDtypeStruct((lb, QKV_W), F32))
        scratch = [pltpu.VMEM((DELTA_COLS // LANES, tm, LANES), F32), pltpu.VMEM((lb, QKV_W), F32)]
        if nx:
            scratch.append(pltpu.VMEM((D_MODEL // LANES, tm, LANES), F32))
    return pl.pallas_call(
        body, grid=(lead + n // tm,), in_specs=in_specs, out_specs=out_specs, out_shape=out_shape,
        scratch_shapes=scratch,
        compiler_params=pltpu.CompilerParams(dimension_semantics=("arbitrary",), vmem_limit_bytes=VMEM_LIMIT),
        name="pre",
    )(*args)


def _post_kernel(final, bsz, out_bm, x1_ref, oa_ref, obcd_ref, gm_ref, wgate_ref, wbr_ref, wout_ref,
                 g2_ref, wgu_ref, wd_ref, gf_ref, out_ref, *scratch):
    x1 = x1_ref[...]
    oa = _to_time_major_rows(oa_ref, scratch[0], bsz) if bsz else oa_ref[...]
    u = _rms(x1, gm_ref[...]).astype(BF16)
    m = jnp.zeros_like(x1)
    for i in range(4):
        gates = jax.nn.sigmoid(jnp.dot(u, wgate_ref[:, i * D_MODEL:(i + 1) * D_MODEL],
                                       preferred_element_type=F32))
        br = oa if i == 0 else obcd_ref[:, (i - 1) * 256:i * 256]
        m = m + gates * _bdot(br, wbr_ref[i])
    x2 = x1 + _bdot(m, wout_ref[...])
    x3 = _swiglu_residual(x2, g2_ref, wgu_ref, wd_ref)
    if final:
        x3 = _rms(x3, gf_ref[...])
    if out_bm:
        for j in range(D_MODEL // LANES):
            scratch[1][j] = x3[:, j * LANES:(j + 1) * LANES]
        _from_time_major_rows(scratch[1], out_ref, bsz)
    else:
        out_ref[...] = x3


def _post_call(x1, oa, obcd, lw, final_norm, final, tm, bsz=0, lead=0, out_bm=False):
    n = x1.shape[0]
    row = lambda w: pl.BlockSpec((tm, w), lambda i: (i, 0))
    oa_spec = pl.BlockSpec((bsz, tm // bsz, 256), lambda i: (0, i + lead, 0)) if bsz else row(256)
    scratch = [pltpu.VMEM((256 // LANES, tm, LANES), F32)] if bsz else []
    out_spec, out_shape = row(D_MODEL), jax.ShapeDtypeStruct((n, D_MODEL), F32)
    if out_bm:
        scratch.append(pltpu.VMEM((D_MODEL // LANES, tm, LANES), F32))
        out_spec = pl.BlockSpec((bsz, tm // bsz, D_MODEL), lambda i: (0, i, 0))
        out_shape = jax.ShapeDtypeStruct((bsz, n // bsz, D_MODEL), F32)
    return pl.pallas_call(
        functools.partial(_post_kernel, final, bsz, out_bm),
        grid=(n // tm,),
        scratch_shapes=scratch,
        in_specs=[row(D_MODEL), oa_spec, row(768), _const_spec((1, D_MODEL)),
                  _const_spec((D_MODEL, 4 * D_MODEL)), _const_spec((4, 256, D_MODEL)),
                  _const_spec((D_MODEL, D_MODEL)), _const_spec((1, D_MODEL)),
                  _const_spec((D_MODEL, 2 * D_FF)), _const_spec((D_FF, D_MODEL)), _const_spec((1, D_MODEL))],
        out_specs=out_spec,
        out_shape=out_shape,
        compiler_params=pltpu.CompilerParams(dimension_semantics=("arbitrary",), vmem_limit_bytes=VMEM_LIMIT),
        name="post",
    )(x1, oa, obcd, lw["mix_norm"], lw["w_gate"], lw["w_branch"], lw["w_out"],
      lw["ffn2_norm"], lw["ffn2_wgu"], lw["ffn2_wd"], final_norm)


def _split16(x):
    hi = x.astype(BF16)
    if SOLVE_PASSES == 1:
        return (hi,)
    return hi, (x - hi.astype(F32)).astype(BF16)


def _bmm(a, b, contract=(2, 1)):
    dims = (((contract[0],), (contract[1],)), ((0,), (0,)))
    return lax.dot_general(a, b, dims, preferred_element_type=F32)


def _dot3(a, b):
    out = _bmm(a[0], b[0])
    if SOLVE_PASSES >= 2:
        out = out + _bmm(a[1], b[0])
    if SOLVE_PASSES >= 3:
        out = out + _bmm(a[0], b[1])
    return out


def _chunk_masks(seq_len):
    ri = lax.broadcasted_iota(jnp.int32, (CHUNK, CHUNK), 0)
    ci = lax.broadcasted_iota(jnp.int32, (CHUNK, CHUNK), 1)
    causal = ri >= ci
    strict = ri > ci
    if seq_len < CHUNK:
        same = (ri // seq_len) == (ci // seq_len)
        causal = causal & same
        strict = strict & same
    return dict(causal_f=causal.astype(F32), strict_f=strict.astype(F32), eye_f=(ri == ci).astype(F32),
                blk_f=((ri // SOLVE_BLOCK) == (ci // SOLVE_BLOCK)).astype(F32), rowseq=ri // seq_len,
                rowpos=lax.broadcasted_iota(jnp.int32, (CHUNK, 256), 0) % seq_len)


def _wy_solve(low, rhs, mk, seq_len):
    eye_f = mk["eye_f"]
    if seq_len > SOLVE_BLOCK:
        nd = low * mk["blk_f"]
        off = low - nd
        blk = SOLVE_BLOCK
    else:
        nd, off, blk = low, None, seq_len
    p = eye_f - nd
    ns = _split16(nd)
    for _ in range(blk.bit_length() - 2):
        ns = _split16(_dot3(ns, ns))
        p = p + _dot3(_split16(p), ns)
    ps = _split16(p)
    y = _dot3(ps, _split16(rhs))
    if off is None:
        return y
    assert seq_len // SOLVE_BLOCK == 4
    ms = _split16(_dot3(ps, _split16(off)))
    y2 = y + _dot3(_split16(_dot3(ms, ms)), _split16(y))
    return y2 - _dot3(ms, _split16(y2))


def _delta_elementwise(chunks, mk, seq_len):
    qs, ks, vs, bs, gcs = [], [], [], [], []
    for q_in, k_in, v_in, beta, gl in chunks:
        gc = gl
        shift = 1
        while shift < seq_len:
            gc = gc + jnp.where(mk["rowpos"] >= shift, pltpu.roll(gc, shift, axis=0), 0.0)
            shift *= 2
        for h in range(HEADS):
            sl = slice(h * DK, (h + 1) * DK)
            qs.append(q_in[:, sl])
            ks.append(k_in[:, sl])
            vs.append(v_in[:, sl])
            bs.append(beta[:, sl])
            gcs.append(gc[:, sl])
    q, k, v, b, gc = [jnp.stack(a, axis=0) for a in (qs, ks, vs, bs, gcs)]
    q = q * lax.rsqrt(jnp.sum(q * q, axis=-1, keepdims=True) + EPS) * (DK ** -0.5)
    k = k * lax.rsqrt(jnp.sum(k * k, axis=-1, keepdims=True) + EPS)
    gct = jnp.sum(gc * mk["eye_f"], axis=1, keepdims=True)
    dec = jnp.exp(jnp.minimum(gc - gct, 0.0))
    eg = jnp.exp(gc)
    kb = k * b
    if seq_len == CHUNK:
        glast = gc[:, CHUNK - 1:CHUNK, :]
    else:
        glast = jnp.concatenate(
            [jnp.broadcast_to(gc[:, (j + 1) * seq_len - 1:(j + 1) * seq_len, :], (gc.shape[0], seq_len, DK))
             for j in range(CHUNK // seq_len)], axis=1)
    return dict(kb16=kb.astype(BF16), k16=k.astype(BF16), q16=q.astype(BF16), dec=dec,
                rhs=jnp.concatenate([v * b, kb * eg], axis=2), qe16=(q * eg).astype(BF16),
                kd16=(k * jnp.exp(glast - gc)).astype(BF16), egl=jnp.exp(glast))


def _delta_solve_store(ew, mk, seq_len, scr, slots):
    u_s, w_s, qe_s, kd_s, a_s, egl_s = scr
    kk = _bmm(ew["kb16"], ew["k16"], (2, 2))
    qk = _bmm(ew["q16"], ew["k16"], (2, 2))
    low = kk * (ew["dec"] * mk["strict_f"])
    x = _wy_solve(low, ew["rhs"], mk, seq_len)
    w16 = x[:, :, DK:2 * DK].astype(BF16)
    a16 = (qk * (ew["dec"] * mk["causal_f"])).astype(BF16)
    qe16, kd16, egl = ew["qe16"], ew["kd16"], ew["egl"]
    for c, (lead, slot) in enumerate(slots):
        dst = (pl.ds(lead, HEADS), pl.ds(slot * CHUNK, CHUNK), slice(None))
        ps = slice(c * HEADS, (c + 1) * HEADS)
        u_s[dst] = x[ps, :, 0:DK]
        w_s[dst] = w16[ps]
        qe_s[dst] = qe16[ps]
        kd_s[dst] = kd16[ps]
        a_s[dst] = a16[ps]
        egl_s[dst] = jnp.broadcast_to(egl[ps], (HEADS, CHUNK, DK))


def _gated_norm(o, dz, ng):
    return o * lax.rsqrt(jnp.mean(o * o, axis=-1, keepdims=True) + EPS) * ng * _silu(dz)


def _delta_scratch(lead, nslots):
    n = nslots * CHUNK
    return ([pltpu.VMEM((lead, n, DK), F32)] + [pltpu.VMEM((lead, n, DK), BF16)] * 4
            + [pltpu.VMEM((lead, n, DK), F32)])


def _delta_kernel(bsz, nchunk, q_ref, k_ref, v_ref, dz_ref, be_ref, ae_ref, s0_ref, alog_ref, dtb_ref, ng_ref,
                  o_ref, s_ref, *scr):
    @pl.when(pl.program_id(0) == 0)
    def _():
        s_ref[...] = s0_ref[...]

    u_s, w_s, qe_s, kd_s, a_s, egl_s = scr
    mk = _chunk_masks(CHUNK)
    neg_a = -jnp.exp(alog_ref[...])
    dtb = dtb_ref[...]
    ng = ng_ref[...]

    def elementwise(c):
        rows = pl.ds(c * CHUNK, CHUNK)
        chunks = []
        for b in range(bsz):
            beta = jax.nn.sigmoid(be_ref[b, rows, :])
            gl = neg_a * _softplus(ae_ref[b, rows, :] + dtb)
            chunks.append((q_ref[b, rows, :], k_ref[b, rows, :], v_ref[b, rows, :], beta, gl))
        return _delta_elementwise(chunks, mk, CHUNK)

    ew = elementwise(0)
    for c in range(nchunk):
        ew_next = elementwise(c + 1) if c + 1 < nchunk else None
        _delta_solve_store(ew, mk, CHUNK, scr, [(b * HEADS, c) for b in range(bsz)])
        ew = ew_next

    def step(c, carry):
        rows = pl.ds(pl.multiple_of(c * CHUNK, CHUNK), CHUNK)
        s = s_ref[...]
        s16 = s.astype(BF16)
        vnew = u_s[:, rows, :] - _bmm(w_s[:, rows, :], s16)
        vn16 = vnew.astype(BF16)
        o = _bmm(qe_s[:, rows, :], s16) + _bmm(a_s[:, rows, :], vn16)
        s_ref[...] = (s * egl_s[:, pl.ds(pl.multiple_of(c * CHUNK, CHUNK), 8), :][:, 0:1, :]
                      + _bmm(kd_s[:, rows, :], vn16, (1, 1)))
        for b in range(bsz):
            for h in range(HEADS):
                sl = slice(h * DK, (h + 1) * DK)
                o_ref[b, rows, sl] = _gated_norm(o[b * HEADS + h], dz_ref[b, rows, sl], ng)
        return carry
    lax.fori_loop(0, nchunk, step, 0)


def _delta_call(zd, s0, lw, nchunk=DELTA_BLOCK):
    bsz, t_len, _ = zd.shape
    tb = nchunk * CHUNK
    assert t_len % tb == 0
    zspec = lambda blk: pl.BlockSpec((bsz, tb, 256), lambda i: (0, i, blk))
    sspec = pl.BlockSpec((bsz * HEADS, DK, DK), lambda i: (0, 0, 0))
    o, s = pl.pallas_call(
        functools.partial(_delta_kernel, bsz, nchunk),
        grid=(t_len // tb,),
        in_specs=[zspec(0), zspec(1), zspec(2), zspec(3), zspec(4), zspec(5), sspec,
                  _const_spec((1, 256)), _const_spec((1, 256)), _const_spec((1, DK))],
        out_specs=[pl.BlockSpec((bsz, tb, 256), lambda i: (0, i, 0)), sspec],
        out_shape=[jax.ShapeDtypeStruct((bsz, t_len, 256), F32),
                   jax.ShapeDtypeStruct((bsz * HEADS, DK, DK), F32)],
        scratch_shapes=_delta_scratch(bsz * HEADS, nchunk),
        compiler_params=pltpu.CompilerParams(dimension_semantics=("arbitrary",), vmem_limit_bytes=VMEM_LIMIT),
        name="delta",
    )(zd, zd, zd, zd, zd, zd, s0.reshape(bsz * HEADS, DK, DK), lw["dn_a_log"], lw["dn_dt_bias"], lw["dn_norm"])
    return o, s.reshape(bsz, HEADS, DK, DK)


def _delta_short_kernel(seq_len, q_ref, k_ref, v_ref, dz_ref, be_ref, ae_ref, buf_ref, s0_ref, cw_ref,
                        alog_ref, dtb_ref, ng_ref, o_ref, nbuf_ref, sfin_ref, xs_ref, *scr):
    nseq = CHUNK // seq_len
    span = seq_len + 8
    mk = _chunk_masks(seq_len)
    cw = cw_ref[...]
    xs_ref[...] = jnp.zeros(xs_ref.shape, F32)
    for j in range(nseq):
        xs_ref[j * span + 5:j * span + 8, :] = buf_ref[j]
        rows = slice(j * seq_len, (j + 1) * seq_len)
        xs_ref[j * span + 8:(j + 1) * span, 0:256] = q_ref[rows, :]
        xs_ref[j * span + 8:(j + 1) * span, 256:512] = k_ref[rows, :]
        xs_ref[j * span + 8:(j + 1) * span, 512:768] = v_ref[rows, :]
    ys = []
    for j in range(nseq):
        win = xs_ref[j * span:(j + 1) * span, :]
        ys.append(cw[3:4] * win[8:span] + cw[2:3] * win[7:span - 1]
                  + cw[1:2] * win[6:span - 2] + cw[0:1] * win[5:span - 3])
        nbuf_ref[j] = win[span - 3:span, :]
    qkv = _silu(jnp.concatenate(ys, axis=0))
    beta = jax.nn.sigmoid(be_ref[...])
    gl = -jnp.exp(alog_ref[...]) * _softplus(ae_ref[...] + dtb_ref[...])
    ew = _delta_elementwise([(qkv[:, 0:256], qkv[:, 256:512], qkv[:, 512:768], beta, gl)], mk, seq_len)
    _delta_solve_store(ew, mk, seq_len, scr, [(0, 0)])

    u_s, w_s, qe_s, kd_s, a_s, egl_s = scr
    w = w_s[...]
    qe = qe_s[...]
    kd = kd_s[...].astype(F32)
    u = u_s[...]
    ws, qs = [], []
    for j in range(nseq):
        s16 = s0_ref[j].astype(BF16)
        ws.append(_bmm(w, s16))
        qs.append(_bmm(qe, s16))
    vnew, o = u, jnp.zeros_like(u)
    for j in range(nseq):
        mine = mk["rowseq"] == j
        vnew = jnp.where(mine, u - ws[j], vnew)
        o = jnp.where(mine, qs[j], o)
    vn16 = vnew.astype(BF16)
    o = o + _bmm(a_s[...], vn16)
    for j in range(nseq):
        kdj = jnp.where(mk["rowseq"] == j, kd, 0.0).astype(BF16)
        sfin_ref[j] = (s0_ref[j] * egl_s[:, j * seq_len:(j + 1) * seq_len, :][:, 0:1, :]
                       + _bmm(kdj, vn16, (1, 1)))
    ng = ng_ref[...]
    for h in range(HEADS):
        sl = slice(h * DK, (h + 1) * DK)
        o_ref[:, sl] = _gated_norm(o[h], dz_ref[:, sl], ng)


def _delta_short_call(z, seq_len, buf, s0, lw):
    n = z.shape[0]
    nseq = CHUNK // seq_len
    zspec = lambda blk: pl.BlockSpec((CHUNK, 256), lambda g: (g, blk))
    bspec = pl.BlockSpec((nseq, DN_K - 1, QKV_W), lambda g: (g, 0, 0))
    sspec = pl.BlockSpec((nseq, HEADS, DK, DK), lambda g: (g, 0, 0, 0))
    return pl.pallas_call(
        functools.partial(_delta_short_kernel, seq_len),
        grid=(n // CHUNK,),
        in_specs=[zspec(0), zspec(1), zspec(2), zspec(3), zspec(4), zspec(5), bspec, sspec, _const_spec((DN_K, QKV_W)), _const_spec((1, 256)), _const_spec((1, 256)),
                  _const_spec((1, DK))],
        out_specs=[pl.BlockSpec((CHUNK, 256), lambda g: (g, 0)), bspec, sspec],
        out_shape=[jax.ShapeDtypeStruct((n, 256), F32), jax.ShapeDtypeStruct(buf.shape, F32),
                   jax.ShapeDtypeStruct(s0.shape, F32)],
        scratch_shapes=[pltpu.VMEM((nseq * (seq_len + 8), QKV_W), F32)] + _delta_scratch(HEADS, 1),
        compiler_params=pltpu.CompilerParams(dimension_semantics=("arbitrary",), vmem_limit_bytes=VMEM_LIMIT),
        name="delta_short",
    )(z, z, z, z, z, z, buf, s0, lw["dn_conv_w"], lw["dn_a_log"], lw["dn_dt_bias"], lw["dn_norm"])


def _scan_time_major(t_len, bsz, state_refs, step):
    def run_group(goff):
        hs = tuple(r[pl.ds(goff, 8), :] for r in state_refs)
        if t_len <= 8:
            for t in range(t_len):
                hs = step(hs, t * bsz + goff)
        else:
            def body(t, hs):
                return step(hs, pl.multiple_of(t * bsz + goff, 8))
            hs = lax.fori_loop(0, t_len, body, hs, unroll=3)
        for r, h in zip(state_refs, hs):
            r[pl.ds(goff, 8), :] = h

    if bsz == 8:
        run_group(0)
    else:
        def gbody(g, carry):
            run_group(pl.multiple_of(g * 8, 8))
            return carry
        lax.fori_loop(0, bsz // 8, gbody, 0)


def _bcd_kernel(t_len, bsz, nsteps,
                lru_ref, cv_ref, s5_ref, s5re0, s5im0, lru0, lbuf0, cbuf0,
                lam_re_ref, lam_im_ref, lstep_ref, wb_ref, wcre_ref, wcim_ref, dskip_ref, wglu_ref, bglu_ref,
                lcw_ref, lcb_ref, wa_ref, ba_ref, wx_ref, bx_ref, llam_ref,
                ccw_ref, ccb_ref, lng_ref, lnb_ref,
                o_ref, s5re_o, s5im_o, lru_o, lbuf_o, cbuf_o,
                xr, xi, lxs, cxs, a_s, b_s, wbp):
    rows = t_len * bsz
    lb = (LRU_K - 1) * bsz
    cb = (CV_K - 1) * bsz

    @pl.when(pl.program_id(0) == 0)
    def _():
        s5re_o[...] = s5re0[...]
        s5im_o[...] = s5im0[...]
        lru_o[...] = lru0[...]
        lxs[0:lb, :] = lbuf0[...]
        cxs[0:cb, :] = cbuf0[...]

    lxs[lb:lb + rows, :] = lru_ref[:, 0:256]
    xf = lcb_ref[...] + lcw_ref[0:1, :] * lxs[0:rows, :]
    for k in range(1, LRU_K):
        xf = xf + lcw_ref[k:k + 1, :] * lxs[k * bsz:k * bsz + rows, :]
    r = jax.nn.sigmoid(_bdot(xf, wa_ref[...]) + ba_ref[...])
    i = jax.nn.sigmoid(_bdot(xf, wx_ref[...]) + bx_ref[...])
    log_a = (-LRU_C) * r * _softplus(-llam_ref[...])
    a_s[...] = jnp.exp(log_a)
    b_s[...] = jnp.sqrt(1.0 - jnp.exp(2.0 * log_a)) * (i * xf)

    def lru_step(hs, row):
        h = a_s[pl.ds(row, 8), :] * hs[0] + b_s[pl.ds(row, 8), :]
        b_s[pl.ds(row, 8), :] = h
        return (h,)
    _scan_time_major(t_len, bsz, (lru_o,), lru_step)
    o_ref[:, 256:512] = b_s[...] * jax.nn.gelu(lru_ref[:, 256:512])
    lbuf_o[...] = lxs[rows:rows + lb, :]
    if nsteps > 1:
        lxs[0:lb, :] = lxs[rows:rows + lb, :]

    dt = jnp.exp(lstep_ref[...])
    lam_re = lam_re_ref[...]
    lam_im = lam_im_ref[...]
    mag = jnp.exp(lam_re * dt)
    lb_re = mag * jnp.cos(lam_im * dt)
    lb_im = mag * jnp.sin(lam_im * dt)
    den = lam_re * lam_re + lam_im * lam_im
    cf_re = ((lb_re - 1.0) * lam_re + lb_im * lam_im) / den
    cf_im = (lb_im * lam_re - (lb_re - 1.0) * lam_im) / den

    @pl.when(pl.program_id(0) == 0)
    def _():
        b_re = wb_ref[:, 0:S5_STATES]
        b_im = wb_ref[:, S5_STATES:2 * S5_STATES]
        wbp[:, 0:S5_STATES] = (cf_re * b_re - cf_im * b_im).astype(BF16)
        wbp[:, S5_STATES:2 * S5_STATES] = (cf_re * b_im + cf_im * b_re).astype(BF16)

    u16 = s5_ref[...].astype(BF16)
    xr[...] = jnp.dot(u16, wbp[:, 0:S5_STATES], preferred_element_type=F32)
    xi[...] = jnp.dot(u16, wbp[:, S5_STATES:2 * S5_STATES], preferred_element_type=F32)
    lbr = jnp.broadcast_to(lb_re, (8, S5_STATES))
    lbi = jnp.broadcast_to(lb_im, (8, S5_STATES))

    def s5_step(hs, row):
        hr, hi = hs
        nr = lbr * hr - lbi * hi + xr[pl.ds(row, 8), :]
        ni = lbr * hi + lbi * hr + xi[pl.ds(row, 8), :]
        xr[pl.ds(row, 8), :] = nr
        xi[pl.ds(row, 8), :] = ni
        return (nr, ni)
    _scan_time_major(t_len, bsz, (s5re_o, s5im_o), s5_step)
    y = (_bdot(xr[...], wcre_ref[...]) - _bdot(xi[...], wcim_ref[...])
         + dskip_ref[...] * s5_ref[...])
    y = jax.nn.gelu(y)
    glu = _bdot(y, wglu_ref[...]) + bglu_ref[...]
    o_ref[:, 0:256] = glu[:, 0:256] * jax.nn.sigmoid(glu[:, 256:512])

    cxs[cb:cb + rows, :] = cv_ref[:, 0:256] * jax.nn.sigmoid(cv_ref[:, 256:512])
    yc = ccb_ref[...] + ccw_ref[0:1, :] * cxs[0:rows, :]
    for k in range(1, CV_K):
        yc = yc + ccw_ref[k:k + 1, :] * cxs[k * bsz:k * bsz + rows, :]
    mu = jnp.mean(yc, axis=-1, keepdims=True)
    ycc = yc - mu
    yn = ycc * lax.rsqrt(jnp.mean(ycc * ycc, axis=-1, keepdims=True) + EPS) * lng_ref[...] + lnb_ref[...]
    o_ref[:, 512:768] = _silu(yn)
    cbuf_o[...] = cxs[rows:rows + cb, :]
    if nsteps > 1:
        cxs[0:cb, :] = cxs[rows:rows + cb, :]


def _bcd_call(zmix, states, lw, t_len, bsz, tb):
    col_blocks = BCD_COL_BLOCKS
    n = zmix.shape[0]
    nsteps = t_len // tb
    rows = tb * bsz
    lb = (LRU_K - 1) * bsz
    cb = (CV_K - 1) * bsz
    assert nsteps == 1 or rows >= cb
    zspec = lambda w, blk: pl.BlockSpec((rows, w), lambda i: (i, blk))
    state_shapes = [(bsz, S5_STATES), (bsz, S5_STATES), (bsz, 256), (lb, 256), (cb, 256)]
    params = [lw["s5_lam_re"], lw["s5_lam_im"], lw["s5_log_step"], lw["s5_wb"], lw["s5_wcre"], lw["s5_wcim"],
              lw["s5_d"], lw["s5_w_glu"], lw["s5_b_glu"],
              lw["lru_conv_w"], lw["lru_conv_b"], lw["lru_wa"], lw["lru_b_a"], lw["lru_wx"], lw["lru_b_x"],
              lw["lru_lam"], lw["cv_conv_w"], lw["cv_conv_b"], lw["cv_ln_g"], lw["cv_ln_b"]]
    return pl.pallas_call(
        functools.partial(_bcd_kernel, tb, bsz, nsteps),
        grid=(nsteps,),
        in_specs=([zspec(512, col_blocks[0]), zspec(512, col_blocks[1]), zspec(256, col_blocks[2])]
                  + [_const_spec(s) for s in state_shapes]
                  + [_const_spec(p.shape) for p in params]),
        out_specs=[pl.BlockSpec((rows, 768), lambda i: (i, 0))]
                  + [pl.BlockSpec(s, lambda i: (0, 0)) for s in state_shapes],
        out_shape=[jax.ShapeDtypeStruct((n, 768), F32)]
                  + [jax.ShapeDtypeStruct(s, F32) for s in state_shapes],
        scratch_shapes=[pltpu.VMEM((rows, S5_STATES), F32), pltpu.VMEM((rows, S5_STATES), F32),
                        pltpu.VMEM((lb + rows, 256), F32), pltpu.VMEM((cb + rows, 256), F32),
                        pltpu.VMEM((rows, 256), F32), pltpu.VMEM((rows, 256), F32),
                        pltpu.VMEM((256, 2 * S5_STATES), BF16)],
        compiler_params=pltpu.CompilerParams(dimension_semantics=("arbitrary",), vmem_limit_bytes=VMEM_LIMIT),
        name="bcd",
    )(zmix, zmix, zmix, *states, *params)


def _block_diag(m):
    g, r, c = m.shape
    return (jnp.eye(g, dtype=m.dtype)[:, None, :, None] * m[:, :, None, :]).reshape(g * r, g * c)


def _layer_weights(l, p):
    row = lambda v: v.reshape(1, -1).astype(F32)
    w_in = p["w_in"][l]
    w_mix = jnp.concatenate(
        [w_in[:, 0:1024], jnp.repeat(w_in[:, 1024:1028], DK, axis=1), jnp.repeat(w_in[:, 1028:1032], DK, axis=1),
         w_in[:, 1288:1800], w_in[:, 1800:2312], w_in[:, 1032:1288]], axis=1)
    lw = {}
    lw["ffn1_wgu"], lw["ffn1_wd"] = p["ffn1_w_gu"][l].astype(BF16), p["ffn1_w_down"][l].astype(BF16)
    lw["ffn2_wgu"], lw["ffn2_wd"] = p["ffn2_w_gu"][l].astype(BF16), p["ffn2_w_down"][l].astype(BF16)
    lw["ffn1_norm"] = row(p["ffn1_norm"][l])
    lw["ffn2_norm"] = row(p["ffn2_norm"][l])
    lw["mix_norm"] = row(p["mix_norm"][l])
    lw["w_mix"] = w_mix.astype(BF16)
    lw["w_gate"] = w_in[:, 2312:].astype(BF16)
    lw["w_branch"] = p["w_branch"][l].astype(BF16)
    lw["w_out"] = p["w_out"][l].astype(BF16)
    lw["dn_conv_w"] = p["dn_conv_w"][l]
    lw["dn_a_log"] = row(jnp.repeat(p["dn_a_log"][l], DK))
    lw["dn_dt_bias"] = row(jnp.repeat(p["dn_dt_bias"][l], DK))
    lw["dn_norm"] = row(p["dn_norm"][l])
    lw["s5_lam_re"] = row(p["s5_lam_re"][l])
    lw["s5_lam_im"] = row(p["s5_lam_im"][l])
    lw["s5_log_step"] = row(jnp.repeat(p["s5_log_step"][l], 64))
    bdt = lambda w: _block_diag(jnp.swapaxes(w, 1, 2))
    lw["s5_wb"] = jnp.concatenate([bdt(p["s5_b_re"][l]), bdt(p["s5_b_im"][l])], axis=1).astype(F32)
    lw["s5_wcre"] = bdt(p["s5_c_re"][l]).astype(BF16)
    lw["s5_wcim"] = bdt(p["s5_c_im"][l]).astype(BF16)
    lw["s5_d"] = row(p["s5_d"][l])
    lw["s5_w_glu"] = p["s5_w_glu"][l].astype(BF16)
    lw["s5_b_glu"] = row(p["s5_b_glu"][l])
    lw["lru_conv_w"] = p["lru_conv_w"][l]
    lw["lru_conv_b"] = row(p["lru_conv_b"][l])
    lw["lru_wa"] = _block_diag(p["lru_w_a"][l]).astype(BF16)
    lw["lru_wx"] = _block_diag(p["lru_w_x"][l]).astype(BF16)
    lw["lru_b_a"] = row(p["lru_b_a"][l])
    lw["lru_b_x"] = row(p["lru_b_x"][l])
    lw["lru_lam"] = row(p["lru_lam"][l])
    lw["cv_conv_w"] = p["cv_conv_w"][l]
    lw["cv_conv_b"] = row(p["cv_conv_b"][l])
    lw["cv_ln_g"] = row(p["cv_ln_g"][l])
    lw["cv_ln_b"] = row(p["cv_ln_b"][l])
    return lw


def _to_time_major(a):
    bsz, k, c = a.shape
    return jnp.transpose(a, (1, 0, 2)).reshape(k * bsz, c)


def _from_time_major(a, bsz):
    k = a.shape[0] // bsz
    return jnp.transpose(a.reshape(k, bsz, a.shape[1]), (1, 0, 2))


def _bcd_states(st, bsz):
    _, _, s_re, s_im, s_lru, s_lruc, s_cv = st
    return (s_re.reshape(bsz, S5_STATES), s_im.reshape(bsz, S5_STATES), s_lru,
            _to_time_major(s_lruc), _to_time_major(s_cv))


def _new_states(n_dn, n_dnc, bcd_new, bsz):
    n_re, n_im, n_lru, n_lruc, n_cv = bcd_new
    return (n_dn, n_dnc, n_re.reshape(bsz, 16, 64), n_im.reshape(bsz, 16, 64), n_lru,
            _from_time_major(n_lruc, bsz), _from_time_major(n_cv, bsz))


def _layer_long(x, st, lw, final_norm, final, t_len, bsz, tm, tb, prefix=None):
    pad = (-t_len) % (DELTA_BLOCK * CHUNK)
    lead, rem = divmod(pad * bsz, tm)
    assert rem == 0
    x1, zd, zb, n_dnc = _pre_call(x, lw, tm, bsz, _to_time_major(st[1]), lead, prefix)
    n_dnc = _from_time_major(n_dnc, bsz)
    oa, n_dn = _delta_call(zd, st[0], lw)
    obcd, *bcd_new = _bcd_call(zb, _bcd_states(st, bsz), lw, t_len, bsz, tb)
    x3 = _post_call(x1, oa, obcd, lw, final_norm, final, tm, bsz, lead, out_bm=final)
    return x3, _new_states(n_dn, n_dnc, bcd_new, bsz)


def _layer_short(x, st, lw, final_norm, final, t_len, bsz, tm):
    x1, zd, zb = _pre_call(x, lw, tm)
    oa, n_dnc, n_dn = _delta_short_call(zd, t_len, st[1], st[0], lw)
    z_tm = _to_time_major(zb.reshape(bsz, t_len, BCD_COLS))
    obcd_tm, *bcd_new = _bcd_call(z_tm, _bcd_states(st, bsz), lw, t_len, bsz, t_len)
    obcd = _from_time_major(obcd_tm, bsz).reshape(bsz * t_len, 768)
    x3 = _post_call(x1, oa, obcd, lw, final_norm, final, tm)
    return x3, _new_states(n_dn, n_dnc, bcd_new, bsz)


def _zero_state(bsz):
    return (jnp.zeros((bsz, HEADS, DK, DK), F32), jnp.zeros((bsz, DN_K - 1, QKV_W), F32),
            jnp.zeros((bsz, 16, 64), F32), jnp.zeros((bsz, 16, 64), F32), jnp.zeros((bsz, 256), F32),
            jnp.zeros((bsz, LRU_K - 1, 256), F32), jnp.zeros((bsz, CV_K - 1, 256), F32))


def kernel(x_prompt, x_sample, state_delta, state_delta_conv, state_s5_re, state_s5_im, state_lru, state_lru_conv, state_conv, meta_tokens, ffn1_norm, ffn1_w_gu, ffn1_w_down, mix_norm, w_in, dn_conv_w, dn_a_log, dn_dt_bias, dn_norm, s5_lam_re, s5_lam_im, s5_log_step, s5_b_re, s5_b_im, s5_c_re, s5_c_im, s5_d, s5_w_glu, s5_b_glu, lru_conv_w, lru_conv_b, lru_w_a, lru_b_a, lru_w_x, lru_b_x, lru_lam, cv_conv_w, cv_conv_b, cv_ln_g, cv_ln_b, w_branch, w_out, ffn2_norm, ffn2_w_gu, ffn2_w_down, final_norm):
    p = dict(ffn1_norm=ffn1_norm, ffn1_w_gu=ffn1_w_gu, ffn1_w_down=ffn1_w_down, mix_norm=mix_norm, w_in=w_in,
             dn_conv_w=dn_conv_w, dn_a_log=dn_a_log, dn_dt_bias=dn_dt_bias, dn_norm=dn_norm,
             s5_lam_re=s5_lam_re, s5_lam_im=s5_lam_im, s5_log_step=s5_log_step, s5_b_re=s5_b_re,
             s5_b_im=s5_b_im, s5_c_re=s5_c_re, s5_c_im=s5_c_im, s5_d=s5_d, s5_w_glu=s5_w_glu,
             s5_b_glu=s5_b_glu, lru_conv_w=lru_conv_w, lru_conv_b=lru_conv_b, lru_w_a=lru_w_a,
             lru_b_a=lru_b_a, lru_w_x=lru_w_x, lru_b_x=lru_b_x, lru_lam=lru_lam, cv_conv_w=cv_conv_w,
             cv_conv_b=cv_conv_b, cv_ln_g=cv_ln_g, cv_ln_b=cv_ln_b, w_branch=w_branch, w_out=w_out,
             ffn2_norm=ffn2_norm, ffn2_w_gu=ffn2_w_gu, ffn2_w_down=ffn2_w_down)
    depth = w_in.shape[0]
    bp, seq, _ = x_prompt.shape
    bs, dseq, _ = x_sample.shape
    tp = seq + N_META
    fnorm = final_norm.reshape(1, D_MODEL)

    xp = x_prompt
    xs = x_sample.reshape(bs * dseq, D_MODEL)

    p_new, s_new = [], []
    for l in range(depth):
        lw = _layer_weights(l, p)
        final = l == depth - 1
        xp, st_p = _layer_long(xp, _zero_state(bp), lw, fnorm, final, tp, bp, tm=384, tb=129,
                               prefix=meta_tokens if l == 0 else None)
        st_s = (state_delta[l], state_delta_conv[l], state_s5_re[l], state_s5_im[l], state_lru[l],
                state_lru_conv[l], state_conv[l])
        xs, st_s = _layer_short(xs, st_s, lw, fnorm, final, dseq, bs, tm=512)
        p_new.append(st_p)
        s_new.append(st_s)

    y_prompt = xp[:, N_META:]
    y_sample = xs.reshape(bs, dseq, D_MODEL)
    stack = lambda new, i: jnp.stack([st[i] for st in new], axis=0)
    return (y_prompt, y_sample, *[stack(p_new, i) for i in range(7)], *[stack(s_new, i) for i in range(7)])
```

```python
import functools

import jax
import jax.numpy as jnp
from jax import lax
from jax.experimental import pallas as pl
from jax.experimental.pallas import tpu as pltpu

F32 = jnp.float32
BF16 = jnp.bfloat16

LANES = 128
D_MODEL = 1024
D_FF = 2816
N_META = 16
EPS = 1e-6
HEADS = 4
DK = 64
QKV_W = 768
CHUNK = 64
DELTA_BLOCK = 3
SOLVE_BLOCK = 16
S5_STATES = 1024
LRU_C = 8.0
CV_K = 31
LRU_K = 4
DN_K = 4

FF_CHUNK = 256
FF_NCHUNK = D_FF // FF_CHUNK
DELTA_COLS = 1536
BCD_COLS = 1280
BCD_COL_BLOCKS = (0, 1, 4)
MIX_COLS = DELTA_COLS + BCD_COLS

VMEM_LIMIT = 56 * 1024 * 1024

LONG_TILE_ROWS = 384
LONG_SCAN_STEPS = 129
SHORT_TILE_ROWS = 512


def _const_spec(shape):
    nd = len(shape)
    return pl.BlockSpec(shape, lambda *_: (0,) * nd, pipeline_mode=pl.Buffered(1))


def _layer_weight(lw, name):
    arr, l = lw["stacked"][name], lw["layer"]
    nd = arr.ndim - 1
    spec = pl.BlockSpec((pl.Squeezed(),) + arr.shape[1:], lambda *_: (l,) + (0,) * nd,
                        pipeline_mode=pl.Buffered(1))
    return spec, arr


def _bdot(a, b):
    return jnp.dot(a.astype(BF16), b.astype(BF16), preferred_element_type=F32)


def _rms(x, g):
    return x * lax.rsqrt(jnp.mean(x * x, axis=-1, keepdims=True) + EPS) * g


def _silu(x):
    return x * jax.nn.sigmoid(x)


def _softplus(x):
    return jnp.maximum(x, 0.0) + jnp.log1p(jnp.exp(-jnp.abs(x)))


def _swiglu_residual(x, g_ref, wgu_ref, wd_ref):
    xn = _rms(x, g_ref[...]).astype(BF16)
    acc = jnp.zeros_like(x)
    for c in range(FF_NCHUNK):
        lo, hi = c * FF_CHUNK, (c + 1) * FF_CHUNK
        gate = jnp.dot(xn, wgu_ref[:, lo:hi], preferred_element_type=F32)
        up = jnp.dot(xn, wgu_ref[:, D_FF + lo:D_FF + hi], preferred_element_type=F32)
        h = (_silu(gate) * up).astype(BF16)
        acc = acc + jnp.dot(h, wd_ref[lo:hi, :], preferred_element_type=F32)
    return x + 0.5 * acc


def _stage_time_major_rows(src_ref, stage, bsz, row0=0):
    steps, ntile = src_ref.shape[1], src_ref.shape[2] // LANES
    for b in range(bsz):
        for j in range(ntile):
            stage[j, pl.ds(row0 + b, steps, stride=bsz), :] = src_ref[b, :, j * LANES:(j + 1) * LANES]


def _to_time_major_rows(src_ref, stage, bsz):
    _stage_time_major_rows(src_ref, stage, bsz)
    return jnp.concatenate([stage[j] for j in range(src_ref.shape[2] // LANES)], axis=1)


def _from_time_major_rows(stage, dst_ref, bsz, tiles=None):
    steps = dst_ref.shape[1]
    for j in (range(dst_ref.shape[2] // LANES) if tiles is None else tiles):
        for b in range(bsz):
            dst_ref[b, :, j * LANES:(j + 1) * LANES] = stage[j, pl.ds(b, steps, stride=bsz), :]


def _ffn_inproj(x, g1_ref, wgu_ref, wd_ref, gm_ref, wmix_ref, x1_ref):
    x1 = _swiglu_residual(x, g1_ref, wgu_ref, wd_ref)
    x1_ref[...] = x1
    u = _rms(x1, gm_ref[...]).astype(BF16)
    return lambda c: jnp.dot(u, wmix_ref[:, c * 256:(c + 1) * 256], preferred_element_type=F32)


def _pre_kernel(x_ref, g1_ref, wgu_ref, wd_ref, gm_ref, wmix_ref, x1_ref, zd_ref, zb_ref):
    z_block = _ffn_inproj(x_ref[...], g1_ref, wgu_ref, wd_ref, gm_ref, wmix_ref, x1_ref)
    for c in range(DELTA_COLS // 256):
        zd_ref[:, c * 256:(c + 1) * 256] = z_block(c)
    for c in range(BCD_COLS // 256):
        zb_ref[:, c * 256:(c + 1) * 256] = z_block(DELTA_COLS // 256 + c)


def _pre_tm_kernel(bsz, lead, nx, *refs):
    x_refs, refs = refs[:max(nx, 1)], refs[max(nx, 1):]
    if nx:
        prefix_ref, refs = refs[0], refs[1:]
    (g1_ref, wgu_ref, wd_ref, gm_ref, wmix_ref, cw_ref, buf_ref,
     x1_ref, zd_ref, zb_ref, nbuf_ref, stage, carry, *stage_x) = refs
    tm = x1_ref.shape[0]
    lb = (DN_K - 1) * bsz
    step = pl.program_id(0)

    @pl.when(step < lead)
    def _():
        zd_ref[...] = jnp.zeros(zd_ref.shape, F32)

    @pl.when(step == lead)
    def _():
        carry[...] = buf_ref[...]

    if nx:
        sub = tm // nx
        for j, ref in enumerate(x_refs):
            _stage_time_major_rows(ref, stage_x[0], bsz, j * sub)

        @pl.when(step == lead)
        def _():
            for b in range(bsz):
                for j in range(D_MODEL // LANES):
                    stage_x[0][j, pl.ds(b, sub // bsz, stride=bsz), :] = prefix_ref[:, j * LANES:(j + 1) * LANES]

    @pl.when(step >= lead)
    def _():
        if nx:
            x = jnp.concatenate([stage_x[0][j] for j in range(D_MODEL // LANES)], axis=1)
        else:
            x = x_refs[0][...]
        z_block = _ffn_inproj(x, g1_ref, wgu_ref, wd_ref, gm_ref, wmix_ref, x1_ref)
        for c in range(DELTA_COLS // 256):
            sl = slice(c * 256, (c + 1) * 256)
            zc = z_block(c)
            if c < QKV_W // 256:
                win = jnp.concatenate([carry[:, sl], zc], axis=0)
                y = cw_ref[0:1, sl] * win[0:tm]
                for k in range(1, DN_K):
                    y = y + cw_ref[k:k + 1, sl] * win[k * bsz:k * bsz + tm]
                carry[:, sl] = zc[tm - lb:, :]
                nbuf_ref[:, sl] = zc[tm - lb:, :]
                zc = _silu(y)
            stage[2 * c] = zc[:, 0:LANES]
            stage[2 * c + 1] = zc[:, LANES:2 * LANES]
        nb, nt = BCD_COLS // 256, DELTA_COLS // LANES
        for c in range(nb):
            zb_ref[:, c * 256:(c + 1) * 256] = z_block(DELTA_COLS // 256 + c)
            _from_time_major_rows(stage, zd_ref, bsz, range(c * nt // nb, (c + 1) * nt // nb))


def _pre_call(x, lw, tm, bsz=0, conv_buf=None, lead=0, prefix=None):
    tile = lambda i: jnp.maximum(i - lead, 0)
    row = lambda w: pl.BlockSpec((tm, w), lambda i: (tile(i), 0))
    if prefix is None:
        n, nx = x.shape[0], 0
        x_specs, x_args = [row(D_MODEL)], [x]
    else:
        npre, steps = prefix.shape[0], tm // bsz
        n = (x.shape[1] + npre) * bsz
        nx, nblk = steps // npre, x.shape[1] // npre
        assert nx * npre == steps and nblk * npre == x.shape[1]
        piece = lambda j: pl.BlockSpec((bsz, npre, D_MODEL),
                                       lambda i: (0, jnp.clip(tile(i) * nx + j - 1, 0, nblk - 1), 0))
        x_specs = [piece(j) for j in range(nx)] + [_const_spec((npre, D_MODEL))]
        x_args = [x] * nx + [prefix]
    wgu, wd, wmix = (_layer_weight(lw, k) for k in ("ffn1_wgu", "ffn1_wd", "w_mix"))
    in_specs = x_specs + [_const_spec((1, D_MODEL)), wgu[0], wd[0], _const_spec((1, D_MODEL)), wmix[0]]
    args = x_args + [lw["ffn1_norm"], wgu[1], wd[1], lw["mix_norm"], wmix[1]]
    out_specs = [row(D_MODEL), row(DELTA_COLS), row(BCD_COLS)]
    out_shape = [jax.ShapeDtypeStruct((n, D_MODEL), F32), jax.ShapeDtypeStruct((n, DELTA_COLS), F32),
                 jax.ShapeDtypeStruct((n, BCD_COLS), F32)]
    body, scratch = _pre_kernel, []
    if bsz:
        lb = (DN_K - 1) * bsz
        steps = tm // bsz
        body = functools.partial(_pre_tm_kernel, bsz, lead, nx)
        in_specs += [_const_spec((DN_K, QKV_W)), _const_spec((lb, QKV_W))]
        args += [lw["dn_conv_w"], conv_buf]
        out_specs[1] = pl.BlockSpec((bsz, steps, DELTA_COLS), lambda i: (0, i, 0))
        out_shape[1] = jax.ShapeDtypeStruct((bsz, lead * steps + n // bsz, DELTA_COLS), F32)
        out_specs.append(pl.BlockSpec((lb, QKV_W), lambda i: (0, 0)))
        out_shape.append(jax.ShapeDtypeStruct((lb, QKV_W), F32))
        scratch = [pltpu.VMEM((DELTA_COLS // LANES, tm, LANES), F32), pltpu.VMEM((lb, QKV_W), F32)]
        if nx:
            scratch.append(pltpu.VMEM((D_MODEL // LANES, tm, LANES), F32))
    return pl.pallas_call(
        body, grid=(lead + n // tm,), in_specs=in_specs, out_specs=out_specs, out_shape=out_shape,
        scratch_shapes=scratch,
        compiler_params=pltpu.CompilerParams(dimension_semantics=("arbitrary",), vmem_limit_bytes=VMEM_LIMIT),
        name="pre",
    )(*args)


def _post_kernel(final, bsz, out_bm, x1_ref, oa_ref, obcd_ref, gm_ref, wgate_ref, wbr_ref, wout_ref,
                 g2_ref, wgu_ref, wd_ref, gf_ref, out_ref, *scratch):
    x1 = x1_ref[...]
    oa = _to_time_major_rows(oa_ref, scratch[0], bsz) if bsz else oa_ref[...]
    u = _rms(x1, gm_ref[...]).astype(BF16)
    m = jnp.zeros_like(x1)
    for i in range(4):
        gates = jax.nn.sigmoid(jnp.dot(u, wgate_ref[:, i * D_MODEL:(i + 1) * D_MODEL],
                                       preferred_element_type=F32))
        br = oa if i == 0 else obcd_ref[:, (i - 1) * 256:i * 256]
        m = m + gates * _bdot(br, wbr_ref[i])
    x2 = x1 + _bdot(m, wout_ref[...])
    x3 = _swiglu_residual(x2, g2_ref, wgu_ref, wd_ref)
    if final:
        x3 = _rms(x3, gf_ref[...])
    if out_bm:
        for j in range(D_MODEL // LANES):
            scratch[1][j] = x3[:, j * LANES:(j + 1) * LANES]
        _from_time_major_rows(scratch[1], out_ref, bsz)
    else:
        out_ref[...] = x3


def _post_call(x1, oa, obcd, lw, final_norm, final, tm, bsz=0, lead=0, out_bm=False):
    n = x1.shape[0]
    row = lambda w: pl.BlockSpec((tm, w), lambda i: (i, 0))
    oa_spec = pl.BlockSpec((bsz, tm // bsz, 256), lambda i: (0, i + lead, 0)) if bsz else row(256)
    scratch = [pltpu.VMEM((256 // LANES, tm, LANES), F32)] if bsz else []
    big = [_layer_weight(lw, k) for k in ("w_gate", "w_branch", "w_out", "ffn2_wgu", "ffn2_wd")]
    out_spec, out_shape = row(D_MODEL), jax.ShapeDtypeStruct((n, D_MODEL), F32)
    if out_bm:
        scratch.append(pltpu.VMEM((D_MODEL // LANES, tm, LANES), F32))
        out_spec = pl.BlockSpec((bsz, tm // bsz, D_MODEL), lambda i: (0, i, 0))
        out_shape = jax.ShapeDtypeStruct((bsz, n // bsz, D_MODEL), F32)
    return pl.pallas_call(
        functools.partial(_post_kernel, final, bsz, out_bm),
        grid=(n // tm,),
        scratch_shapes=scratch,
        in_specs=[row(D_MODEL), oa_spec, row(768), _const_spec((1, D_MODEL)), big[0][0], big[1][0], big[2][0],
                  _const_spec((1, D_MODEL)), big[3][0], big[4][0], _const_spec((1, D_MODEL))],
        out_specs=out_spec,
        out_shape=out_shape,
        compiler_params=pltpu.CompilerParams(dimension_semantics=("arbitrary",), vmem_limit_bytes=VMEM_LIMIT),
        name="post",
    )(x1, oa, obcd, lw["mix_norm"], big[0][1], big[1][1], big[2][1],
      lw["ffn2_norm"], big[3][1], big[4][1], final_norm)


def _bmm(a, b, contract=(2, 1)):
    dims = (((contract[0],), (contract[1],)), ((0,), (0,)))
    return lax.dot_general(a, b, dims, preferred_element_type=F32)


def _chunk_masks(seq_len):
    ri = lax.broadcasted_iota(jnp.int32, (CHUNK, CHUNK), 0)
    ci = lax.broadcasted_iota(jnp.int32, (CHUNK, CHUNK), 1)
    causal = ri >= ci
    strict = ri > ci
    if seq_len < CHUNK:
        same = (ri // seq_len) == (ci // seq_len)
        causal = causal & same
        strict = strict & same
    return dict(causal_f=causal.astype(F32), strict_f=strict.astype(F32), eye_f=(ri == ci).astype(F32),
                blk_f=((ri // SOLVE_BLOCK) == (ci // SOLVE_BLOCK)).astype(F32), rowseq=ri // seq_len,
                rowpos=lax.broadcasted_iota(jnp.int32, (CHUNK, 256), 0) % seq_len)


def _wy_solve(low, rhs, mk, seq_len):
    eye_f = mk["eye_f"]
    if seq_len > SOLVE_BLOCK:
        nd = low * mk["blk_f"]
        off = low - nd
        blk = SOLVE_BLOCK
    else:
        nd, off, blk = low, None, seq_len
    p = eye_f - nd
    n16 = nd.astype(BF16)
    for _ in range(blk.bit_length() - 2):
        n16 = _bmm(n16, n16).astype(BF16)
        p = p + _bmm(p.astype(BF16), n16)
    p16 = p.astype(BF16)
    y = _bmm(p16, rhs.astype(BF16))
    if off is None:
        return y
    assert seq_len // SOLVE_BLOCK == 4
    m16 = _bmm(p16, off.astype(BF16)).astype(BF16)
    y2 = y + _bmm(_bmm(m16, m16).astype(BF16), y.astype(BF16))
    return y2 - _bmm(m16, y2.astype(BF16))


def _delta_elementwise(chunks, mk, seq_len):
    qs, ks, vs, bs, gcs = [], [], [], [], []
    for q_in, k_in, v_in, beta, gl in chunks:
        gc = gl
        shift = 1
        while shift < seq_len:
            gc = gc + jnp.where(mk["rowpos"] >= shift, pltpu.roll(gc, shift, axis=0), 0.0)
            shift *= 2
        for h in range(HEADS):
            sl = slice(h * DK, (h + 1) * DK)
            qs.append(q_in[:, sl])
            ks.append(k_in[:, sl])
            vs.append(v_in[:, sl])
            bs.append(beta[:, sl])
            gcs.append(gc[:, sl])
    q, k, v, b, gc = [jnp.stack(a, axis=0) for a in (qs, ks, vs, bs, gcs)]
    q = q * lax.rsqrt(jnp.sum(q * q, axis=-1, keepdims=True) + EPS) * (DK ** -0.5)
    k = k * lax.rsqrt(jnp.sum(k * k, axis=-1, keepdims=True) + EPS)
    gct = jnp.sum(gc * mk["eye_f"], axis=1, keepdims=True)
    dec = jnp.exp(jnp.minimum(gc - gct, 0.0))
    eg = jnp.exp(gc)
    kb = k * b
    if seq_len == CHUNK:
        glast = gc[:, CHUNK - 1:CHUNK, :]
    else:
        glast = jnp.concatenate(
            [jnp.broadcast_to(gc[:, (j + 1) * seq_len - 1:(j + 1) * seq_len, :], (gc.shape[0], seq_len, DK))
             for j in range(CHUNK // seq_len)], axis=1)
    return dict(kb16=kb.astype(BF16), k16=k.astype(BF16), q16=q.astype(BF16), dec=dec,
                rhs=jnp.concatenate([v * b, kb * eg], axis=2), qe16=(q * eg).astype(BF16),
                kd16=(k * jnp.exp(glast - gc)).astype(BF16), egl=jnp.exp(glast))


def _delta_solve_store(ew, mk, seq_len, scr, slots):
    u_s, w_s, qe_s, kd_s, a_s, egl_s = scr
    kk = _bmm(ew["kb16"], ew["k16"], (2, 2))
    qk = _bmm(ew["q16"], ew["k16"], (2, 2))
    low = kk * (ew["dec"] * mk["strict_f"])
    x = _wy_solve(low, ew["rhs"], mk, seq_len)
    w16 = x[:, :, DK:2 * DK].astype(BF16)
    a16 = (qk * (ew["dec"] * mk["causal_f"])).astype(BF16)
    qe16, kd16, egl = ew["qe16"], ew["kd16"], ew["egl"]
    for c, (lead, slot) in enumerate(slots):
        dst = (pl.ds(lead, HEADS), pl.ds(slot * CHUNK, CHUNK), slice(None))
        ps = slice(c * HEADS, (c + 1) * HEADS)
        u_s[dst] = x[ps, :, 0:DK]
        w_s[dst] = w16[ps]
        qe_s[dst] = qe16[ps]
        kd_s[dst] = kd16[ps]
        a_s[dst] = a16[ps]
        egl_s[dst] = jnp.broadcast_to(egl[ps], (HEADS, CHUNK, DK))


def _gated_norm(o, dz, ng):
    return o * lax.rsqrt(jnp.mean(o * o, axis=-1, keepdims=True) + EPS) * ng * _silu(dz)


def _delta_scratch(lead, nslots):
    n = nslots * CHUNK
    return ([pltpu.VMEM((lead, n, DK), F32)] + [pltpu.VMEM((lead, n, DK), BF16)] * 4
            + [pltpu.VMEM((lead, n, DK), F32)])


def _delta_kernel(bsz, nchunk, q_ref, k_ref, v_ref, dz_ref, be_ref, ae_ref, s0_ref, alog_ref, dtb_ref, ng_ref,
                  o_ref, s_ref, *scr):
    @pl.when(pl.program_id(0) == 0)
    def _():
        s_ref[...] = s0_ref[...]

    u_s, w_s, qe_s, kd_s, a_s, egl_s = scr
    mk = _chunk_masks(CHUNK)
    neg_a = -jnp.exp(alog_ref[...])
    dtb = dtb_ref[...]
    ng = ng_ref[...]

    def elementwise(c):
        rows = pl.ds(c * CHUNK, CHUNK)
        chunks = []
        for b in range(bsz):
            beta = jax.nn.sigmoid(be_ref[b, rows, :])
            gl = neg_a * _softplus(ae_ref[b, rows, :] + dtb)
            chunks.append((q_ref[b, rows, :], k_ref[b, rows, :], v_ref[b, rows, :], beta, gl))
        return _delta_elementwise(chunks, mk, CHUNK)

    def state_step(c):
        rows = pl.ds(c * CHUNK, CHUNK)
        s = s_ref[...]
        s16 = s.astype(BF16)
        vnew = u_s[:, rows, :] - _bmm(w_s[:, rows, :], s16)
        vn16 = vnew.astype(BF16)
        o = _bmm(qe_s[:, rows, :], s16) + _bmm(a_s[:, rows, :], vn16)
        s_ref[...] = (s * egl_s[:, pl.ds(c * CHUNK, 8), :][:, 0:1, :]
                      + _bmm(kd_s[:, rows, :], vn16, (1, 1)))
        for b in range(bsz):
            for h in range(HEADS):
                sl = slice(h * DK, (h + 1) * DK)
                o_ref[b, rows, sl] = _gated_norm(o[b * HEADS + h], dz_ref[b, rows, sl], ng)

    ew = elementwise(0)
    for c in range(nchunk):
        ew_next = elementwise(c + 1) if c + 1 < nchunk else None
        _delta_solve_store(ew, mk, CHUNK, scr, [(b * HEADS, c) for b in range(bsz)])
        if c:
            state_step(c - 1)
        ew = ew_next
    state_step(nchunk - 1)


def _delta_call(zd, s0, lw, nchunk=DELTA_BLOCK):
    bsz, t_len, _ = zd.shape
    tb = nchunk * CHUNK
    assert t_len % tb == 0
    zspec = lambda blk: pl.BlockSpec((bsz, tb, 256), lambda i: (0, i, blk))
    sspec = pl.BlockSpec((bsz * HEADS, DK, DK), lambda i: (0, 0, 0))
    o, s = pl.pallas_call(
        functools.partial(_delta_kernel, bsz, nchunk),
        grid=(t_len // tb,),
        in_specs=[zspec(0), zspec(1), zspec(2), zspec(3), zspec(4), zspec(5), sspec,
                  _const_spec((1, 256)), _const_spec((1, 256)), _const_spec((1, DK))],
        out_specs=[pl.BlockSpec((bsz, tb, 256), lambda i: (0, i, 0)), sspec],
        out_shape=[jax.ShapeDtypeStruct((bsz, t_len, 256), F32),
                   jax.ShapeDtypeStruct((bsz * HEADS, DK, DK), F32)],
        scratch_shapes=_delta_scratch(bsz * HEADS, nchunk),
        compiler_params=pltpu.CompilerParams(dimension_semantics=("arbitrary",), vmem_limit_bytes=VMEM_LIMIT),
        name="delta",
    )(zd, zd, zd, zd, zd, zd, s0.reshape(bsz * HEADS, DK, DK), lw["dn_a_log"], lw["dn_dt_bias"], lw["dn_norm"])
    return o, s.reshape(bsz, HEADS, DK, DK)


def _delta_short_kernel(seq_len, q_ref, k_ref, v_ref, dz_ref, be_ref, ae_ref, buf_ref, s0_ref, cw_ref,
                        alog_ref, dtb_ref, ng_ref, o_ref, nbuf_ref, sfin_ref, xs_ref, *scr):
    nseq = CHUNK // seq_len
    span = seq_len + 8
    mk = _chunk_masks(seq_len)
    cw = cw_ref[...]
    xs_ref[...] = jnp.zeros(xs_ref.shape, F32)
    for j in range(nseq):
        xs_ref[j * span + 5:j * span + 8, :] = buf_ref[j]
        rows = slice(j * seq_len, (j + 1) * seq_len)
        xs_ref[j * span + 8:(j + 1) * span, 0:256] = q_ref[rows, :]
        xs_ref[j * span + 8:(j + 1) * span, 256:512] = k_ref[rows, :]
        xs_ref[j * span + 8:(j + 1) * span, 512:768] = v_ref[rows, :]
    ys = []
    for j in range(nseq):
        win = xs_ref[j * span:(j + 1) * span, :]
        ys.append(cw[3:4] * win[8:span] + cw[2:3] * win[7:span - 1]
                  + cw[1:2] * win[6:span - 2] + cw[0:1] * win[5:span - 3])
        nbuf_ref[j] = win[span - 3:span, :]
    qkv = _silu(jnp.concatenate(ys, axis=0))
    beta = jax.nn.sigmoid(be_ref[...])
    gl = -jnp.exp(alog_ref[...]) * _softplus(ae_ref[...] + dtb_ref[...])
    ew = _delta_elementwise([(qkv[:, 0:256], qkv[:, 256:512], qkv[:, 512:768], beta, gl)], mk, seq_len)
    _delta_solve_store(ew, mk, seq_len, scr, [(0, 0)])

    u_s, w_s, qe_s, kd_s, a_s, egl_s = scr
    w = w_s[...]
    qe = qe_s[...]
    kd = kd_s[...].astype(F32)
    u = u_s[...]
    ws, qs = [], []
    for j in range(nseq):
        s16 = s0_ref[j].astype(BF16)
        ws.append(_bmm(w, s16))
        qs.append(_bmm(qe, s16))
    vnew, o = u, jnp.zeros_like(u)
    for j in range(nseq):
        mine = mk["rowseq"] == j
        vnew = jnp.where(mine, u - ws[j], vnew)
        o = jnp.where(mine, qs[j], o)
    vn16 = vnew.astype(BF16)
    o = o + _bmm(a_s[...], vn16)
    for j in range(nseq):
        kdj = jnp.where(mk["rowseq"] == j, kd, 0.0).astype(BF16)
        sfin_ref[j] = (s0_ref[j] * egl_s[:, j * seq_len:(j + 1) * seq_len, :][:, 0:1, :]
                       + _bmm(kdj, vn16, (1, 1)))
    ng = ng_ref[...]
    for h in range(HEADS):
        sl = slice(h * DK, (h + 1) * DK)
        o_ref[:, sl] = _gated_norm(o[h], dz_ref[:, sl], ng)


def _delta_short_call(z, seq_len, buf, s0, lw):
    n = z.shape[0]
    nseq = CHUNK // seq_len
    zspec = lambda blk: pl.BlockSpec((CHUNK, 256), lambda g: (g, blk))
    bspec = pl.BlockSpec((nseq, DN_K - 1, QKV_W), lambda g: (g, 0, 0))
    sspec = pl.BlockSpec((nseq, HEADS, DK, DK), lambda g: (g, 0, 0, 0))
    return pl.pallas_call(
        functools.partial(_delta_short_kernel, seq_len),
        grid=(n // CHUNK,),
        in_specs=[zspec(0), zspec(1), zspec(2), zspec(3), zspec(4), zspec(5), bspec, sspec, _const_spec((DN_K, QKV_W)), _const_spec((1, 256)), _const_spec((1, 256)),
                  _const_spec((1, DK))],
        out_specs=[pl.BlockSpec((CHUNK, 256), lambda g: (g, 0)), bspec, sspec],
        out_shape=[jax.ShapeDtypeStruct((n, 256), F32), jax.ShapeDtypeStruct(buf.shape, F32),
                   jax.ShapeDtypeStruct(s0.shape, F32)],
        scratch_shapes=[pltpu.VMEM((nseq * (seq_len + 8), QKV_W), F32)] + _delta_scratch(HEADS, 1),
        compiler_params=pltpu.CompilerParams(dimension_semantics=("arbitrary",), vmem_limit_bytes=VMEM_LIMIT),
        name="delta_short",
    )(z, z, z, z, z, z, buf, s0, lw["dn_conv_w"], lw["dn_a_log"], lw["dn_dt_bias"], lw["dn_norm"])


def _scan_time_major(t_len, bsz, state_refs, step):
    def run_group(goff):
        hs = tuple(r[pl.ds(goff, 8), :] for r in state_refs)
        if t_len <= 8:
            for t in range(t_len):
                hs = step(hs, t * bsz + goff)
        else:
            def body(t, hs):
                return step(hs, pl.multiple_of(t * bsz + goff, 8))
            hs = lax.fori_loop(0, t_len, body, hs, unroll=3)
        for r, h in zip(state_refs, hs):
            r[pl.ds(goff, 8), :] = h

    if bsz == 8:
        run_group(0)
    else:
        def gbody(g, carry):
            run_group(pl.multiple_of(g * 8, 8))
            return carry
        lax.fori_loop(0, bsz // 8, gbody, 0)


def _bcd_kernel(t_len, bsz, nsteps,
                lru_ref, cv_ref, s5_ref, s5re0, s5im0, lru0, lbuf0, cbuf0,
                lam_re_ref, lam_im_ref, lstep_ref, wb_ref, wcre_ref, wcim_ref, dskip_ref, wglu_ref, bglu_ref,
                lcw_ref, lcb_ref, wa_ref, ba_ref, wx_ref, bx_ref, llam_ref,
                ccw_ref, ccb_ref, lng_ref, lnb_ref,
                o_ref, s5re_o, s5im_o, lru_o, lbuf_o, cbuf_o,
                xr, xi, lxs, cxs, a_s, b_s, wbp):
    rows = t_len * bsz
    lb = (LRU_K - 1) * bsz
    cb = (CV_K - 1) * bsz

    @pl.when(pl.program_id(0) == 0)
    def _():
        s5re_o[...] = s5re0[...]
        s5im_o[...] = s5im0[...]
        lru_o[...] = lru0[...]
        lxs[0:lb, :] = lbuf0[...]
        cxs[0:cb, :] = cbuf0[...]

    lxs[lb:lb + rows, :] = lru_ref[:, 0:256]
    xf = lcb_ref[...] + lcw_ref[0:1, :] * lxs[0:rows, :]
    for k in range(1, LRU_K):
        xf = xf + lcw_ref[k:k + 1, :] * lxs[k * bsz:k * bsz + rows, :]
    r = jax.nn.sigmoid(_bdot(xf, wa_ref[...]) + ba_ref[...])
    i = jax.nn.sigmoid(_bdot(xf, wx_ref[...]) + bx_ref[...])
    log_a = (-LRU_C) * r * _softplus(-llam_ref[...])
    a_s[...] = jnp.exp(log_a)
    b_s[...] = jnp.sqrt(1.0 - jnp.exp(2.0 * log_a)) * (i * xf)

    def lru_step(hs, row):
        h = a_s[pl.ds(row, 8), :] * hs[0] + b_s[pl.ds(row, 8), :]
        b_s[pl.ds(row, 8), :] = h
        return (h,)
    _scan_time_major(t_len, bsz, (lru_o,), lru_step)
    o_ref[:, 256:512] = b_s[...] * jax.nn.gelu(lru_ref[:, 256:512])
    lbuf_o[...] = lxs[rows:rows + lb, :]
    if nsteps > 1:
        lxs[0:lb, :] = lxs[rows:rows + lb, :]

    dt = jnp.exp(lstep_ref[...])
    lam_re = lam_re_ref[...]
    lam_im = lam_im_ref[...]
    mag = jnp.exp(lam_re * dt)
    lb_re = mag * jnp.cos(lam_im * dt)
    lb_im = mag * jnp.sin(lam_im * dt)
    den = lam_re * lam_re + lam_im * lam_im
    cf_re = ((lb_re - 1.0) * lam_re + lb_im * lam_im) / den
    cf_im = (lb_im * lam_re - (lb_re - 1.0) * lam_im) / den

    @pl.when(pl.program_id(0) == 0)
    def _():
        b_re = wb_ref[:, 0:S5_STATES]
        b_im = wb_ref[:, S5_STATES:2 * S5_STATES]
        wbp[:, 0:S5_STATES] = (cf_re * b_re - cf_im * b_im).astype(BF16)
        wbp[:, S5_STATES:2 * S5_STATES] = (cf_re * b_im + cf_im * b_re).astype(BF16)

    u16 = s5_ref[...].astype(BF16)
    xr[...] = jnp.dot(u16, wbp[:, 0:S5_STATES], preferred_element_type=F32)
    xi[...] = jnp.dot(u16, wbp[:, S5_STATES:2 * S5_STATES], preferred_element_type=F32)
    lbr = jnp.broadcast_to(lb_re, (8, S5_STATES))
    lbi = jnp.broadcast_to(lb_im, (8, S5_STATES))

    def s5_step(hs, row):
        hr, hi = hs
        nr = lbr * hr - lbi * hi + xr[pl.ds(row, 8), :]
        ni = lbr * hi + lbi * hr + xi[pl.ds(row, 8), :]
        xr[pl.ds(row, 8), :] = nr
        xi[pl.ds(row, 8), :] = ni
        return (nr, ni)
    _scan_time_major(t_len, bsz, (s5re_o, s5im_o), s5_step)
    y = (_bdot(xr[...], wcre_ref[...]) - _bdot(xi[...], wcim_ref[...])
         + dskip_ref[...] * s5_ref[...])
    y = jax.nn.gelu(y)
    glu = _bdot(y, wglu_ref[...]) + bglu_ref[...]
    o_ref[:, 0:256] = glu[:, 0:256] * jax.nn.sigmoid(glu[:, 256:512])

    cxs[cb:cb + rows, :] = cv_ref[:, 0:256] * jax.nn.sigmoid(cv_ref[:, 256:512])
    yc = ccb_ref[...] + ccw_ref[0:1, :] * cxs[0:rows, :]
    for k in range(1, CV_K):
        yc = yc + ccw_ref[k:k + 1, :] * cxs[k * bsz:k * bsz + rows, :]
    mu = jnp.mean(yc, axis=-1, keepdims=True)
    ycc = yc - mu
    yn = ycc * lax.rsqrt(jnp.mean(ycc * ycc, axis=-1, keepdims=True) + EPS) * lng_ref[...] + lnb_ref[...]
    o_ref[:, 512:768] = _silu(yn)
    cbuf_o[...] = cxs[rows:rows + cb, :]
    if nsteps > 1:
        cxs[0:cb, :] = cxs[rows:rows + cb, :]


def _bcd_call(zmix, states, lw, t_len, bsz, tb):
    col_blocks = BCD_COL_BLOCKS
    n = zmix.shape[0]
    nsteps = t_len // tb
    rows = tb * bsz
    lb = (LRU_K - 1) * bsz
    cb = (CV_K - 1) * bsz
    assert nsteps == 1 or rows >= cb
    zspec = lambda w, blk: pl.BlockSpec((rows, w), lambda i: (i, blk))
    state_shapes = [(bsz, S5_STATES), (bsz, S5_STATES), (bsz, 256), (lb, 256), (cb, 256)]
    params = [lw["s5_lam_re"], lw["s5_lam_im"], lw["s5_log_step"], lw["s5_wb"], lw["s5_wcre"], lw["s5_wcim"],
              lw["s5_d"], lw["s5_w_glu"], lw["s5_b_glu"],
              lw["lru_conv_w"], lw["lru_conv_b"], lw["lru_wa"], lw["lru_b_a"], lw["lru_wx"], lw["lru_b_x"],
              lw["lru_lam"], lw["cv_conv_w"], lw["cv_conv_b"], lw["cv_ln_g"], lw["cv_ln_b"]]
    return pl.pallas_call(
        functools.partial(_bcd_kernel, tb, bsz, nsteps),
        grid=(nsteps,),
        in_specs=([zspec(512, col_blocks[0]), zspec(512, col_blocks[1]), zspec(256, col_blocks[2])]
                  + [_const_spec(s) for s in state_shapes]
                  + [_const_spec(p.shape) for p in params]),
        out_specs=[pl.BlockSpec((rows, 768), lambda i: (i, 0))]
                  + [pl.BlockSpec(s, lambda i: (0, 0)) for s in state_shapes],
        out_shape=[jax.ShapeDtypeStruct((n, 768), F32)]
                  + [jax.ShapeDtypeStruct(s, F32) for s in state_shapes],
        scratch_shapes=[pltpu.VMEM((rows, S5_STATES), F32), pltpu.VMEM((rows, S5_STATES), F32),
                        pltpu.VMEM((lb + rows, 256), F32), pltpu.VMEM((cb + rows, 256), F32),
                        pltpu.VMEM((rows, 256), F32), pltpu.VMEM((rows, 256), F32),
                        pltpu.VMEM((256, 2 * S5_STATES), BF16)],
        compiler_params=pltpu.CompilerParams(dimension_semantics=("arbitrary",), vmem_limit_bytes=VMEM_LIMIT),
        name="bcd",
    )(zmix, zmix, zmix, *states, *params)


def _block_diag(m):
    g, r, c = m.shape
    return (jnp.eye(g, dtype=m.dtype)[:, None, :, None] * m[:, :, None, :]).reshape(g * r, g * c)


def _stacked_weights(p):
    w_in = p["w_in"]
    w_mix = jnp.concatenate(
        [w_in[..., 0:1024], jnp.repeat(w_in[..., 1024:1028], DK, axis=-1),
         jnp.repeat(w_in[..., 1028:1032], DK, axis=-1),
         w_in[..., 1288:1800], w_in[..., 1800:2312], w_in[..., 1032:1288]], axis=-1)
    return dict(ffn1_wgu=p["ffn1_w_gu"].astype(BF16), ffn1_wd=p["ffn1_w_down"].astype(BF16),
                ffn2_wgu=p["ffn2_w_gu"].astype(BF16), ffn2_wd=p["ffn2_w_down"].astype(BF16),
                w_mix=w_mix.astype(BF16), w_gate=w_in[..., 2312:].astype(BF16),
                w_branch=p["w_branch"].astype(BF16), w_out=p["w_out"].astype(BF16))


def _layer_weights(l, p, stacked):
    row = lambda v: v.reshape(1, -1).astype(F32)
    lw = dict(layer=l, stacked=stacked)
    lw["ffn1_norm"] = row(p["ffn1_norm"][l])
    lw["ffn2_norm"] = row(p["ffn2_norm"][l])
    lw["mix_norm"] = row(p["mix_norm"][l])
    lw["dn_conv_w"] = p["dn_conv_w"][l]
    lw["dn_a_log"] = row(jnp.repeat(p["dn_a_log"][l], DK))
    lw["dn_dt_bias"] = row(jnp.repeat(p["dn_dt_bias"][l], DK))
    lw["dn_norm"] = row(p["dn_norm"][l])
    lw["s5_lam_re"] = row(p["s5_lam_re"][l])
    lw["s5_lam_im"] = row(p["s5_lam_im"][l])
    lw["s5_log_step"] = row(jnp.repeat(p["s5_log_step"][l], 64))
    bdt = lambda w: _block_diag(jnp.swapaxes(w, 1, 2))
    lw["s5_wb"] = jnp.concatenate([bdt(p["s5_b_re"][l]), bdt(p["s5_b_im"][l])], axis=1).astype(F32)
    lw["s5_wcre"] = bdt(p["s5_c_re"][l]).astype(BF16)
    lw["s5_wcim"] = bdt(p["s5_c_im"][l]).astype(BF16)
    lw["s5_d"] = row(p["s5_d"][l])
    lw["s5_w_glu"] = p["s5_w_glu"][l].astype(BF16)
    lw["s5_b_glu"] = row(p["s5_b_glu"][l])
    lw["lru_conv_w"] = p["lru_conv_w"][l]
    lw["lru_conv_b"] = row(p["lru_conv_b"][l])
    lw["lru_wa"] = _block_diag(p["lru_w_a"][l]).astype(BF16)
    lw["lru_wx"] = _block_diag(p["lru_w_x"][l]).astype(BF16)
    lw["lru_b_a"] = row(p["lru_b_a"][l])
    lw["lru_b_x"] = row(p["lru_b_x"][l])
    lw["lru_lam"] = row(p["lru_lam"][l])
    lw["cv_conv_w"] = p["cv_conv_w"][l]
    lw["cv_conv_b"] = row(p["cv_conv_b"][l])
    lw["cv_ln_g"] = row(p["cv_ln_g"][l])
    lw["cv_ln_b"] = row(p["cv_ln_b"][l])
    return lw


def _to_time_major(a):
    bsz, k, c = a.shape
    return jnp.transpose(a, (1, 0, 2)).reshape(k * bsz, c)


def _from_time_major(a, bsz):
    k = a.shape[0] // bsz
    return jnp.transpose(a.reshape(k, bsz, a.shape[1]), (1, 0, 2))


def _bcd_states(st, bsz):
    _, _, s_re, s_im, s_lru, s_lruc, s_cv = st
    return (s_re.reshape(bsz, S5_STATES), s_im.reshape(bsz, S5_STATES), s_lru,
            _to_time_major(s_lruc), _to_time_major(s_cv))


def _new_states(n_dn, n_dnc, bcd_new, bsz):
    n_re, n_im, n_lru, n_lruc, n_cv = bcd_new
    return (n_dn, n_dnc, n_re.reshape(bsz, 16, 64), n_im.reshape(bsz, 16, 64), n_lru,
            _from_time_major(n_lruc, bsz), _from_time_major(n_cv, bsz))


def _layer_long(x, st, lw, final_norm, final, t_len, bsz, tm, tb, prefix=None):
    pad = (-t_len) % (DELTA_BLOCK * CHUNK)
    lead, rem = divmod(pad * bsz, tm)
    assert rem == 0
    x1, zd, zb, n_dnc = _pre_call(x, lw, tm, bsz, _to_time_major(st[1]), lead, prefix)
    n_dnc = _from_time_major(n_dnc, bsz)
    oa, n_dn = _delta_call(zd, st[0], lw)
    obcd, *bcd_new = _bcd_call(zb, _bcd_states(st, bsz), lw, t_len, bsz, tb)
    x3 = _post_call(x1, oa, obcd, lw, final_norm, final, tm, bsz, lead, out_bm=final)
    return x3, _new_states(n_dn, n_dnc, bcd_new, bsz)


def _layer_short(x, st, lw, final_norm, final, t_len, bsz, tm):
    x1, zd, zb = _pre_call(x, lw, tm)
    oa, n_dnc, n_dn = _delta_short_call(zd, t_len, st[1], st[0], lw)
    z_tm = _to_time_major(zb.reshape(bsz, t_len, BCD_COLS))
    obcd_tm, *bcd_new = _bcd_call(z_tm, _bcd_states(st, bsz), lw, t_len, bsz, t_len)
    obcd = _from_time_major(obcd_tm, bsz).reshape(bsz * t_len, 768)
    x3 = _post_call(x1, oa, obcd, lw, final_norm, final, tm)
    return x3, _new_states(n_dn, n_dnc, bcd_new, bsz)


def _zero_state(bsz):
    return (jnp.zeros((bsz, HEADS, DK, DK), F32), jnp.zeros((bsz, DN_K - 1, QKV_W), F32),
            jnp.zeros((bsz, 16, 64), F32), jnp.zeros((bsz, 16, 64), F32), jnp.zeros((bsz, 256), F32),
            jnp.zeros((bsz, LRU_K - 1, 256), F32), jnp.zeros((bsz, CV_K - 1, 256), F32))


def kernel(x_prompt, x_sample, state_delta, state_delta_conv, state_s5_re, state_s5_im, state_lru, state_lru_conv, state_conv, meta_tokens, ffn1_norm, ffn1_w_gu, ffn1_w_down, mix_norm, w_in, dn_conv_w, dn_a_log, dn_dt_bias, dn_norm, s5_lam_re, s5_lam_im, s5_log_step, s5_b_re, s5_b_im, s5_c_re, s5_c_im, s5_d, s5_w_glu, s5_b_glu, lru_conv_w, lru_conv_b, lru_w_a, lru_b_a, lru_w_x, lru_b_x, lru_lam, cv_conv_w, cv_conv_b, cv_ln_g, cv_ln_b, w_branch, w_out, ffn2_norm, ffn2_w_gu, ffn2_w_down, final_norm):
    p = dict(ffn1_norm=ffn1_norm, ffn1_w_gu=ffn1_w_gu, ffn1_w_down=ffn1_w_down, mix_norm=mix_norm, w_in=w_in,
             dn_conv_w=dn_conv_w, dn_a_log=dn_a_log, dn_dt_bias=dn_dt_bias, dn_norm=dn_norm,
             s5_lam_re=s5_lam_re, s5_lam_im=s5_lam_im, s5_log_step=s5_log_step, s5_b_re=s5_b_re,
             s5_b_im=s5_b_im, s5_c_re=s5_c_re, s5_c_im=s5_c_im, s5_d=s5_d, s5_w_glu=s5_w_glu,
             s5_b_glu=s5_b_glu, lru_conv_w=lru_conv_w, lru_conv_b=lru_conv_b, lru_w_a=lru_w_a,
             lru_b_a=lru_b_a, lru_w_x=lru_w_x, lru_b_x=lru_b_x, lru_lam=lru_lam, cv_conv_w=cv_conv_w,
             cv_conv_b=cv_conv_b, cv_ln_g=cv_ln_g, cv_ln_b=cv_ln_b, w_branch=w_branch, w_out=w_out,
             ffn2_norm=ffn2_norm, ffn2_w_gu=ffn2_w_gu, ffn2_w_down=ffn2_w_down)
    depth = w_in.shape[0]
    bp, seq, _ = x_prompt.shape
    bs, dseq, _ = x_sample.shape
    tp = seq + N_META
    fnorm = final_norm.reshape(1, D_MODEL)

    xp = x_prompt
    xs = x_sample.reshape(bs * dseq, D_MODEL)

    p_new, s_new = [], []
    stacked = _stacked_weights(p)
    for l in range(depth):
        lw = _layer_weights(l, p, stacked)
        final = l == depth - 1
        xp, st_p = _layer_long(xp, _zero_state(bp), lw, fnorm, final, tp, bp, tm=LONG_TILE_ROWS, tb=LONG_SCAN_STEPS,
                               prefix=meta_tokens if l == 0 else None)
        st_s = (state_delta[l], state_delta_conv[l], state_s5_re[l], state_s5_im[l], state_lru[l],
                state_lru_conv[l], state_conv[l])
        xs, st_s = _layer_short(xs, st_s, lw, fnorm, final, dseq, bs, tm=SHORT_TILE_ROWS)
        p_new.append(st_p)
        s_new.append(st_s)

    y_prompt = xp[:, N_META:]
    y_sample = xs.reshape(bs, dseq, D_MODEL)
    stack = lambda new, i: jnp.stack([st[i] for st in new], axis=0)
    return (y_prompt, y_sample, *[stack(p_new, i) for i in range(7)], *[stack(s_new, i) for i in range(7)])
```

```python
import functools

import jax
import jax.numpy as jnp
from jax import lax
from jax.experimental import pallas as pl
from jax.experimental.pallas import tpu as pltpu

F32 = jnp.float32
BF16 = jnp.bfloat16

LANES = 128
D_MODEL = 1024
D_FF = 2816
N_META = 16
EPS = 1e-6
HEADS = 4
DK = 64
QKV_W = 768
CHUNK = 64
DELTA_BLOCK = 3
SOLVE_BLOCK = 16
S5_STATES = 1024
LRU_C = 8.0
CV_K = 31
LRU_K = 4
DN_K = 4

FF_CHUNK = 256
FF_NCHUNK = D_FF // FF_CHUNK
DELTA_COLS = 1536
BCD_COLS = 1280
BCD_COL_BLOCKS = (0, 1, 4)
MIX_COLS = DELTA_COLS + BCD_COLS

VMEM_LIMIT = 56 * 1024 * 1024

LONG_TILE_ROWS = 384
LONG_SCAN_STEPS = 129
SHORT_TILE_ROWS = 512


def _const_spec(shape):
    nd = len(shape)
    return pl.BlockSpec(shape, lambda *_: (0,) * nd, pipeline_mode=pl.Buffered(1))


def _layer_weight(lw, name):
    arr, l = lw["stacked"][name], lw["layer"]
    nd = arr.ndim - 1
    spec = pl.BlockSpec((pl.Squeezed(),) + arr.shape[1:], lambda *_: (l,) + (0,) * nd,
                        pipeline_mode=pl.Buffered(1))
    return spec, arr


def _bdot(a, b):
    return jnp.dot(a.astype(BF16), b.astype(BF16), preferred_element_type=F32)


def _rms(x, g):
    return x * lax.rsqrt(jnp.mean(x * x, axis=-1, keepdims=True) + EPS) * g


def _silu(x):
    return x * jax.nn.sigmoid(x)


def _softplus(x):
    return jnp.maximum(x, 0.0) + jnp.log1p(jnp.exp(-jnp.abs(x)))


def _swiglu_residual(x, g_ref, wgu_ref, wd_ref):
    xn = _rms(x, g_ref[...]).astype(BF16)
    acc = jnp.zeros_like(x)
    for c in range(FF_NCHUNK):
        lo, hi = c * FF_CHUNK, (c + 1) * FF_CHUNK
        gate = jnp.dot(xn, wgu_ref[:, lo:hi], preferred_element_type=F32)
        up = jnp.dot(xn, wgu_ref[:, D_FF + lo:D_FF + hi], preferred_element_type=F32)
        h = (_silu(gate) * up).astype(BF16)
        acc = acc + jnp.dot(h, wd_ref[lo:hi, :], preferred_element_type=F32)
    return x + 0.5 * acc


def _stage_time_major_rows(src_ref, stage, bsz, row0=0):
    steps, ntile = src_ref.shape[1], src_ref.shape[2] // LANES
    for b in range(bsz):
        for j in range(ntile):
            stage[j, pl.ds(row0 + b, steps, stride=bsz), :] = src_ref[b, :, j * LANES:(j + 1) * LANES]


def _to_time_major_rows(src_ref, stage, bsz):
    _stage_time_major_rows(src_ref, stage, bsz)
    return jnp.concatenate([stage[j] for j in range(src_ref.shape[2] // LANES)], axis=1)


def _from_time_major_rows(stage, dst_ref, bsz, tiles=None):
    steps = dst_ref.shape[1]
    for j in (range(dst_ref.shape[2] // LANES) if tiles is None else tiles):
        for b in range(bsz):
            dst_ref[b, :, j * LANES:(j + 1) * LANES] = stage[j, pl.ds(b, steps, stride=bsz), :]


def _ffn_inproj(x, g1_ref, wgu_ref, wd_ref, gm_ref, wmix_ref, x1_ref):
    x1 = _swiglu_residual(x, g1_ref, wgu_ref, wd_ref)
    x1_ref[...] = x1
    u = _rms(x1, gm_ref[...]).astype(BF16)
    return lambda c: jnp.dot(u, wmix_ref[:, c * 256:(c + 1) * 256], preferred_element_type=F32)


def _pre_kernel(x_ref, g1_ref, wgu_ref, wd_ref, gm_ref, wmix_ref, x1_ref, zd_ref, zb_ref):
    z_block = _ffn_inproj(x_ref[...], g1_ref, wgu_ref, wd_ref, gm_ref, wmix_ref, x1_ref)
    for c in range(DELTA_COLS // 256):
        zd_ref[:, c * 256:(c + 1) * 256] = z_block(c)
    for c in range(BCD_COLS // 256):
        zb_ref[:, c * 256:(c + 1) * 256] = z_block(DELTA_COLS // 256 + c)


def _pre_tm_kernel(bsz, lead, nx, *refs):
    x_refs, refs = refs[:max(nx, 1)], refs[max(nx, 1):]
    if nx:
        prefix_ref, refs = refs[0], refs[1:]
    (g1_ref, wgu_ref, wd_ref, gm_ref, wmix_ref, cw_ref, buf_ref,
     x1_ref, zd_ref, zb_ref, nbuf_ref, stage, carry, *stage_x) = refs
    tm = x1_ref.shape[0]
    lb = (DN_K - 1) * bsz
    step = pl.program_id(0)

    @pl.when(step < lead)
    def _():
        zd_ref[...] = jnp.zeros(zd_ref.shape, F32)

    @pl.when(step == lead)
    def _():
        carry[...] = buf_ref[...]

    if nx:
        sub = tm // nx
        for j, ref in enumerate(x_refs):
            _stage_time_major_rows(ref, stage_x[0], bsz, j * sub)

        @pl.when(step == lead)
        def _():
            for b in range(bsz):
                for j in range(D_MODEL // LANES):
                    stage_x[0][j, pl.ds(b, sub // bsz, stride=bsz), :] = prefix_ref[:, j * LANES:(j + 1) * LANES]

    @pl.when(step >= lead)
    def _():
        if nx:
            x = jnp.concatenate([stage_x[0][j] for j in range(D_MODEL // LANES)], axis=1)
        else:
            x = x_refs[0][...]
        z_block = _ffn_inproj(x, g1_ref, wgu_ref, wd_ref, gm_ref, wmix_ref, x1_ref)
        for c in range(DELTA_COLS // 256):
            sl = slice(c * 256, (c + 1) * 256)
            zc = z_block(c)
            if c < QKV_W // 256:
                win = jnp.concatenate([carry[:, sl], zc], axis=0)
                y = cw_ref[0:1, sl] * win[0:tm]
                for k in range(1, DN_K):
                    y = y + cw_ref[k:k + 1, sl] * win[k * bsz:k * bsz + tm]
                carry[:, sl] = zc[tm - lb:, :]
                nbuf_ref[:, sl] = zc[tm - lb:, :]
                zc = _silu(y)
            stage[2 * c] = zc[:, 0:LANES]
            stage[2 * c + 1] = zc[:, LANES:2 * LANES]
        nb, nt = BCD_COLS // 256, DELTA_COLS // LANES
        for c in range(nb):
            zb_ref[:, c * 256:(c + 1) * 256] = z_block(DELTA_COLS // 256 + c)
            _from_time_major_rows(stage, zd_ref, bsz, range(c * nt // nb, (c + 1) * nt // nb))


def _pre_call(x, lw, tm, bsz=0, conv_buf=None, lead=0, prefix=None):
    tile = lambda i: jnp.maximum(i - lead, 0)
    row = lambda w: pl.BlockSpec((tm, w), lambda i: (tile(i), 0))
    if prefix is None:
        n, nx = x.shape[0], 0
        x_specs, x_args = [row(D_MODEL)], [x]
    else:
        npre, steps = prefix.shape[0], tm // bsz
        n = (x.shape[1] + npre) * bsz
        nx, nblk = steps // npre, x.shape[1] // npre
        assert nx * npre == steps and nblk * npre == x.shape[1]
        piece = lambda j: pl.BlockSpec((bsz, npre, D_MODEL),
                                       lambda i: (0, jnp.clip(tile(i) * nx + j - 1, 0, nblk - 1), 0))
        x_specs = [piece(j) for j in range(nx)] + [_const_spec((npre, D_MODEL))]
        x_args = [x] * nx + [prefix]
    wgu, wd, wmix = (_layer_weight(lw, k) for k in ("ffn1_wgu", "ffn1_wd", "w_mix"))
    in_specs = x_specs + [_const_spec((1, D_MODEL)), wgu[0], wd[0], _const_spec((1, D_MODEL)), wmix[0]]
    args = x_args + [lw["ffn1_norm"], wgu[1], wd[1], lw["mix_norm"], wmix[1]]
    out_specs = [row(D_MODEL), row(DELTA_COLS), row(BCD_COLS)]
    out_shape = [jax.ShapeDtypeStruct((n, D_MODEL), F32), jax.ShapeDtypeStruct((n, DELTA_COLS), F32),
                 jax.ShapeDtypeStruct((n, BCD_COLS), F32)]
    body, scratch = _pre_kernel, []
    if bsz:
        lb = (DN_K - 1) * bsz
        steps = tm // bsz
        body = functools.partial(_pre_tm_kernel, bsz, lead, nx)
        in_specs += [_const_spec((DN_K, QKV_W)), _const_spec((lb, QKV_W))]
        args += [lw["dn_conv_w"], conv_buf]
        out_specs[1] = pl.BlockSpec((bsz, steps, DELTA_COLS), lambda i: (0, i, 0))
        out_shape[1] = jax.ShapeDtypeStruct((bsz, lead * steps + n // bsz, DELTA_COLS), F32)
        out_specs.append(pl.BlockSpec((lb, QKV_W), lambda i: (0, 0)))
        out_shape.append(jax.ShapeDtypeStruct((lb, QKV_W), F32))
        scratch = [pltpu.VMEM((DELTA_COLS // LANES, tm, LANES), F32), pltpu.VMEM((lb, QKV_W), F32)]
        if nx:
            scratch.append(pltpu.VMEM((D_MODEL // LANES, tm, LANES), F32))
    return pl.pallas_call(
        body, grid=(lead + n // tm,), in_specs=in_specs, out_specs=out_specs, out_shape=out_shape,
        scratch_shapes=scratch,
        compiler_params=pltpu.CompilerParams(dimension_semantics=("arbitrary",), vmem_limit_bytes=VMEM_LIMIT),
        name="pre",
    )(*args)


def _post_kernel(final, bsz, out_bm, x1_ref, oa_ref, obcd_ref, gm_ref, wgate_ref, wbr_ref, wout_ref,
                 g2_ref, wgu_ref, wd_ref, gf_ref, out_ref, *scratch):
    x1 = x1_ref[...]
    oa = _to_time_major_rows(oa_ref, scratch[0], bsz) if bsz else oa_ref[...]
    u = _rms(x1, gm_ref[...]).astype(BF16)
    m = jnp.zeros_like(x1)
    for i in range(4):
        gates = jax.nn.sigmoid(jnp.dot(u, wgate_ref[:, i * D_MODEL:(i + 1) * D_MODEL],
                                       preferred_element_type=F32))
        br = oa if i == 0 else obcd_ref[:, (i - 1) * 256:i * 256]
        m = m + gates * _bdot(br, wbr_ref[i])
    x2 = x1 + _bdot(m, wout_ref[...])
    x3 = _swiglu_residual(x2, g2_ref, wgu_ref, wd_ref)
    if final:
        x3 = _rms(x3, gf_ref[...])
    if out_bm:
        for j in range(D_MODEL // LANES):
            scratch[1][j] = x3[:, j * LANES:(j + 1) * LANES]
        _from_time_major_rows(scratch[1], out_ref, bsz)
    else:
        out_ref[...] = x3


def _post_call(x1, oa, obcd, lw, final_norm, final, tm, bsz=0, lead=0, out_bm=False):
    n = x1.shape[0]
    row = lambda w: pl.BlockSpec((tm, w), lambda i: (i, 0))
    oa_spec = pl.BlockSpec((bsz, tm // bsz, 256), lambda i: (0, i + lead, 0)) if bsz else row(256)
    scratch = [pltpu.VMEM((256 // LANES, tm, LANES), F32)] if bsz else []
    big = [_layer_weight(lw, k) for k in ("w_gate", "w_branch", "w_out", "ffn2_wgu", "ffn2_wd")]
    out_spec, out_shape = row(D_MODEL), jax.ShapeDtypeStruct((n, D_MODEL), F32)
    if out_bm:
        scratch.append(pltpu.VMEM((D_MODEL // LANES, tm, LANES), F32))
        out_spec = pl.BlockSpec((bsz, tm // bsz, D_MODEL), lambda i: (0, i, 0))
        out_shape = jax.ShapeDtypeStruct((bsz, n // bsz, D_MODEL), F32)
    return pl.pallas_call(
        functools.partial(_post_kernel, final, bsz, out_bm),
        grid=(n // tm,),
        scratch_shapes=scratch,
        in_specs=[row(D_MODEL), oa_spec, row(768), _const_spec((1, D_MODEL)), big[0][0], big[1][0], big[2][0],
                  _const_spec((1, D_MODEL)), big[3][0], big[4][0], _const_spec((1, D_MODEL))],
        out_specs=out_spec,
        out_shape=out_shape,
        compiler_params=pltpu.CompilerParams(dimension_semantics=("arbitrary",), vmem_limit_bytes=VMEM_LIMIT),
        name="post",
    )(x1, oa, obcd, lw["mix_norm"], big[0][1], big[1][1], big[2][1],
      lw["ffn2_norm"], big[3][1], big[4][1], final_norm)


def _bmm(a, b, contract=(2, 1)):
    dims = (((contract[0],), (contract[1],)), ((0,), (0,)))
    return lax.dot_general(a, b, dims, preferred_element_type=F32)


def _chunk_masks(seq_len):
    ri = lax.broadcasted_iota(jnp.int32, (CHUNK, CHUNK), 0)
    ci = lax.broadcasted_iota(jnp.int32, (CHUNK, CHUNK), 1)
    causal = ri >= ci
    strict = ri > ci
    if seq_len < CHUNK:
        same = (ri // seq_len) == (ci // seq_len)
        causal = causal & same
        strict = strict & same
    return dict(causal_f=causal.astype(F32), strict_f=strict.astype(F32), eye_f=(ri == ci).astype(F32),
                blk_f=((ri // SOLVE_BLOCK) == (ci // SOLVE_BLOCK)).astype(F32), rowseq=ri // seq_len,
                rowpos=lax.broadcasted_iota(jnp.int32, (CHUNK, 256), 0) % seq_len)


def _wy_solve(low, rhs, mk, seq_len):
    eye_f = mk["eye_f"]
    if seq_len > SOLVE_BLOCK:
        nd = low * mk["blk_f"]
        off = low - nd
        blk = SOLVE_BLOCK
    else:
        nd, off, blk = low, None, seq_len
    p = eye_f - nd
    n16 = nd.astype(BF16)
    for _ in range(blk.bit_length() - 2):
        n16 = _bmm(n16, n16).astype(BF16)
        p = p + _bmm(p.astype(BF16), n16)
    p16 = p.astype(BF16)
    y = _bmm(p16, rhs.astype(BF16))
    if off is None:
        return y
    assert seq_len // SOLVE_BLOCK == 4
    m16 = _bmm(p16, off.astype(BF16)).astype(BF16)
    y2 = y + _bmm(_bmm(m16, m16).astype(BF16), y.astype(BF16))
    return y2 - _bmm(m16, y2.astype(BF16))


def _delta_elementwise(chunks, mk, seq_len):
    qs, ks, vs, bs, gcs = [], [], [], [], []
    for q_in, k_in, v_in, beta, gl in chunks:
        gc = gl
        shift = 1
        while shift < seq_len:
            gc = gc + jnp.where(mk["rowpos"] >= shift, pltpu.roll(gc, shift, axis=0), 0.0)
            shift *= 2
        for h in range(HEADS):
            sl = slice(h * DK, (h + 1) * DK)
            qs.append(q_in[:, sl])
            ks.append(k_in[:, sl])
            vs.append(v_in[:, sl])
            bs.append(beta[:, sl])
            gcs.append(gc[:, sl])
    q, k, v, b, gc = [jnp.stack(a, axis=0) for a in (qs, ks, vs, bs, gcs)]
    q = q * lax.rsqrt(jnp.sum(q * q, axis=-1, keepdims=True) + EPS) * (DK ** -0.5)
    k = k * lax.rsqrt(jnp.sum(k * k, axis=-1, keepdims=True) + EPS)
    gct = jnp.sum(gc * mk["eye_f"], axis=1, keepdims=True)
    dec = jnp.exp(jnp.minimum(gc - gct, 0.0))
    eg = jnp.exp(gc)
    kb = k * b
    if seq_len == CHUNK:
        glast = gc[:, CHUNK - 1:CHUNK, :]
    else:
        glast = jnp.concatenate(
            [jnp.broadcast_to(gc[:, (j + 1) * seq_len - 1:(j + 1) * seq_len, :], (gc.shape[0], seq_len, DK))
             for j in range(CHUNK // seq_len)], axis=1)
    return dict(kb16=kb.astype(BF16), k16=k.astype(BF16), q16=q.astype(BF16), dec=dec,
                rhs=jnp.concatenate([v * b, kb * eg], axis=2), qe16=(q * eg).astype(BF16),
                kd16=(k * jnp.exp(glast - gc)).astype(BF16), egl=jnp.exp(glast))


def _delta_solve_store(ew, mk, seq_len, scr, slots):
    u_s, w_s, qe_s, kd_s, a_s, egl_s = scr
    kk = _bmm(ew["kb16"], ew["k16"], (2, 2))
    qk = _bmm(ew["q16"], ew["k16"], (2, 2))
    low = kk * (ew["dec"] * mk["strict_f"])
    x = _wy_solve(low, ew["rhs"], mk, seq_len)
    w16 = x[:, :, DK:2 * DK].astype(BF16)
    a16 = (qk * (ew["dec"] * mk["causal_f"])).astype(BF16)
    qe16, kd16, egl = ew["qe16"], ew["kd16"], ew["egl"]
    for c, (lead, slot) in enumerate(slots):
        dst = (pl.ds(lead, HEADS), pl.ds(slot * CHUNK, CHUNK), slice(None))
        ps = slice(c * HEADS, (c + 1) * HEADS)
        u_s[dst] = x[ps, :, 0:DK]
        w_s[dst] = w16[ps]
        qe_s[dst] = qe16[ps]
        kd_s[dst] = kd16[ps]
        a_s[dst] = a16[ps]
        egl_s[dst] = jnp.broadcast_to(egl[ps], (HEADS, CHUNK, DK))


def _gated_norm(o, dz, ng):
    return o * lax.rsqrt(jnp.mean(o * o, axis=-1, keepdims=True) + EPS) * ng * _silu(dz)


def _delta_scratch(lead, nslots):
    n = nslots * CHUNK
    return ([pltpu.VMEM((lead, n, DK), F32)] + [pltpu.VMEM((lead, n, DK), BF16)] * 4
            + [pltpu.VMEM((lead, n, DK), F32)])


def _delta_kernel(bsz, nchunk, q_ref, k_ref, v_ref, dz_ref, be_ref, ae_ref, s0_ref, alog_ref, dtb_ref, ng_ref,
                  o_ref, s_ref, *scr):
    @pl.when(pl.program_id(0) == 0)
    def _():
        s_ref[...] = s0_ref[...]

    u_s, w_s, qe_s, kd_s, a_s, egl_s = scr
    mk = _chunk_masks(CHUNK)
    neg_a = -jnp.exp(alog_ref[...])
    dtb = dtb_ref[...]
    ng = ng_ref[...]

    def elementwise(c):
        rows = pl.ds(c * CHUNK, CHUNK)
        chunks = []
        for b in range(bsz):
            beta = jax.nn.sigmoid(be_ref[b, rows, :])
            gl = neg_a * _softplus(ae_ref[b, rows, :] + dtb)
            chunks.append((q_ref[b, rows, :], k_ref[b, rows, :], v_ref[b, rows, :], beta, gl))
        return _delta_elementwise(chunks, mk, CHUNK)

    def state_step(c):
        rows = pl.ds(c * CHUNK, CHUNK)
        s = s_ref[...]
        s16 = s.astype(BF16)
        vnew = u_s[:, rows, :] - _bmm(w_s[:, rows, :], s16)
        vn16 = vnew.astype(BF16)
        o = _bmm(qe_s[:, rows, :], s16) + _bmm(a_s[:, rows, :], vn16)
        s_ref[...] = (s * egl_s[:, pl.ds(c * CHUNK, 8), :][:, 0:1, :]
                      + _bmm(kd_s[:, rows, :], vn16, (1, 1)))
        for b in range(bsz):
            for h in range(HEADS):
                sl = slice(h * DK, (h + 1) * DK)
                o_ref[b, rows, sl] = _gated_norm(o[b * HEADS + h], dz_ref[b, rows, sl], ng)

    ew = elementwise(0)
    for c in range(nchunk):
        ew_next = elementwise(c + 1) if c + 1 < nchunk else None
        _delta_solve_store(ew, mk, CHUNK, scr, [(b * HEADS, c) for b in range(bsz)])
        if c:
            state_step(c - 1)
        ew = ew_next
    state_step(nchunk - 1)


def _delta_call(zd, s0, lw, nchunk=DELTA_BLOCK):
    bsz, t_len, _ = zd.shape
    tb = nchunk * CHUNK
    assert t_len % tb == 0
    zspec = lambda blk: pl.BlockSpec((bsz, tb, 256), lambda i: (0, i, blk))
    sspec = pl.BlockSpec((bsz * HEADS, DK, DK), lambda i: (0, 0, 0))
    o, s = pl.pallas_call(
        functools.partial(_delta_kernel, bsz, nchunk),
        grid=(t_len // tb,),
        in_specs=[zspec(0), zspec(1), zspec(2), zspec(3), zspec(4), zspec(5), sspec,
                  _const_spec((1, 256)), _const_spec((1, 256)), _const_spec((1, DK))],
        out_specs=[pl.BlockSpec((bsz, tb, 256), lambda i: (0, i, 0)), sspec],
        out_shape=[jax.ShapeDtypeStruct((bsz, t_len, 256), F32),
                   jax.ShapeDtypeStruct((bsz * HEADS, DK, DK), F32)],
        scratch_shapes=_delta_scratch(bsz * HEADS, nchunk),
        compiler_params=pltpu.CompilerParams(dimension_semantics=("arbitrary",), vmem_limit_bytes=VMEM_LIMIT),
        name="delta",
    )(zd, zd, zd, zd, zd, zd, s0.reshape(bsz * HEADS, DK, DK), lw["dn_a_log"], lw["dn_dt_bias"], lw["dn_norm"])
    return o, s.reshape(bsz, HEADS, DK, DK)


def _delta_short_kernel(seq_len, q_ref, k_ref, v_ref, dz_ref, be_ref, ae_ref, buf_ref, s0_ref, cw_ref,
                        alog_ref, dtb_ref, ng_ref, o_ref, nbuf_ref, sfin_ref, xs_ref, *scr):
    nseq = CHUNK // seq_len
    span = seq_len + 8
    mk = _chunk_masks(seq_len)
    cw = cw_ref[...]
    xs_ref[...] = jnp.zeros(xs_ref.shape, F32)
    for j in range(nseq):
        xs_ref[j * span + 5:j * span + 8, :] = buf_ref[j]
        rows = slice(j * seq_len, (j + 1) * seq_len)
        xs_ref[j * span + 8:(j + 1) * span, 0:256] = q_ref[rows, :]
        xs_ref[j * span + 8:(j + 1) * span, 256:512] = k_ref[rows, :]
        xs_ref[j * span + 8:(j + 1) * span, 512:768] = v_ref[rows, :]
    ys = []
    for j in range(nseq):
        win = xs_ref[j * span:(j + 1) * span, :]
        ys.append(cw[3:4] * win[8:span] + cw[2:3] * win[7:span - 1]
                  + cw[1:2] * win[6:span - 2] + cw[0:1] * win[5:span - 3])
        nbuf_ref[j] = win[span - 3:span, :]
    qkv = _silu(jnp.concatenate(ys, axis=0))
    beta = jax.nn.sigmoid(be_ref[...])
    gl = -jnp.exp(alog_ref[...]) * _softplus(ae_ref[...] + dtb_ref[...])
    ew = _delta_elementwise([(qkv[:, 0:256], qkv[:, 256:512], qkv[:, 512:768], beta, gl)], mk, seq_len)
    _delta_solve_store(ew, mk, seq_len, scr, [(0, 0)])

    u_s, w_s, qe_s, kd_s, a_s, egl_s = scr
    w = w_s[...]
    qe = qe_s[...]
    kd = kd_s[...].astype(F32)
    u = u_s[...]
    ws, qs = [], []
    for j in range(nseq):
        s16 = s0_ref[j].astype(BF16)
        ws.append(_bmm(w, s16))
        qs.append(_bmm(qe, s16))
    vnew, o = u, jnp.zeros_like(u)
    for j in range(nseq):
        mine = mk["rowseq"] == j
        vnew = jnp.where(mine, u - ws[j], vnew)
        o = jnp.where(mine, qs[j], o)
    vn16 = vnew.astype(BF16)
    o = o + _bmm(a_s[...], vn16)
    for j in range(nseq):
        kdj = jnp.where(mk["rowseq"] == j, kd, 0.0).astype(BF16)
        sfin_ref[j] = (s0_ref[j] * egl_s[:, j * seq_len:(j + 1) * seq_len, :][:, 0:1, :]
                       + _bmm(kdj, vn16, (1, 1)))
    ng = ng_ref[...]
    for h in range(HEADS):
        sl = slice(h * DK, (h + 1) * DK)
        o_ref[:, sl] = _gated_norm(o[h], dz_ref[:, sl], ng)


def _delta_short_call(z, seq_len, buf, s0, lw):
    n = z.shape[0]
    nseq = CHUNK // seq_len
    zspec = lambda blk: pl.BlockSpec((CHUNK, 256), lambda g: (g, blk))
    bspec = pl.BlockSpec((nseq, DN_K - 1, QKV_W), lambda g: (g, 0, 0))
    sspec = pl.BlockSpec((nseq, HEADS, DK, DK), lambda g: (g, 0, 0, 0))
    return pl.pallas_call(
        functools.partial(_delta_short_kernel, seq_len),
        grid=(n // CHUNK,),
        in_specs=[zspec(0), zspec(1), zspec(2), zspec(3), zspec(4), zspec(5), bspec, sspec, _const_spec((DN_K, QKV_W)), _const_spec((1, 256)), _const_spec((1, 256)),
                  _const_spec((1, DK))],
        out_specs=[pl.BlockSpec((CHUNK, 256), lambda g: (g, 0)), bspec, sspec],
        out_shape=[jax.ShapeDtypeStruct((n, 256), F32), jax.ShapeDtypeStruct(buf.shape, F32),
                   jax.ShapeDtypeStruct(s0.shape, F32)],
        scratch_shapes=[pltpu.VMEM((nseq * (seq_len + 8), QKV_W), F32)] + _delta_scratch(HEADS, 1),
        compiler_params=pltpu.CompilerParams(dimension_semantics=("arbitrary",), vmem_limit_bytes=VMEM_LIMIT),
        name="delta_short",
    )(z, z, z, z, z, z, buf, s0, lw["dn_conv_w"], lw["dn_a_log"], lw["dn_dt_bias"], lw["dn_norm"])


def _scan_time_major(t_len, bsz, state_refs, step):
    def run_group(goff):
        hs = tuple(r[pl.ds(goff, 8), :] for r in state_refs)
        if t_len <= 8:
            for t in range(t_len):
                hs = step(hs, t * bsz + goff)
        else:
            def body(t, hs):
                return step(hs, pl.multiple_of(t * bsz + goff, 8))
            hs = lax.fori_loop(0, t_len, body, hs, unroll=3)
        for r, h in zip(state_refs, hs):
            r[pl.ds(goff, 8), :] = h

    if bsz == 8:
        run_group(0)
    else:
        def gbody(g, carry):
            run_group(pl.multiple_of(g * 8, 8))
            return carry
        lax.fori_loop(0, bsz // 8, gbody, 0)


def _bcd_kernel(t_len, bsz, nsteps,
                lru_ref, cv_ref, s5_ref, s5re0, s5im0, lru0, lbuf0, cbuf0,
                lam_re_ref, lam_im_ref, lstep_ref, wb_ref, wcre_ref, wcim_ref, dskip_ref, wglu_ref, bglu_ref,
                lcw_ref, lcb_ref, wa_ref, ba_ref, wx_ref, bx_ref, llam_ref,
                ccw_ref, ccb_ref, lng_ref, lnb_ref,
                o_ref, s5re_o, s5im_o, lru_o, lbuf_o, cbuf_o,
                xr, xi, lxs, cxs, a_s, b_s, wbp):
    rows = t_len * bsz
    lb = (LRU_K - 1) * bsz
    cb = (CV_K - 1) * bsz

    @pl.when(pl.program_id(0) == 0)
    def _():
        s5re_o[...] = s5re0[...]
        s5im_o[...] = s5im0[...]
        lru_o[...] = lru0[...]
        lxs[0:lb, :] = lbuf0[...]
        cxs[0:cb, :] = cbuf0[...]

    lxs[lb:lb + rows, :] = lru_ref[:, 0:256]
    xf = lcb_ref[...] + lcw_ref[0:1, :] * lxs[0:rows, :]
    for k in range(1, LRU_K):
        xf = xf + lcw_ref[k:k + 1, :] * lxs[k * bsz:k * bsz + rows, :]
    r = jax.nn.sigmoid(_bdot(xf, wa_ref[...]) + ba_ref[...])
    i = jax.nn.sigmoid(_bdot(xf, wx_ref[...]) + bx_ref[...])
    log_a = (-LRU_C) * r * _softplus(-llam_ref[...])
    a_s[...] = jnp.exp(log_a)
    b_s[...] = jnp.sqrt(1.0 - jnp.exp(2.0 * log_a)) * (i * xf)

    def lru_step(hs, row):
        h = a_s[pl.ds(row, 8), :] * hs[0] + b_s[pl.ds(row, 8), :]
        b_s[pl.ds(row, 8), :] = h
        return (h,)
    _scan_time_major(t_len, bsz, (lru_o,), lru_step)
    o_ref[:, 256:512] = b_s[...] * jax.nn.gelu(lru_ref[:, 256:512])
    lbuf_o[...] = lxs[rows:rows + lb, :]
    if nsteps > 1:
        lxs[0:lb, :] = lxs[rows:rows + lb, :]

    dt = jnp.exp(lstep_ref[...])
    lam_re = lam_re_ref[...]
    lam_im = lam_im_ref[...]
    mag = jnp.exp(lam_re * dt)
    lb_re = mag * jnp.cos(lam_im * dt)
    lb_im = mag * jnp.sin(lam_im * dt)
    den = lam_re * lam_re + lam_im * lam_im
    cf_re = ((lb_re - 1.0) * lam_re + lb_im * lam_im) / den
    cf_im = (lb_im * lam_re - (lb_re - 1.0) * lam_im) / den

    @pl.when(pl.program_id(0) == 0)
    def _():
        b_re = wb_ref[:, 0:S5_STATES]
        b_im = wb_ref[:, S5_STATES:2 * S5_STATES]
        wbp[:, 0:S5_STATES] = (cf_re * b_re - cf_im * b_im).astype(BF16)
        wbp[:, S5_STATES:2 * S5_STATES] = (cf_re * b_im + cf_im * b_re).astype(BF16)

    u16 = s5_ref[...].astype(BF16)
    xr[...] = jnp.dot(u16, wbp[:, 0:S5_STATES], preferred_element_type=F32)
    xi[...] = jnp.dot(u16, wbp[:, S5_STATES:2 * S5_STATES], preferred_element_type=F32)
    lbr = jnp.broadcast_to(lb_re, (8, S5_STATES))
    lbi = jnp.broadcast_to(lb_im, (8, S5_STATES))

    def s5_step(hs, row):
        hr, hi = hs
        nr = lbr * hr - lbi * hi + xr[pl.ds(row, 8), :]
        ni = lbr * hi + lbi * hr + xi[pl.ds(row, 8), :]
        xr[pl.ds(row, 8), :] = nr
        xi[pl.ds(row, 8), :] = ni
        return (nr, ni)
    _scan_time_major(t_len, bsz, (s5re_o, s5im_o), s5_step)
    y = (_bdot(xr[...], wcre_ref[...]) - _bdot(xi[...], wcim_ref[...])
         + dskip_ref[...] * s5_ref[...])
    y = jax.nn.gelu(y)
    glu = _bdot(y, wglu_ref[...]) + bglu_ref[...]
    o_ref[:, 0:256] = glu[:, 0:256] * jax.nn.sigmoid(glu[:, 256:512])

    cxs[cb:cb + rows, :] = cv_ref[:, 0:256] * jax.nn.sigmoid(cv_ref[:, 256:512])
    yc = ccb_ref[...] + ccw_ref[0:1, :] * cxs[0:rows, :]
    for k in range(1, CV_K):
        yc = yc + ccw_ref[k:k + 1, :] * cxs[k * bsz:k * bsz + rows, :]
    mu = jnp.mean(yc, axis=-1, keepdims=True)
    ycc = yc - mu
    yn = ycc * lax.rsqrt(jnp.mean(ycc * ycc, axis=-1, keepdims=True) + EPS) * lng_ref[...] + lnb_ref[...]
    o_ref[:, 512:768] = _silu(yn)
    cbuf_o[...] = cxs[rows:rows + cb, :]
    if nsteps > 1:
        cxs[0:cb, :] = cxs[rows:rows + cb, :]


def _bcd_call(zmix, states, lw, t_len, bsz, tb):
    col_blocks = BCD_COL_BLOCKS
    n = zmix.shape[0]
    nsteps = t_len // tb
    rows = tb * bsz
    lb = (LRU_K - 1) * bsz
    cb = (CV_K - 1) * bsz
    assert nsteps == 1 or rows >= cb
    zspec = lambda w, blk: pl.BlockSpec((rows, w), lambda i: (i, blk))
    state_shapes = [(bsz, S5_STATES), (bsz, S5_STATES), (bsz, 256), (lb, 256), (cb, 256)]
    params = [lw["s5_lam_re"], lw["s5_lam_im"], lw["s5_log_step"], lw["s5_wb"], lw["s5_wcre"], lw["s5_wcim"],
              lw["s5_d"], lw["s5_w_glu"], lw["s5_b_glu"],
              lw["lru_conv_w"], lw["lru_conv_b"], lw["lru_wa"], lw["lru_b_a"], lw["lru_wx"], lw["lru_b_x"],
              lw["lru_lam"], lw["cv_conv_w"], lw["cv_conv_b"], lw["cv_ln_g"], lw["cv_ln_b"]]
    return pl.pallas_call(
        functools.partial(_bcd_kernel, tb, bsz, nsteps),
        grid=(nsteps,),
        in_specs=([zspec(512, col_blocks[0]), zspec(512, col_blocks[1]), zspec(256, col_blocks[2])]
                  + [_const_spec(s) for s in state_shapes]
                  + [_const_spec(p.shape) for p in params]),
        out_specs=[pl.BlockSpec((rows, 768), lambda i: (i, 0))]
                  + [pl.BlockSpec(s, lambda i: (0, 0)) for s in state_shapes],
        out_shape=[jax.ShapeDtypeStruct((n, 768), F32)]
                  + [jax.ShapeDtypeStruct(s, F32) for s in state_shapes],
        scratch_shapes=[pltpu.VMEM((rows, S5_STATES), F32), pltpu.VMEM((rows, S5_STATES), F32),
                        pltpu.VMEM((lb + rows, 256), F32), pltpu.VMEM((cb + rows, 256), F32),
                        pltpu.VMEM((rows, 256), F32), pltpu.VMEM((rows, 256), F32),
                        pltpu.VMEM((256, 2 * S5_STATES), BF16)],
        compiler_params=pltpu.CompilerParams(dimension_semantics=("arbitrary",), vmem_limit_bytes=VMEM_LIMIT),
        name="bcd",
    )(zmix, zmix, zmix, *states, *params)


def _block_diag(m):
    g, r, c = m.shape
    return (jnp.eye(g, dtype=m.dtype)[:, None, :, None] * m[:, :, None, :]).reshape(g * r, g * c)


def _stacked_weights(p):
    w_in = p["w_in"].astype(BF16)
    w_mix = jnp.concatenate(
        [w_in[..., 0:1024], jnp.repeat(w_in[..., 1024:1028], DK, axis=-1),
         jnp.repeat(w_in[..., 1028:1032], DK, axis=-1),
         w_in[..., 1288:1800], w_in[..., 1800:2312], w_in[..., 1032:1288]], axis=-1)
    return dict(ffn1_wgu=p["ffn1_w_gu"].astype(BF16), ffn1_wd=p["ffn1_w_down"].astype(BF16),
                ffn2_wgu=p["ffn2_w_gu"].astype(BF16), ffn2_wd=p["ffn2_w_down"].astype(BF16),
                w_mix=w_mix, w_gate=w_in[..., 2312:],
                w_branch=p["w_branch"].astype(BF16), w_out=p["w_out"].astype(BF16))


def _layer_weights(l, p, stacked):
    row = lambda v: v.reshape(1, -1).astype(F32)
    lw = dict(layer=l, stacked=stacked)
    lw["ffn1_norm"] = row(p["ffn1_norm"][l])
    lw["ffn2_norm"] = row(p["ffn2_norm"][l])
    lw["mix_norm"] = row(p["mix_norm"][l])
    lw["dn_conv_w"] = p["dn_conv_w"][l]
    lw["dn_a_log"] = row(jnp.repeat(p["dn_a_log"][l], DK))
    lw["dn_dt_bias"] = row(jnp.repeat(p["dn_dt_bias"][l], DK))
    lw["dn_norm"] = row(p["dn_norm"][l])
    lw["s5_lam_re"] = row(p["s5_lam_re"][l])
    lw["s5_lam_im"] = row(p["s5_lam_im"][l])
    lw["s5_log_step"] = row(jnp.repeat(p["s5_log_step"][l], 64))
    bdt = lambda w: _block_diag(jnp.swapaxes(w, 1, 2))
    lw["s5_wb"] = jnp.concatenate([bdt(p["s5_b_re"][l]), bdt(p["s5_b_im"][l])], axis=1).astype(F32)
    lw["s5_wcre"] = bdt(p["s5_c_re"][l]).astype(BF16)
    lw["s5_wcim"] = bdt(p["s5_c_im"][l]).astype(BF16)
    lw["s5_d"] = row(p["s5_d"][l])
    lw["s5_w_glu"] = p["s5_w_glu"][l].astype(BF16)
    lw["s5_b_glu"] = row(p["s5_b_glu"][l])
    lw["lru_conv_w"] = p["lru_conv_w"][l]
    lw["lru_conv_b"] = row(p["lru_conv_b"][l])
    lw["lru_wa"] = _block_diag(p["lru_w_a"][l]).astype(BF16)
    lw["lru_wx"] = _block_diag(p["lru_w_x"][l]).astype(BF16)
    lw["lru_b_a"] = row(p["lru_b_a"][l])
    lw["lru_b_x"] = row(p["lru_b_x"][l])
    lw["lru_lam"] = row(p["lru_lam"][l])
    lw["cv_conv_w"] = p["cv_conv_w"][l]
    lw["cv_conv_b"] = row(p["cv_conv_b"][l])
    lw["cv_ln_g"] = row(p["cv_ln_g"][l])
    lw["cv_ln_b"] = row(p["cv_ln_b"][l])
    return lw


def _to_time_major(a):
    bsz, k, c = a.shape
    return jnp.transpose(a, (1, 0, 2)).reshape(k * bsz, c)


def _from_time_major(a, bsz):
    k = a.shape[0] // bsz
    return jnp.transpose(a.reshape(k, bsz, a.shape[1]), (1, 0, 2))


def _bcd_states(st, bsz):
    _, _, s_re, s_im, s_lru, s_lruc, s_cv = st
    return (s_re.reshape(bsz, S5_STATES), s_im.reshape(bsz, S5_STATES), s_lru,
            _to_time_major(s_lruc), _to_time_major(s_cv))


def _new_states(n_dn, n_dnc, bcd_new, bsz):
    n_re, n_im, n_lru, n_lruc, n_cv = bcd_new
    return (n_dn, n_dnc, n_re.reshape(bsz, 16, 64), n_im.reshape(bsz, 16, 64), n_lru,
            _from_time_major(n_lruc, bsz), _from_time_major(n_cv, bsz))


def _layer_long(x, st, lw, final_norm, final, t_len, bsz, tm, tb, prefix=None):
    pad = (-t_len) % (DELTA_BLOCK * CHUNK)
    lead, rem = divmod(pad * bsz, tm)
    assert rem == 0
    x1, zd, zb, n_dnc = _pre_call(x, lw, tm, bsz, _to_time_major(st[1]), lead, prefix)
    n_dnc = _from_time_major(n_dnc, bsz)
    oa, n_dn = _delta_call(zd, st[0], lw)
    obcd, *bcd_new = _bcd_call(zb, _bcd_states(st, bsz), lw, t_len, bsz, tb)
    x3 = _post_call(x1, oa, obcd, lw, final_norm, final, tm, bsz, lead, out_bm=final)
    return x3, _new_states(n_dn, n_dnc, bcd_new, bsz)


def _layer_short(x, st, lw, final_norm, final, t_len, bsz, tm):
    x1, zd, zb = _pre_call(x, lw, tm)
    oa, n_dnc, n_dn = _delta_short_call(zd, t_len, st[1], st[0], lw)
    z_tm = _to_time_major(zb.reshape(bsz, t_len, BCD_COLS))
    obcd_tm, *bcd_new = _bcd_call(z_tm, _bcd_states(st, bsz), lw, t_len, bsz, t_len)
    obcd = _from_time_major(obcd_tm, bsz).reshape(bsz * t_len, 768)
    x3 = _post_call(x1, oa, obcd, lw, final_norm, final, tm)
    return x3, _new_states(n_dn, n_dnc, bcd_new, bsz)


def _zero_state(bsz):
    return (jnp.zeros((bsz, HEADS, DK, DK), F32), jnp.zeros((bsz, DN_K - 1, QKV_W), F32),
            jnp.zeros((bsz, 16, 64), F32), jnp.zeros((bsz, 16, 64), F32), jnp.zeros((bsz, 256), F32),
            jnp.zeros((bsz, LRU_K - 1, 256), F32), jnp.zeros((bsz, CV_K - 1, 256), F32))


def kernel(x_prompt, x_sample, state_delta, state_delta_conv, state_s5_re, state_s5_im, state_lru, state_lru_conv, state_conv, meta_tokens, ffn1_norm, ffn1_w_gu, ffn1_w_down, mix_norm, w_in, dn_conv_w, dn_a_log, dn_dt_bias, dn_norm, s5_lam_re, s5_lam_im, s5_log_step, s5_b_re, s5_b_im, s5_c_re, s5_c_im, s5_d, s5_w_glu, s5_b_glu, lru_conv_w, lru_conv_b, lru_w_a, lru_b_a, lru_w_x, lru_b_x, lru_lam, cv_conv_w, cv_conv_b, cv_ln_g, cv_ln_b, w_branch, w_out, ffn2_norm, ffn2_w_gu, ffn2_w_down, final_norm):
    p = dict(ffn1_norm=ffn1_norm, ffn1_w_gu=ffn1_w_gu, ffn1_w_down=ffn1_w_down, mix_norm=mix_norm, w_in=w_in,
             dn_conv_w=dn_conv_w, dn_a_log=dn_a_log, dn_dt_bias=dn_dt_bias, dn_norm=dn_norm,
             s5_lam_re=s5_lam_re, s5_lam_im=s5_lam_im, s5_log_step=s5_log_step, s5_b_re=s5_b_re,
             s5_b_im=s5_b_im, s5_c_re=s5_c_re, s5_c_im=s5_c_im, s5_d=s5_d, s5_w_glu=s5_w_glu,
             s5_b_glu=s5_b_glu, lru_conv_w=lru_conv_w, lru_conv_b=lru_conv_b, lru_w_a=lru_w_a,
             lru_b_a=lru_b_a, lru_w_x=lru_w_x, lru_b_x=lru_b_x, lru_lam=lru_lam, cv_conv_w=cv_conv_w,
             cv_conv_b=cv_conv_b, cv_ln_g=cv_ln_g, cv_ln_b=cv_ln_b, w_branch=w_branch, w_out=w_out,
             ffn2_norm=ffn2_norm, ffn2_w_gu=ffn2_w_gu, ffn2_w_down=ffn2_w_down)
    depth = w_in.shape[0]
    bp, seq, _ = x_prompt.shape
    bs, dseq, _ = x_sample.shape
    tp = seq + N_META
    fnorm = final_norm.reshape(1, D_MODEL)

    xp = x_prompt
    xs = x_sample.reshape(bs * dseq, D_MODEL)

    p_new, s_new = [], []
    stacked = _stacked_weights(p)
    for l in range(depth):
        lw = _layer_weights(l, p, stacked)
        final = l == depth - 1
        xp, st_p = _layer_long(xp, _zero_state(bp), lw, fnorm, final, tp, bp, tm=LONG_TILE_ROWS, tb=LONG_SCAN_STEPS,
                               prefix=meta_tokens if l == 0 else None)
        st_s = (state_delta[l], state_delta_conv[l], state_s5_re[l], state_s5_im[l], state_lru[l],
                state_lru_conv[l], state_conv[l])
        xs, st_s = _layer_short(xs, st_s, lw, fnorm, final, dseq, bs, tm=SHORT_TILE_ROWS)
        p_new.append(st_p)
        s_new.append(st_s)

    y_prompt = xp[:, N_META:]
    y_sample = xs.reshape(bs, dseq, D_MODEL)
    stack = lambda new, i: jnp.stack([st[i] for st in new], axis=0)
    return (y_prompt, y_sample, *[stack(p_new, i) for i in range(7)], *[stack(s_new, i) for i in range(7)])
```
